```python
import math, functools
import jax, jax.numpy as jnp
from jax import lax
import numpy as np

D_MODEL = 1024
BATCH = 8
SEQ = 2048
DEPTH = 2

N_A_LAYERS = DEPTH // 2
N_B_LAYERS = DEPTH - N_A_LAYERS
GDN_HEADS = D_MODEL // 128
GDN_DK = 128
GDN_DV = 128
GDN_WIDTH = GDN_HEADS * GDN_DV
CONV_WIDTH = 4
CHUNK = 64
DIFF_HEADS = D_MODEL // 128
DIFF_DK = 64
DIFF_DV = 2 * DIFF_DK
Q_BLOCK = 128
D_FF = 4 * D_MODEL
ROPE_THETA = 10000.0
EPS = 1e-6
GDN_IN_WIDTH = 4 * GDN_WIDTH + 2 * GDN_HEADS
Q_WIDTH = DIFF_HEADS * 2 * DIFF_DK
KV_WIDTH = DIFF_HEADS * 2 * DIFF_DK + DIFF_HEADS * DIFF_DV

kernel_name = 'yoco_gdn_diffattn_hybrid'


def rms_norm(x, w):
    xf = x.astype(jnp.float32)
    y = xf * lax.rsqrt(jnp.mean(xf * xf, axis=-1, keepdims=True) + EPS)
    return (y * w.astype(jnp.float32)).astype(x.dtype)


def l2_norm(x):
    return x * lax.rsqrt(jnp.sum(x * x, axis=-1, keepdims=True) + EPS)


def rope(t, positions):
    half = t.shape[-1] // 2
    freqs = ROPE_THETA ** (-jnp.arange(half, dtype=jnp.float32) / half)
    ang = positions.astype(jnp.float32)[:, :, None] * freqs
    ang = ang.reshape(ang.shape[:2] + (1,) * (t.ndim - 3) + (half,))
    cos, sin = jnp.cos(ang), jnp.sin(ang)
    tf = t.astype(jnp.float32)
    t1, t2 = tf[..., :half], tf[..., half:]
    return jnp.concatenate([t1 * cos - t2 * sin, t2 * cos + t1 * sin], axis=-1).astype(t.dtype)


def causal_conv(x, w):
    width = w.shape[0]
    seq = x.shape[1]
    xp = jnp.pad(x, ((0, 0), (width - 1, 0), (0, 0)))
    return sum(w[j] * xp[:, j:j + seq] for j in range(width))


def chunked_gated_delta_rule(q, k, v, beta, g):
    b, s, h, dk = q.shape
    dv = v.shape[-1]
    nc = s // CHUNK

    def chunks(t):
        return t.reshape((b, nc, CHUNK, h) + t.shape[3:]).swapaxes(2, 3)

    q, k, v, beta, g = chunks(q), chunks(k), chunks(v), chunks(beta), chunks(g)
    g_cum = jnp.cumsum(g, axis=-1)
    causal = jnp.tril(jnp.ones((CHUNK, CHUNK), dtype=bool))
    strict = jnp.tril(jnp.ones((CHUNK, CHUNK), dtype=bool), k=-1)
    decay = jnp.where(causal, jnp.exp(jnp.where(causal, g_cum[..., :, None] - g_cum[..., None, :], 0.0)), 0.0)
    k_beta = k * beta[..., None]
    kk = jnp.einsum('bnhcd,bnhed->bnhce', k_beta, k) * decay
    tri = jnp.where(strict, kk, 0.0) + jnp.eye(CHUNK, dtype=jnp.float32)
    solve = functools.partial(lax.linalg.triangular_solve, left_side=True, lower=True, unit_diagonal=True)
    u = solve(tri, v * beta[..., None])
    w = solve(tri, k_beta * jnp.exp(g_cum)[..., None])
    intra = jnp.einsum('bnhcd,bnhed->bnhce', q, k) * decay
    q_dec = q * jnp.exp(g_cum)[..., None]
    k_dec = k * jnp.exp(g_cum[..., -1:] - g_cum)[..., None]
    chunk_decay = jnp.exp(g_cum[..., -1])
    xs = tuple(jnp.moveaxis(t, 1, 0) for t in (q_dec, k_dec, w, u, intra, chunk_decay))

    def step(state, inp):
        qd, kd, wc, uc, ic, cd = inp
        v_new = uc - jnp.einsum('bhcd,bhdv->bhcv', wc, state)
        out = jnp.einsum('bhcd,bhdv->bhcv', qd, state) + jnp.einsum('bhce,bhev->bhcv', ic, v_new)
        state = state * cd[..., None, None] + jnp.einsum('bhcd,bhcv->bhdv', kd, v_new)
        return state, out

    state0 = jnp.zeros((b, h, dk, dv), jnp.float32)
    _, o = lax.scan(step, state0, xs)
    return o.transpose(1, 0, 3, 2, 4).reshape(b, s, h, dv)


def gated_deltanet(x, norm_w, w_in, conv_w, a_log, dt_bias, out_norm, w_out):
    b, s, _ = x.shape
    f32 = jnp.float32
    proj = rms_norm(x, norm_w) @ w_in
    qkv = jax.nn.silu(causal_conv(proj[..., :3 * GDN_WIDTH], conv_w)).astype(f32)
    z = proj[..., 3 * GDN_WIDTH:4 * GDN_WIDTH].astype(f32)
    b_raw = proj[..., 4 * GDN_WIDTH:4 * GDN_WIDTH + GDN_HEADS].astype(f32)
    a_raw = proj[..., 4 * GDN_WIDTH + GDN_HEADS:].astype(f32)
    q, k, v = jnp.split(qkv, 3, axis=-1)
    q = l2_norm(q.reshape(b, s, GDN_HEADS, GDN_DK)) * (GDN_DK ** -0.5)
    k = l2_norm(k.reshape(b, s, GDN_HEADS, GDN_DK))
    v = v.reshape(b, s, GDN_HEADS, GDN_DV)
    beta = jax.nn.sigmoid(b_raw)
    g = -jnp.exp(a_log.astype(f32)) * jax.nn.softplus(a_raw + dt_bias.astype(f32))
    o = chunked_gated_delta_rule(q, k, v, beta, g)
    o = rms_norm(o, out_norm) * jax.nn.silu(z.reshape(b, s, GDN_HEADS, GDN_DV))
    return o.reshape(b, s, GDN_WIDTH).astype(x.dtype) @ w_out


def shared_kv(x, kv_norm, w_kv, k_norm, positions):
    b, s, _ = x.shape
    kv = rms_norm(x, kv_norm) @ w_kv
    k = kv[..., :Q_WIDTH].reshape(b, s, DIFF_HEADS, 2, DIFF_DK)
    v = kv[..., Q_WIDTH:].reshape(b, s, DIFF_HEADS, DIFF_DV)
    k = rope(rms_norm(k, k_norm), positions)
    return k.transpose(0, 2, 1, 3, 4), v.transpose(0, 2, 1, 3)


def diff_attention(x, positions, k, v, norm_w, w_q, q_norm, lam_params, sub_norm, w_out, lam_init):
    b, s, _ = x.shape
    f32 = jnp.float32
    q = (rms_norm(x, norm_w) @ w_q).reshape(b, s, DIFF_HEADS, 2, DIFF_DK)
    q = rope(rms_norm(q, q_norm), positions).transpose(0, 2, 1, 3, 4)
    lp = lam_params.astype(f32)
    lam = jnp.exp(jnp.sum(lp[0] * lp[1])) - jnp.exp(jnp.sum(lp[2] * lp[3])) + lam_init
    scale = DIFF_DK ** -0.5
    outs = []
    for blk in range(s // Q_BLOCK):
        start, end = blk * Q_BLOCK, (blk + 1) * Q_BLOCK
        scores = jnp.einsum('bhqmd,bhkmd->bhmqk', q[:, :, start:end], k[:, :, :end]).astype(f32) * scale
        causal = jnp.arange(end)[None, :] <= (start + jnp.arange(Q_BLOCK))[:, None]
        probs = jax.nn.softmax(jnp.where(causal, scores, -1e30), axis=-1)
        diff = probs[:, :, 0] - lam * probs[:, :, 1]
        outs.append(jnp.einsum('bhqk,bhkv->bhqv', diff, v[:, :, :end].astype(f32)))
    o = jnp.concatenate(outs, axis=2)
    o = rms_norm(o, sub_norm) * (1.0 - lam_init)
    return o.transpose(0, 2, 1, 3).reshape(b, s, DIFF_HEADS * DIFF_DV).astype(x.dtype) @ w_out


def sq_relu_mlp(x, norm_w, w1, w2):
    h = jax.nn.relu(rms_norm(x, norm_w) @ w1)
    return (h * h) @ w2


def setup_inputs(seed: int = 0) -> dict:
    key = jax.random.key(seed)
    keys = jax.random.split(key, 21)
    f32 = jnp.float32

    def normal(k, shape, scale):
        return jax.random.normal(k, shape, f32) * scale

    def gain(k, shape):
        return 1.0 + 0.05 * jax.random.normal(k, shape, f32)

    x = normal(keys[0], (BATCH, SEQ, D_MODEL), 1.0)
    offset = jax.random.randint(keys[1], (BATCH, 1), 0, 4096, dtype=jnp.int32)
    positions = offset + jnp.arange(SEQ, dtype=jnp.int32)[None, :]
    a_norm = gain(keys[2], (N_A_LAYERS, D_MODEL))
    a_w_in = normal(keys[3], (N_A_LAYERS, D_MODEL, GDN_IN_WIDTH), D_MODEL ** -0.5)
    a_conv_w = normal(keys[4], (N_A_LAYERS, CONV_WIDTH, 3 * GDN_WIDTH), CONV_WIDTH ** -0.5)
    a_a_log = jnp.log(jax.random.uniform(keys[5], (N_A_LAYERS, GDN_HEADS), f32, 1.0, 16.0))
    dt = jnp.exp(jax.random.uniform(keys[6], (N_A_LAYERS, GDN_HEADS), f32, math.log(1e-3), math.log(1e-1)))
    a_dt_bias = dt + jnp.log(-jnp.expm1(-dt))
    a_out_norm = gain(keys[7], (N_A_LAYERS, GDN_DV))
    a_w_out = normal(keys[8], (N_A_LAYERS, GDN_WIDTH, D_MODEL), GDN_WIDTH ** -0.5)
    kv_norm = gain(keys[9], (D_MODEL,))
    w_kv = normal(keys[10], (D_MODEL, KV_WIDTH), D_MODEL ** -0.5)
    k_norm = gain(keys[11], (DIFF_DK,))
    b_norm = gain(keys[12], (N_B_LAYERS, D_MODEL))
    b_w_q = normal(keys[13], (N_B_LAYERS, D_MODEL, Q_WIDTH), D_MODEL ** -0.5)
    b_q_norm = gain(keys[14], (N_B_LAYERS, DIFF_DK))
    b_lambda = normal(keys[15], (N_B_LAYERS, 4, DIFF_DK), 0.1)
    b_sub_norm = gain(keys[16], (N_B_LAYERS, DIFF_DV))
    b_w_out = normal(keys[17], (N_B_LAYERS, DIFF_HEADS * DIFF_DV, D_MODEL), (DIFF_HEADS * DIFF_DV) ** -0.5)
    mlp_norm = gain(keys[18], (DEPTH, D_MODEL))
    mlp_w1 = normal(keys[19], (DEPTH, D_MODEL, D_FF), D_MODEL ** -0.5)
    mlp_w2 = normal(keys[20], (DEPTH, D_FF, D_MODEL), 0.5 * D_FF ** -0.5)
    return {'x': x, 'positions': positions,
            'a_norm': a_norm, 'a_w_in': a_w_in, 'a_conv_w': a_conv_w, 'a_a_log': a_a_log,
            'a_dt_bias': a_dt_bias, 'a_out_norm': a_out_norm, 'a_w_out': a_w_out,
            'kv_norm': kv_norm, 'w_kv': w_kv, 'k_norm': k_norm,
            'b_norm': b_norm, 'b_w_q': b_w_q, 'b_q_norm': b_q_norm, 'b_lambda': b_lambda,
            'b_sub_norm': b_sub_norm, 'b_w_out': b_w_out,
            'mlp_norm': mlp_norm, 'mlp_w1': mlp_w1, 'mlp_w2': mlp_w2}


def reference(x, positions, a_norm, a_w_in, a_conv_w, a_a_log, a_dt_bias, a_out_norm, a_w_out,
              kv_norm, w_kv, k_norm, b_norm, b_w_q, b_q_norm, b_lambda, b_sub_norm, b_w_out,
              mlp_norm, mlp_w1, mlp_w2):
    k_shared, v_shared = None, None
    for layer in range(DEPTH):
        if layer < N_A_LAYERS:
            x = x + gated_deltanet(x, a_norm[layer], a_w_in[layer], a_conv_w[layer], a_a_log[layer],
                                   a_dt_bias[layer], a_out_norm[layer], a_w_out[layer])
        else:
            if layer == N_A_LAYERS:
                k_shared, v_shared = shared_kv(x, kv_norm, w_kv, k_norm, positions)
            j = layer - N_A_LAYERS
            lam_init = 0.8 - 0.6 * math.exp(-0.3 * layer)
            x = x + diff_attention(x, positions, k_shared, v_shared, b_norm[j], b_w_q[j], b_q_norm[j],
                                   b_lambda[j], b_sub_norm[j], b_w_out[j], lam_init)
        x = x + sq_relu_mlp(x, mlp_norm[layer], mlp_w1[layer], mlp_w2[layer])
    return x
```

```python
import functools
import math

import jax
import jax.numpy as jnp
from jax import lax
from jax.experimental import pallas as pl
from jax.experimental.pallas import tpu as pltpu

F32 = jnp.float32
BF16 = jnp.bfloat16

D_MODEL = 1024
HEADS = 8
HEAD_DIM = 128
MAP_DIM = 64
ROPE_HALF = MAP_DIM // 2
CONV_WIDTH = 4
D_FF = 4 * D_MODEL
ROPE_THETA = 10000.0
EPS = 1e-6
NEG_INF = -1e30

LANES = 128
SUBLANES = 8
MXU_DIM = 256

ROW_TILE = 512
FF_CHUNK = 1024
DELTA_CHUNK = 64
DELTA_UNROLL = 2
ATTN_TQ = 512
ATTN_TK = 512
VMEM_LIMIT = 56 * 1024 * 1024

NT_DIMS = (((1,), (1,)), ((), ()))
TN_DIMS = (((0,), (0,)), ((), ()))


def _rms_hat(x):
    return x * lax.rsqrt(jnp.mean(x * x, axis=-1, keepdims=True) + EPS)


def _sigmoid(x):
    return 1.0 / (1.0 + jnp.exp(-x))


def _softplus(x):
    return jnp.maximum(x, 0.0) + jnp.log(1.0 + jnp.exp(-jnp.abs(x)))


def _dot(a, b):
    return jnp.dot(a, b, preferred_element_type=F32)


def _const_spec(shape):
    zeros = (0,) * len(shape)
    return pl.BlockSpec(shape, lambda *_: zeros, pipeline_mode=pl.Buffered(1))


def _gdn_in_kernel(x_ref, xh_ref, nw_ref, w_ref, wgt_ref, cw_ref, alog_ref, dtb_ref,
                   q_ref, k_ref, v_ref, z_ref, gate_ref, *, tiles_per_seq):
    tm = x_ref.shape[0]
    width = HEADS * HEAD_DIM
    nw = nw_ref[...]
    xn = (_rms_hat(x_ref[...]) * nw).astype(BF16)
    seq_start = (pl.program_id(0) % tiles_per_seq) == 0
    xh = jnp.where(seq_start, 0.0, _rms_hat(xh_ref[...]) * nw).astype(BF16)

    for ci, o_ref in enumerate((q_ref, k_ref, v_ref)):
        w = w_ref[:, ci * width:(ci + 1) * width]
        xp = jnp.concatenate([_dot(xh, w), _dot(xn, w)], axis=0)
        cw = cw_ref[:, ci * width:(ci + 1) * width]
        base = SUBLANES - (CONV_WIDTH - 1)
        c = cw[0:1, :] * xp[base:base + tm, :]
        for j in range(1, CONV_WIDTH):
            c = c + cw[j:j + 1, :] * xp[base + j:base + j + tm, :]
        a = c * _sigmoid(c)
        if ci == 2:
            o_ref[...] = a
        else:
            scale = HEAD_DIM ** -0.5 if ci == 0 else 1.0
            for h in range(HEADS):
                ah = a[:, h * HEAD_DIM:(h + 1) * HEAD_DIM]
                inv = lax.rsqrt(jnp.sum(ah * ah, axis=-1, keepdims=True) + EPS)
                o_ref[:, h * HEAD_DIM:(h + 1) * HEAD_DIM] = ah * (inv * scale)

    z_ref[...] = _dot(xn, w_ref[:, 3 * width:4 * width])

    gt = lax.dot_general(wgt_ref[...], xn, NT_DIMS, preferred_element_type=F32)
    beta = _sigmoid(gt)
    decay = -jnp.exp(alog_ref[...]) * _softplus(gt + dtb_ref[...])
    row = lax.broadcasted_iota(jnp.int32, gt.shape, 0)
    gate_ref[...] = jnp.where(row < HEADS, beta, decay)


def _gdn_in(x, norm_w, w_main, wg_t, conv_w, alog16, dtb16, seq_len):
    t, d = x.shape
    tm = ROW_TILE
    width = HEADS * HEAD_DIM
    act = jax.ShapeDtypeStruct((t, width), F32)
    row_spec = pl.BlockSpec((tm, width), lambda i: (i, 0))
    return pl.pallas_call(
        functools.partial(_gdn_in_kernel, tiles_per_seq=seq_len // tm),
        grid=(t // tm,),
        in_specs=[
            pl.BlockSpec((tm, d), lambda i: (i, 0)),
            pl.BlockSpec((SUBLANES, d), lambda i: (jnp.maximum(i * (tm // SUBLANES) - 1, 0), 0)),
            _const_spec((1, d)),
            _const_spec(w_main.shape),
            _const_spec(wg_t.shape),
            _const_spec(conv_w.shape),
            _const_spec(alog16.shape),
            _const_spec(dtb16.shape),
        ],
        out_specs=[row_spec, row_spec, row_spec, row_spec,
                   pl.BlockSpec((2 * HEADS, tm), lambda i: (0, i))],
        out_shape=[act, act, act, act, jax.ShapeDtypeStruct((2 * HEADS, t), F32)],
        compiler_params=pltpu.CompilerParams(dimension_semantics=("parallel",),
                                             vmem_limit_bytes=VMEM_LIMIT),
        name="gdn_in",
    )(x, x, norm_w, w_main, wg_t, conv_w, alog16, dtb16)


def _delta_kernel(q_ref, k_ref, v_ref, z_ref, beta_ref, g_ref, onw_ref, o_ref, gc_ref,
                  *, chunk, unroll):
    c_len = chunk
    n_chunks = q_ref.shape[0] // c_len
    row = lax.broadcasted_iota(jnp.int32, (c_len, c_len), 0)
    col = lax.broadcasted_iota(jnp.int32, (c_len, c_len), 1)
    causal = row >= col
    strict = row > col
    eye = row == col
    gc_ref[...] = jnp.dot(g_ref[0], (row <= col).astype(F32), precision=lax.Precision.HIGHEST,
                          preferred_element_type=F32)
    onw = onw_ref[...]
    n_steps = int(math.log2(c_len))

    def to_col(r):
        return jnp.sum(jnp.where(eye, r, 0.0), axis=1, keepdims=True)

    def one_chunk(c, st):
        rows = pl.ds(pl.multiple_of(c * c_len, c_len), c_len)
        q = q_ref[rows, :]
        k = k_ref[rows, :]
        v = v_ref[rows, :]
        beta_c = to_col(beta_ref[0, pl.ds(c, 1), :])
        gc_r = gc_ref[pl.ds(c, 1), :]
        gc_c = to_col(gc_r)
        gc_last = gc_r[:, c_len - 1:c_len]
        e_c = jnp.exp(gc_c)
        ek_c = jnp.exp(gc_last - gc_c)
        cd = jnp.exp(gc_last)
        decay = jnp.where(causal, jnp.exp(jnp.where(causal, gc_c - gc_r, 0.0)), 0.0)
        kb = k * beta_c
        k16 = k.astype(BF16)
        s = lax.dot_general(jnp.concatenate([kb, q], axis=0).astype(BF16), k16, NT_DIMS,
                            preferred_element_type=F32)
        neg_l = jnp.where(strict, -(s[:c_len] * decay), 0.0)
        intra = s[c_len:] * decay
        y = jnp.concatenate([v * beta_c, kb * e_c], axis=1)
        p = neg_l
        for step in range(n_steps):
            p16 = p.astype(BF16)
            y = y + _dot(p16, y.astype(BF16))
            if step + 1 < n_steps:
                p = _dot(p16, p16)
        y16 = y.astype(BF16)
        kd16 = (k * ek_c).astype(BF16)
        mb = lax.dot_general(y16, kd16, TN_DIMS, preferred_element_type=F32)
        b_t = mb[:HEAD_DIM]
        m1 = mb[HEAD_DIM:]
        iu = _dot(intra.astype(BF16), y16)
        qp = q * e_c - iu[:, HEAD_DIM:]
        st16 = st.astype(BF16)
        out = lax.dot_general(qp.astype(BF16), st16, NT_DIMS, preferred_element_type=F32) \
            + iu[:, :HEAD_DIM]
        st_new = cd * st - _dot(st16, m1.astype(BF16)) + b_t
        zc = z_ref[rows, :]
        o_ref[rows, :] = _rms_hat(out) * onw * (zc * _sigmoid(zc))
        return st_new

    def body(i, st):
        for u in range(unroll):
            st = one_chunk(i * unroll + u, st)
        return st

    lax.fori_loop(0, n_chunks // unroll, body, jnp.zeros((HEAD_DIM, HEAD_DIM), F32))


def _delta(q, k, v, z, gates, out_norm, batch, seq_len):
    t, width = q.shape
    c_len = DELTA_CHUNK
    n_chunks = seq_len // c_len
    gates3 = gates.reshape(2 * HEADS, batch * n_chunks, c_len)
    seq_spec = pl.BlockSpec((seq_len, HEAD_DIM), lambda b, h: (b, h))
    return pl.pallas_call(
        functools.partial(_delta_kernel, chunk=c_len, unroll=DELTA_UNROLL),
        grid=(batch, HEADS),
        in_specs=[seq_spec, seq_spec, seq_spec, seq_spec,
                  pl.BlockSpec((1, n_chunks, c_len), lambda b, h: (h, b, 0)),
                  pl.BlockSpec((1, n_chunks, c_len), lambda b, h: (HEADS + h, b, 0)),
                  pl.BlockSpec((1, HEAD_DIM), lambda b, h: (0, 0))],
        out_specs=seq_spec,
        out_shape=jax.ShapeDtypeStruct((t, width), F32),
        scratch_shapes=[pltpu.VMEM((n_chunks, c_len), F32)],
        compiler_params=pltpu.CompilerParams(dimension_semantics=("parallel", "parallel"),
                                             vmem_limit_bytes=VMEM_LIMIT),
        name="delta",
    )(q, k, v, z, gates3, gates3, out_norm)


def _proj_mlp_kernel(o_ref, x_ref, wo_ref, nw_ref, w1_ref, w2_ref, out_ref):
    x1 = x_ref[...] + _dot(o_ref[...].astype(BF16), wo_ref[...])
    xn = (_rms_hat(x1) * nw_ref[...]).astype(BF16)
    acc = x1
    for j in range(D_FF // FF_CHUNK):
        h = jnp.maximum(_dot(xn, w1_ref[:, j * FF_CHUNK:(j + 1) * FF_CHUNK]), 0.0)
        acc = acc + _dot((h * h).astype(BF16), w2_ref[j * FF_CHUNK:(j + 1) * FF_CHUNK, :])
    out_ref[...] = acc


def _proj_mlp(o, x, w_out, norm_w, w1, w2):
    t, d = x.shape
    tm = ROW_TILE
    return pl.pallas_call(
        _proj_mlp_kernel,
        grid=(t // tm,),
        in_specs=[pl.BlockSpec((tm, o.shape[1]), lambda i: (i, 0)),
                  pl.BlockSpec((tm, d), lambda i: (i, 0)),
                  _const_spec(w_out.shape), _const_spec((1, d)),
                  _const_spec(w1.shape), _const_spec(w2.shape)],
        out_specs=pl.BlockSpec((tm, d), lambda i: (i, 0)),
        out_shape=jax.ShapeDtypeStruct((t, d), F32),
        compiler_params=pltpu.CompilerParams(dimension_semantics=("parallel",),
                                             vmem_limit_bytes=VMEM_LIMIT),
        name="proj_mlp",
    )(o, x, w_out, norm_w, w1, w2)


def _rope_tab_kernel(pos_ref, freq_ref, cos_ref, s1_ref, s2_ref):
    ang = pos_ref[...].astype(F32) * freq_ref[...]
    sin = jnp.sin(ang)
    lane = lax.broadcasted_iota(jnp.int32, ang.shape, 1)
    first_half = (lane % MAP_DIM) < ROPE_HALF
    cos_ref[...] = jnp.cos(ang)
    s1_ref[...] = jnp.where(first_half, -sin, 0.0)
    s2_ref[...] = jnp.where(first_half, 0.0, sin)


def _rope_tab(pos_col, freq_row):
    t = pos_col.shape[0]
    tm = 2048
    tab = jax.ShapeDtypeStruct((t, LANES), F32)
    spec = pl.BlockSpec((tm, LANES), lambda i: (i, 0))
    return pl.pallas_call(
        _rope_tab_kernel,
        grid=(t // tm,),
        in_specs=[pl.BlockSpec((tm, 1), lambda i: (i, 0)), _const_spec((1, LANES))],
        out_specs=[spec, spec, spec],
        out_shape=[tab, tab, tab],
        compiler_params=pltpu.CompilerParams(dimension_semantics=("parallel",)),
        name="rope_tab",
    )(pos_col, freq_row)


def _attn_in_kernel(x_ref, kvn_ref, qnw_ref, wkv_ref, wq_ref, kg_ref, qg_ref,
                    cos_ref, s1_ref, s2_ref, k_ref, v_ref, q_ref):
    width = HEADS * HEAD_DIM
    xhat = _rms_hat(x_ref[...])
    kvn = (xhat * kvn_ref[...]).astype(BF16)
    qn = (xhat * qnw_ref[...]).astype(BF16)
    cos = cos_ref[...]
    s1 = s1_ref[...]
    s2 = s2_ref[...]
    r = lax.broadcasted_iota(jnp.int32, (MXU_DIM, MXU_DIM), 0) // MAP_DIM
    c = lax.broadcasted_iota(jnp.int32, (MXU_DIM, MXU_DIM), 1) // MAP_DIM
    group_ones = (r == c).astype(BF16)

    def norm_rope(raw, gain, scale, o_ref):
        for s in range(width // MXU_DIM):
            blk = raw[:, s * MXU_DIM:(s + 1) * MXU_DIM]
            ss = _dot((blk * blk).astype(BF16), group_ones)
            nb = blk * lax.rsqrt(ss * (1.0 / MAP_DIM) + EPS) * gain[:, s * MXU_DIM:(s + 1) * MXU_DIM]
            for hh in range(MXU_DIM // LANES):
                xb = nb[:, hh * LANES:(hh + 1) * LANES]
                rot = xb * cos + pltpu.roll(xb, LANES - ROPE_HALF, 1) * s1 + pltpu.roll(xb, ROPE_HALF, 1) * s2
                lo = s * MXU_DIM + hh * LANES
                o_ref[:, lo:lo + LANES] = (rot * scale).astype(o_ref.dtype)

    norm_rope(_dot(kvn, wkv_ref[:, :width]), kg_ref[...], 1.0, k_ref)
    v_ref[...] = _dot(kvn, wkv_ref[:, width:]).astype(v_ref.dtype)
    norm_rope(_dot(qn, wq_ref[...]), qg_ref[...], MAP_DIM ** -0.5, q_ref)


def _attn_in(x, kv_norm, q_norm_w, w_kv, w_q, k_gain, q_gain, cos, s1, s2):
    t, d = x.shape
    tm = ROW_TILE
    width = HEADS * HEAD_DIM
    act = jax.ShapeDtypeStruct((t, width), BF16)
    row_spec = pl.BlockSpec((tm, width), lambda i: (i, 0))
    tab_spec = pl.BlockSpec((tm, LANES), lambda i: (i, 0))
    return pl.pallas_call(
        _attn_in_kernel,
        grid=(t // tm,),
        in_specs=[pl.BlockSpec((tm, d), lambda i: (i, 0)),
                  _const_spec((1, d)), _const_spec((1, d)),
                  _const_spec(w_kv.shape), _const_spec(w_q.shape),
                  _const_spec((1, width)), _const_spec((1, width)),
                  tab_spec, tab_spec, tab_spec],
        out_specs=[row_spec, row_spec, row_spec],
        out_shape=[act, act, act],
        compiler_params=pltpu.CompilerParams(dimension_semantics=("parallel",),
                                             vmem_limit_bytes=VMEM_LIMIT),
        name="attn_in",
    )(x, kv_norm, q_norm_w, w_kv, w_q, k_gain, q_gain, cos, s1, s2)


def _diff_attn_kernel(q_ref, k_ref, v_ref, lam_ref, snw_ref, o_ref, m_ref, l_ref, acc_ref,
                      *, tk, lam_init):
    tq = q_ref.shape[0]
    qi = pl.program_id(2)
    q = q_ref[...]
    lane = lax.broadcasted_iota(jnp.int32, q.shape, 1)
    zero = jnp.zeros_like(q)
    qm = (jnp.where(lane < MAP_DIM, q, zero), jnp.where(lane < MAP_DIM, zero, q))

    m_ref[...] = jnp.full(m_ref.shape, NEG_INF, F32)
    l_ref[...] = jnp.zeros(l_ref.shape, F32)
    acc_ref[...] = jnp.zeros(acc_ref.shape, F32)

    def step(j, masked):
        rows = pl.ds(pl.multiple_of(j * tk, tk), tk)
        ks = k_ref[rows, :]
        vs = v_ref[rows, :]
        for mi in range(2):
            s = lax.dot_general(qm[mi], ks, NT_DIMS, preferred_element_type=F32)
            if masked:
                r = lax.broadcasted_iota(jnp.int32, s.shape, 0)
                c = lax.broadcasted_iota(jnp.int32, s.shape, 1)
                s = jnp.where(r >= c, s, NEG_INF)
            m_old = m_ref[mi]
            m_new = jnp.maximum(m_old, jnp.max(s, axis=-1, keepdims=True))
            alpha = jnp.exp(m_old - m_new)
            p = jnp.exp(s - m_new[:, 0:1])
            l_ref[mi] = alpha * l_ref[mi] + jnp.sum(p, axis=-1, keepdims=True)
            acc_ref[mi] = alpha * acc_ref[mi] + _dot(p.astype(BF16), vs)
            m_ref[mi] = m_new

    def full_body(j, carry):
        step(j, False)
        return carry

    lax.fori_loop(0, qi * (tq // tk), full_body, 0)
    step(qi, True)

    lp = lam_ref[...]
    lam = (jnp.exp(jnp.sum(lp[0:1] * lp[1:2], axis=-1, keepdims=True))
           - jnp.exp(jnp.sum(lp[2:3] * lp[3:4], axis=-1, keepdims=True)) + lam_init)
    o = acc_ref[0] / l_ref[0] - lam * (acc_ref[1] / l_ref[1])
    o_ref[...] = (_rms_hat(o) * snw_ref[...] * (1.0 - lam_init)).astype(o_ref.dtype)


def _diff_attn(q, k, v, lam_params, sub_norm, batch, seq_len, lam_init):
    t, width = q.shape
    tq, tk = ATTN_TQ, ATTN_TK
    assert tq == tk, "the single masked diagonal step assumes square tiles"
    nq = seq_len // tq
    q_spec = pl.BlockSpec((tq, HEAD_DIM), lambda b, h, i: (b * nq + i, h))
    kv_spec = pl.BlockSpec((seq_len, HEAD_DIM), lambda b, h, i: (b, h))
    return pl.pallas_call(
        functools.partial(_diff_attn_kernel, tk=tk, lam_init=lam_init),
        grid=(batch, HEADS, nq),
        in_specs=[q_spec, kv_spec, kv_spec,
                  _const_spec(lam_params.shape), _const_spec((1, HEAD_DIM))],
        out_specs=q_spec,
        out_shape=jax.ShapeDtypeStruct((t, width), BF16),
        scratch_shapes=[pltpu.VMEM((2, tq, LANES), F32), pltpu.VMEM((2, tq, LANES), F32),
                        pltpu.VMEM((2, tq, HEAD_DIM), F32)],
        compiler_params=pltpu.CompilerParams(
            dimension_semantics=("parallel", "parallel", "arbitrary"),
            vmem_limit_bytes=VMEM_LIMIT),
        name="diff_attn",
    )(q, k, v, lam_params, sub_norm)


def kernel(x, positions, a_norm, a_w_in, a_conv_w, a_a_log, a_dt_bias, a_out_norm, a_w_out,
           kv_norm, w_kv, k_norm, b_norm, b_w_q, b_q_norm, b_lambda, b_sub_norm, b_w_out,
           mlp_norm, mlp_w1, mlp_w2):
    batch, seq_len, d = x.shape
    assert d == D_MODEL and a_norm.shape[0] == 1 and b_norm.shape[0] == 1
    assert seq_len % ROW_TILE == 0 and seq_len % ATTN_TQ == 0 and seq_len % DELTA_CHUNK == 0
    t = batch * seq_len
    width = HEADS * HEAD_DIM
    xf = x.reshape(t, d)

    w_in = a_w_in[0]
    w_main = w_in[:, :4 * width].astype(BF16)
    wg_t = w_in[:, 4 * width:].T.astype(BF16)
    pad = jnp.zeros((HEADS, 1), F32)
    alog16 = jnp.concatenate([pad, a_a_log[0].reshape(HEADS, 1)], axis=0)
    dtb16 = jnp.concatenate([pad, a_dt_bias[0].reshape(HEADS, 1)], axis=0)
    q, k, v, z, gates = _gdn_in(xf, a_norm[0].reshape(1, d), w_main, wg_t, a_conv_w[0],
                                alog16, dtb16, seq_len)
    o = _delta(q, k, v, z, gates, a_out_norm[0].reshape(1, HEAD_DIM), batch, seq_len)
    xf = _proj_mlp(o, xf, a_w_out[0].astype(BF16), mlp_norm[0].reshape(1, d),
                   mlp_w1[0].astype(BF16), mlp_w2[0].astype(BF16))

    half = ROPE_HALF
    freqs = ROPE_THETA ** (-jnp.arange(half, dtype=F32) / half)
    freq_row = jnp.tile(freqs, LANES // half).reshape(1, LANES)
    cos, s1, s2 = _rope_tab(positions.reshape(t, 1), freq_row)
    k_gain = jnp.tile(k_norm, width // MAP_DIM).reshape(1, width)
    q_gain = jnp.tile(b_q_norm[0], width // MAP_DIM).reshape(1, width)
    kr, vv, qr = _attn_in(xf, kv_norm.reshape(1, d), b_norm[0].reshape(1, d),
                          w_kv.astype(BF16), b_w_q[0].astype(BF16), k_gain, q_gain, cos, s1, s2)
    lam_init = 0.8 - 0.6 * math.exp(-0.3 * 1)
    oa = _diff_attn(qr, kr, vv, b_lambda[0], b_sub_norm[0].reshape(1, HEAD_DIM),
                    batch, seq_len, lam_init)
    xf = _proj_mlp(oa, xf, b_w_out[0].astype(BF16), mlp_norm[1].reshape(1, d),
                   mlp_w1[1].astype(BF16), mlp_w2[1].astype(BF16))
    return xf.reshape(batch, seq_len, d)
```

```python
import functools
import math

import jax
import jax.numpy as jnp
from jax import lax
from jax.experimental import pallas as pl
from jax.experimental.pallas import tpu as pltpu

F32 = jnp.float32
BF16 = jnp.bfloat16

D_MODEL = 1024
HEADS = 8
HEAD_DIM = 128
MAP_DIM = 64
ROPE_HALF = MAP_DIM // 2
CONV_WIDTH = 4
D_FF = 4 * D_MODEL
ROPE_THETA = 10000.0
EPS = 1e-6
NEG_INF = -1e30

LANES = 128
SUBLANES = 8
MXU_DIM = 256

ROW_TILE = 512
FF_CHUNK = 1024
DELTA_CHUNK = 64
DELTA_GROUP = 8
ATTN_TQ = 512
ATTN_TK = 512
VMEM_LIMIT = 56 * 1024 * 1024

NT_DIMS = (((1,), (1,)), ((), ()))
TN_DIMS = (((0,), (0,)), ((), ()))


def _rms_hat(x):
    return x * lax.rsqrt(jnp.mean(x * x, axis=-1, keepdims=True) + EPS)


def _sigmoid(x):
    return 1.0 / (1.0 + jnp.exp(-x))


def _softplus(x):
    return jnp.maximum(x, 0.0) + jnp.log(1.0 + jnp.exp(-jnp.abs(x)))


def _dot(a, b):
    return jnp.dot(a, b, preferred_element_type=F32)


def _const_spec(shape):
    zeros = (0,) * len(shape)
    return pl.BlockSpec(shape, lambda *_: zeros, pipeline_mode=pl.Buffered(1))


def _gdn_in_kernel(x_ref, xh_ref, nw_ref, w_ref, wgt_ref, cw_ref, alog_ref, dtb_ref,
                   q_ref, k_ref, v_ref, z_ref, gate_ref, *, tiles_per_seq):
    tm = x_ref.shape[0]
    width = HEADS * HEAD_DIM
    nw = nw_ref[...]
    xn = (_rms_hat(x_ref[...]) * nw).astype(BF16)
    seq_start = (pl.program_id(0) % tiles_per_seq) == 0
    xh = jnp.where(seq_start, 0.0, _rms_hat(xh_ref[...]) * nw).astype(BF16)

    for ci, o_ref in enumerate((q_ref, k_ref, v_ref)):
        w = w_ref[:, ci * width:(ci + 1) * width]
        xp = jnp.concatenate([_dot(xh, w), _dot(xn, w)], axis=0)
        cw = cw_ref[:, ci * width:(ci + 1) * width]
        base = SUBLANES - (CONV_WIDTH - 1)
        c = cw[0:1, :] * xp[base:base + tm, :]
        for j in range(1, CONV_WIDTH):
            c = c + cw[j:j + 1, :] * xp[base + j:base + j + tm, :]
        a = c * _sigmoid(c)
        if ci == 2:
            o_ref[...] = a
        else:
            scale = HEAD_DIM ** -0.5 if ci == 0 else 1.0
            for h in range(HEADS):
                ah = a[:, h * HEAD_DIM:(h + 1) * HEAD_DIM]
                inv = lax.rsqrt(jnp.sum(ah * ah, axis=-1, keepdims=True) + EPS)
                o_ref[:, h * HEAD_DIM:(h + 1) * HEAD_DIM] = ah * (inv * scale)

    z_ref[...] = _dot(xn, w_ref[:, 3 * width:4 * width])

    gt = lax.dot_general(wgt_ref[...], xn, NT_DIMS, preferred_element_type=F32)
    beta = _sigmoid(gt)
    decay = -jnp.exp(alog_ref[...]) * _softplus(gt + dtb_ref[...])
    row = lax.broadcasted_iota(jnp.int32, gt.shape, 0)
    gate_ref[...] = jnp.where(row < HEADS, beta, decay)


def _gdn_in(x, norm_w, w_main, wg_t, conv_w, alog16, dtb16, seq_len):
    t, d = x.shape
    tm = ROW_TILE
    width = HEADS * HEAD_DIM
    act = jax.ShapeDtypeStruct((t, width), F32)
    row_spec = pl.BlockSpec((tm, width), lambda i: (i, 0))
    return pl.pallas_call(
        functools.partial(_gdn_in_kernel, tiles_per_seq=seq_len // tm),
        grid=(t // tm,),
        in_specs=[
            pl.BlockSpec((tm, d), lambda i: (i, 0)),
            pl.BlockSpec((SUBLANES, d), lambda i: (jnp.maximum(i * (tm // SUBLANES) - 1, 0), 0)),
            _const_spec((1, d)),
            _const_spec(w_main.shape),
            _const_spec(wg_t.shape),
            _const_spec(conv_w.shape),
            _const_spec(alog16.shape),
            _const_spec(dtb16.shape),
        ],
        out_specs=[row_spec, row_spec, row_spec, row_spec,
                   pl.BlockSpec((2 * HEADS, tm), lambda i: (0, i))],
        out_shape=[act, act, act, act, jax.ShapeDtypeStruct((2 * HEADS, t), F32)],
        compiler_params=pltpu.CompilerParams(dimension_semantics=("parallel",),
                                             vmem_limit_bytes=VMEM_LIMIT),
        name="gdn_in",
    )(x, x, norm_w, w_main, wg_t, conv_w, alog16, dtb16)


def _delta_kernel(q_ref, k_ref, v_ref, z_ref, beta_ref, g_ref, onw_ref, o_ref, gc_ref,
                  m1_a, bt_a, qp_a, op_a, cd_a, m1_b, bt_b, qp_b, op_b, cd_b, *, chunk, group):
    c_len = chunk
    n_groups = q_ref.shape[0] // (c_len * group)
    bufs = ((m1_a, bt_a, qp_a, op_a, cd_a), (m1_b, bt_b, qp_b, op_b, cd_b))
    row = lax.broadcasted_iota(jnp.int32, (c_len, c_len), 0)
    col = lax.broadcasted_iota(jnp.int32, (c_len, c_len), 1)
    causal = row >= col
    strict = row > col
    eye = row == col
    gc_ref[...] = jnp.dot(g_ref[0], (row <= col).astype(F32), precision=lax.Precision.HIGHEST,
                          preferred_element_type=F32)
    onw = onw_ref[...]
    n_steps = int(math.log2(c_len))

    def to_col(r):
        return jnp.sum(jnp.where(eye, r, 0.0), axis=1, keepdims=True)

    def chunk_rows(c):
        return pl.ds(pl.multiple_of(c * c_len, c_len), c_len)

    def run(prep, adv, st):
        todo = list(range(group)) if adv is not None else []

        def advance_one(st):
            if not todo:
                return st
            g = todo.pop(0)
            grp, (m1_ref, bt_ref, qp_ref, op_ref, cd_ref) = adv
            rows = chunk_rows(grp * group + g)
            st16 = st.astype(BF16)
            out = lax.dot_general(qp_ref[g], st16, NT_DIMS, preferred_element_type=F32) + op_ref[g]
            st = cd_ref[g, 0:1, :] * st - _dot(st16, m1_ref[g]) + bt_ref[g]
            zc = z_ref[rows, :]
            o_ref[rows, :] = _rms_hat(out) * onw * (zc * _sigmoid(zc))
            return st

        if prep is not None:
            grp, (m1_ref, bt_ref, qp_ref, op_ref, cd_ref) = prep
            chunks = [grp * group + g for g in range(group)]
            q = [q_ref[chunk_rows(c), :] for c in chunks]
            k = [k_ref[chunk_rows(c), :] for c in chunks]
            gc_r = [gc_ref[pl.ds(c, 1), :] for c in chunks]
            beta_c = [to_col(beta_ref[0, pl.ds(c, 1), :]) for c in chunks]
            gc_c = [to_col(r) for r in gc_r]
            gc_last = [r[:, c_len - 1:c_len] for r in gc_r]
            e_c = [jnp.exp(x) for x in gc_c]
            kb = [k[g] * beta_c[g] for g in range(group)]
            s = [lax.dot_general(jnp.concatenate([kb[g], q[g]], axis=0).astype(BF16),
                                 k[g].astype(BF16), NT_DIMS, preferred_element_type=F32)
                 for g in range(group)]
            st = advance_one(st)
            decay = [jnp.where(causal, jnp.exp(jnp.where(causal, gc_c[g] - gc_r[g], 0.0)), 0.0)
                     for g in range(group)]
            p = [jnp.where(strict, -(s[g][:c_len] * decay[g]), 0.0) for g in range(group)]
            intra = [s[g][c_len:] * decay[g] for g in range(group)]
            y = [jnp.concatenate([v_ref[chunk_rows(chunks[g]), :] * beta_c[g], kb[g] * e_c[g]], axis=1)
                 for g in range(group)]
            for step in range(n_steps):
                p16 = [x.astype(BF16) for x in p]
                y = [y[g] + _dot(p16[g], y[g].astype(BF16)) for g in range(group)]
                if step + 1 < n_steps:
                    p = [_dot(x, x) for x in p16]
                st = advance_one(st)
            y16 = [x.astype(BF16) for x in y]
            kd16 = [(k[g] * jnp.exp(gc_last[g] - gc_c[g])).astype(BF16) for g in range(group)]
            mb = [lax.dot_general(y16[g], kd16[g], TN_DIMS, preferred_element_type=F32)
                  for g in range(group)]
            iu = [_dot(intra[g].astype(BF16), y16[g]) for g in range(group)]
            st = advance_one(st)
            for g in range(group):
                bt_ref[g] = mb[g][:HEAD_DIM]
                m1_ref[g] = mb[g][HEAD_DIM:].astype(BF16)
                qp_ref[g] = (q[g] * e_c[g] - iu[g][:, HEAD_DIM:]).astype(BF16)
                op_ref[g] = iu[g][:, :HEAD_DIM]
                cd_ref[g] = jnp.broadcast_to(jnp.exp(gc_last[g]), (SUBLANES, LANES))
        while todo:
            st = advance_one(st)
        return st

    st = run((0, bufs[0]), None, jnp.zeros((HEAD_DIM, HEAD_DIM), F32))

    def body(i, st):
        st = run((2 * i + 1, bufs[1]), (2 * i, bufs[0]), st)
        return run((2 * i + 2, bufs[0]), (2 * i + 1, bufs[1]), st)

    st = lax.fori_loop(0, n_groups // 2 - 1, body, st)
    st = run((n_groups - 1, bufs[1]), (n_groups - 2, bufs[0]), st)
    run(None, (n_groups - 1, bufs[1]), st)


def _delta(q, k, v, z, gates, out_norm, batch, seq_len):
    t, width = q.shape
    c_len = DELTA_CHUNK
    n_chunks = seq_len // c_len
    assert n_chunks % (2 * DELTA_GROUP) == 0
    gates3 = gates.reshape(2 * HEADS, batch * n_chunks, c_len)
    seq_spec = pl.BlockSpec((seq_len, HEAD_DIM), lambda b, h: (b, h))
    buf_set = [pltpu.VMEM((DELTA_GROUP, HEAD_DIM, HEAD_DIM), BF16),
               pltpu.VMEM((DELTA_GROUP, HEAD_DIM, HEAD_DIM), F32),
               pltpu.VMEM((DELTA_GROUP, c_len, HEAD_DIM), BF16),
               pltpu.VMEM((DELTA_GROUP, c_len, HEAD_DIM), F32),
               pltpu.VMEM((DELTA_GROUP, SUBLANES, LANES), F32)]
    return pl.pallas_call(
        functools.partial(_delta_kernel, chunk=c_len, group=DELTA_GROUP),
        grid=(batch, HEADS),
        in_specs=[seq_spec, seq_spec, seq_spec, seq_spec,
                  pl.BlockSpec((1, n_chunks, c_len), lambda b, h: (h, b, 0)),
                  pl.BlockSpec((1, n_chunks, c_len), lambda b, h: (HEADS + h, b, 0)),
                  pl.BlockSpec((1, HEAD_DIM), lambda b, h: (0, 0))],
        out_specs=seq_spec,
        out_shape=jax.ShapeDtypeStruct((t, width), F32),
        scratch_shapes=[pltpu.VMEM((n_chunks, c_len), F32)] + buf_set + buf_set,
        compiler_params=pltpu.CompilerParams(dimension_semantics=("parallel", "parallel"),
                                             vmem_limit_bytes=VMEM_LIMIT),
        name="delta",
    )(q, k, v, z, gates3, gates3, out_norm)


def _proj_mlp_kernel(o_ref, x_ref, wo_ref, nw_ref, w1_ref, w2_ref, out_ref):
    x1 = x_ref[...] + _dot(o_ref[...].astype(BF16), wo_ref[...])
    xn = (_rms_hat(x1) * nw_ref[...]).astype(BF16)
    acc = x1
    for j in range(D_FF // FF_CHUNK):
        h = jnp.maximum(_dot(xn, w1_ref[:, j * FF_CHUNK:(j + 1) * FF_CHUNK]), 0.0)
        acc = acc + _dot((h * h).astype(BF16), w2_ref[j * FF_CHUNK:(j + 1) * FF_CHUNK, :])
    out_ref[...] = acc


def _proj_mlp(o, x, w_out, norm_w, w1, w2):
    t, d = x.shape
    tm = ROW_TILE
    return pl.pallas_call(
        _proj_mlp_kernel,
        grid=(t // tm,),
        in_specs=[pl.BlockSpec((tm, o.shape[1]), lambda i: (i, 0)),
                  pl.BlockSpec((tm, d), lambda i: (i, 0)),
                  _const_spec(w_out.shape), _const_spec((1, d)),
                  _const_spec(w1.shape), _const_spec(w2.shape)],
        out_specs=pl.BlockSpec((tm, d), lambda i: (i, 0)),
        out_shape=jax.ShapeDtypeStruct((t, d), F32),
        compiler_params=pltpu.CompilerParams(dimension_semantics=("parallel",),
                                             vmem_limit_bytes=VMEM_LIMIT),
        name="proj_mlp",
    )(o, x, w_out, norm_w, w1, w2)


def _rope_tab_kernel(pos_ref, freq_ref, cos_ref, s1_ref, s2_ref):
    ang = pos_ref[...].astype(F32) * freq_ref[...]
    sin = jnp.sin(ang)
    lane = lax.broadcasted_iota(jnp.int32, ang.shape, 1)
    first_half = (lane % MAP_DIM) < ROPE_HALF
    cos_ref[...] = jnp.cos(ang)
    s1_ref[...] = jnp.where(first_half, -sin, 0.0)
    s2_ref[...] = jnp.where(first_half, 0.0, sin)


def _rope_tab(pos_col, freq_row):
    t = pos_col.shape[0]
    tm = 2048
    tab = jax.ShapeDtypeStruct((t, LANES), F32)
    spec = pl.BlockSpec((tm, LANES), lambda i: (i, 0))
    return pl.pallas_call(
        _rope_tab_kernel,
        grid=(t // tm,),
        in_specs=[pl.BlockSpec((tm, 1), lambda i: (i, 0)), _const_spec((1, LANES))],
        out_specs=[spec, spec, spec],
        out_shape=[tab, tab, tab],
        compiler_params=pltpu.CompilerParams(dimension_semantics=("parallel",)),
        name="rope_tab",
    )(pos_col, freq_row)


def _attn_in_kernel(x_ref, kvn_ref, qnw_ref, wkv_ref, wq_ref, kg_ref, qg_ref,
                    cos_ref, s1_ref, s2_ref, k_ref, v_ref, q_ref):
    width = HEADS * HEAD_DIM
    xhat = _rms_hat(x_ref[...])
    kvn = (xhat * kvn_ref[...]).astype(BF16)
    qn = (xhat * qnw_ref[...]).astype(BF16)
    cos = cos_ref[...]
    s1 = s1_ref[...]
    s2 = s2_ref[...]
    r = lax.broadcasted_iota(jnp.int32, (MXU_DIM, MXU_DIM), 0) // MAP_DIM
    c = lax.broadcasted_iota(jnp.int32, (MXU_DIM, MXU_DIM), 1) // MAP_DIM
    group_ones = (r == c).astype(BF16)

    def norm_rope(raw, gain, scale, o_ref):
        for s in range(width // MXU_DIM):
            blk = raw[:, s * MXU_DIM:(s + 1) * MXU_DIM]
            ss = _dot((blk * blk).astype(BF16), group_ones)
            nb = blk * lax.rsqrt(ss * (1.0 / MAP_DIM) + EPS) * gain[:, s * MXU_DIM:(s + 1) * MXU_DIM]
            for hh in range(MXU_DIM // LANES):
                xb = nb[:, hh * LANES:(hh + 1) * LANES]
                rot = xb * cos + pltpu.roll(xb, LANES - ROPE_HALF, 1) * s1 + pltpu.roll(xb, ROPE_HALF, 1) * s2
                lo = s * MXU_DIM + hh * LANES
                o_ref[:, lo:lo + LANES] = (rot * scale).astype(o_ref.dtype)

    norm_rope(_dot(kvn, wkv_ref[:, :width]), kg_ref[...], 1.0, k_ref)
    v_ref[...] = _dot(kvn, wkv_ref[:, width:]).astype(v_ref.dtype)
    norm_rope(_dot(qn, wq_ref[...]), qg_ref[...], MAP_DIM ** -0.5, q_ref)


def _attn_in(x, kv_norm, q_norm_w, w_kv, w_q, k_gain, q_gain, cos, s1, s2):
    t, d = x.shape
    tm = ROW_TILE
    width = HEADS * HEAD_DIM
    act = jax.ShapeDtypeStruct((t, width), BF16)
    row_spec = pl.BlockSpec((tm, width), lambda i: (i, 0))
    tab_spec = pl.BlockSpec((tm, LANES), lambda i: (i, 0))
    return pl.pallas_call(
        _attn_in_kernel,
        grid=(t // tm,),
        in_specs=[pl.BlockSpec((tm, d), lambda i: (i, 0)),
                  _const_spec((1, d)), _const_spec((1, d)),
                  _const_spec(w_kv.shape), _const_spec(w_q.shape),
                  _const_spec((1, width)), _const_spec((1, width)),
                  tab_spec, tab_spec, tab_spec],
        out_specs=[row_spec, row_spec, row_spec],
        out_shape=[act, act, act],
        compiler_params=pltpu.CompilerParams(dimension_semantics=("parallel",),
                                             vmem_limit_bytes=VMEM_LIMIT),
        name="attn_in",
    )(x, kv_norm, q_norm_w, w_kv, w_q, k_gain, q_gain, cos, s1, s2)


def _diff_attn_kernel(q_ref, k_ref, v_ref, lam_ref, snw_ref, o_ref, m_ref, l_ref, acc_ref,
                      *, tk, lam_init):
    tq = q_ref.shape[0]
    qi = pl.program_id(2)
    q = q_ref[...]
    lane = lax.broadcasted_iota(jnp.int32, q.shape, 1)
    zero = jnp.zeros_like(q)
    qm = (jnp.where(lane < MAP_DIM, q, zero), jnp.where(lane < MAP_DIM, zero, q))

    m_ref[...] = jnp.full(m_ref.shape, NEG_INF, F32)
    l_ref[...] = jnp.zeros(l_ref.shape, F32)
    acc_ref[...] = jnp.zeros(acc_ref.shape, F32)

    def step(j, masked):
        rows = pl.ds(pl.multiple_of(j * tk, tk), tk)
        ks = k_ref[rows, :]
        vs = v_ref[rows, :]
        for mi in range(2):
            s = lax.dot_general(qm[mi], ks, NT_DIMS, preferred_element_type=F32)
            if masked:
                r = lax.broadcasted_iota(jnp.int32, s.shape, 0)
                c = lax.broadcasted_iota(jnp.int32, s.shape, 1)
                s = jnp.where(r >= c, s, NEG_INF)
            m_old = m_ref[mi]
            m_new = jnp.maximum(m_old, jnp.max(s, axis=-1, keepdims=True))
            alpha = jnp.exp(m_old - m_new)
            p = jnp.exp(s - m_new[:, 0:1])
            l_ref[mi] = alpha * l_ref[mi] + jnp.sum(p, axis=-1, keepdims=True)
            acc_ref[mi] = alpha * acc_ref[mi] + _dot(p.astype(BF16), vs)
            m_ref[mi] = m_new

    def full_body(j, carry):
        step(j, False)
        return carry

    lax.fori_loop(0, qi * (tq // tk), full_body, 0)
    step(qi, True)

    lp = lam_ref[...]
    lam = (jnp.exp(jnp.sum(lp[0:1] * lp[1:2], axis=-1, keepdims=True))
           - jnp.exp(jnp.sum(lp[2:3] * lp[3:4], axis=-1, keepdims=True)) + lam_init)
    o = acc_ref[0] / l_ref[0] - lam * (acc_ref[1] / l_ref[1])
    o_ref[...] = (_rms_hat(o) * snw_ref[...] * (1.0 - lam_init)).astype(o_ref.dtype)


def _diff_attn(q, k, v, lam_params, sub_norm, batch, seq_len, lam_init):
    t, width = q.shape
    tq, tk = ATTN_TQ, ATTN_TK
    assert tq == tk, "the single masked diagonal step assumes square tiles"
    nq = seq_len // tq
    q_spec = pl.BlockSpec((tq, HEAD_DIM), lambda b, h, i: (b * nq + i, h))
    kv_spec = pl.BlockSpec((seq_len, HEAD_DIM), lambda b, h, i: (b, h))
    return pl.pallas_call(
        functools.partial(_diff_attn_kernel, tk=tk, lam_init=lam_init),
        grid=(batch, HEADS, nq),
        in_specs=[q_spec, kv_spec, kv_spec,
                  _const_spec(lam_params.shape), _const_spec((1, HEAD_DIM))],
        out_specs=q_spec,
        out_shape=jax.ShapeDtypeStruct((t, width), BF16),
        scratch_shapes=[pltpu.VMEM((2, tq, LANES), F32), pltpu.VMEM((2, tq, LANES), F32),
                        pltpu.VMEM((2, tq, HEAD_DIM), F32)],
        compiler_params=pltpu.CompilerParams(
            dimension_semantics=("parallel", "parallel", "arbitrary"),
            vmem_limit_bytes=VMEM_LIMIT),
        name="diff_attn",
    )(q, k, v, lam_params, sub_norm)


def kernel(x, positions, a_norm, a_w_in, a_conv_w, a_a_log, a_dt_bias, a_out_norm, a_w_out,
           kv_norm, w_kv, k_norm, b_norm, b_w_q, b_q_norm, b_lambda, b_sub_norm, b_w_out,
           mlp_norm, mlp_w1, mlp_w2):
    batch, seq_len, d = x.shape
    assert d == D_MODEL and a_norm.shape[0] == 1 and b_norm.shape[0] == 1
    assert seq_len % ROW_TILE == 0 and seq_len % ATTN_TQ == 0 and seq_len % DELTA_CHUNK == 0
    t = batch * seq_len
    width = HEADS * HEAD_DIM
    xf = x.reshape(t, d)

    w_in = a_w_in[0]
    w_main = w_in[:, :4 * width].astype(BF16)
    wg_t = w_in[:, 4 * width:].T.astype(BF16)
    pad = jnp.zeros((HEADS, 1), F32)
    alog16 = jnp.concatenate([pad, a_a_log[0].reshape(HEADS, 1)], axis=0)
    dtb16 = jnp.concatenate([pad, a_dt_bias[0].reshape(HEADS, 1)], axis=0)
    q, k, v, z, gates = _gdn_in(xf, a_norm[0].reshape(1, d), w_main, wg_t, a_conv_w[0],
                                alog16, dtb16, seq_len)
    o = _delta(q, k, v, z, gates, a_out_norm[0].reshape(1, HEAD_DIM), batch, seq_len)
    xf = _proj_mlp(o, xf, a_w_out[0].astype(BF16), mlp_norm[0].reshape(1, d),
                   mlp_w1[0].astype(BF16), mlp_w2[0].astype(BF16))

    half = ROPE_HALF
    freqs = ROPE_THETA ** (-jnp.arange(half, dtype=F32) / half)
    freq_row = jnp.tile(freqs, LANES // half).reshape(1, LANES)
    cos, s1, s2 = _rope_tab(positions.reshape(t, 1), freq_row)
    k_gain = jnp.tile(k_norm, width // MAP_DIM).reshape(1, width)
    q_gain = jnp.tile(b_q_norm[0], width // MAP_DIM).reshape(1, width)
    kr, vv, qr = _attn_in(xf, kv_norm.reshape(1, d), b_norm[0].reshape(1, d),
                          w_kv.astype(BF16), b_w_q[0].astype(BF16), k_gain, q_gain, cos, s1, s2)
    lam_init = 0.8 - 0.6 * math.exp(-0.3 * 1)
    oa = _diff_attn(qr, kr, vv, b_lambda[0], b_sub_norm[0].reshape(1, HEAD_DIM),
                    batch, seq_len, lam_init)
    xf = _proj_mlp(oa, xf, b_w_out[0].astype(BF16), mlp_norm[1].reshape(1, d),
                   mlp_w1[1].astype(BF16), mlp_w2[1].astype(BF16))
    return xf.reshape(batch, seq_len, d)
```

```python
import functools
import math

import jax
import jax.numpy as jnp
from jax import lax
from jax.experimental import pallas as pl
from jax.experimental.pallas import tpu as pltpu

F32 = jnp.float32
BF16 = jnp.bfloat16

D_MODEL = 1024
HEADS = 8
HEAD_DIM = 128
MAP_DIM = 64
ROPE_HALF = MAP_DIM // 2
CONV_WIDTH = 4
D_FF = 4 * D_MODEL
ROPE_THETA = 10000.0
EPS = 1e-6
NEG_INF = -1e30
LOG2E = math.log2(math.e)
SCORE_BOUND_COEF = MAP_DIM * MAP_DIM ** -0.5 * LOG2E * 1.02
SCORE_BOUND_LIMIT = 100.0

LANES = 128
SUBLANES = 8
MXU_DIM = 256

ROW_TILE = 512
FF_CHUNK = 1024
DELTA_CHUNK = 64
DELTA_GROUP = 8
ATTN_TQ = 512
ATTN_TK = 512
VMEM_LIMIT = 56 * 1024 * 1024

NT_DIMS = (((1,), (1,)), ((), ()))
TN_DIMS = (((0,), (0,)), ((), ()))


def _rms_hat(x):
    return x * lax.rsqrt(jnp.mean(x * x, axis=-1, keepdims=True) + EPS)


def _sigmoid(x):
    return 1.0 / (1.0 + jnp.exp(-x))


def _softplus(x):
    return jnp.maximum(x, 0.0) + jnp.log(1.0 + jnp.exp(-jnp.abs(x)))


def _dot(a, b):
    return jnp.dot(a, b, preferred_element_type=F32)


def _const_spec(shape):
    zeros = (0,) * len(shape)
    return pl.BlockSpec(shape, lambda *_: zeros, pipeline_mode=pl.Buffered(1))


def _gdn_in_kernel(x_ref, xh_ref, nw_ref, w_ref, wgt_ref, cw_ref, alog_ref, dtb_ref,
                   q_ref, k_ref, v_ref, z_ref, gate_ref, *, tiles_per_seq):
    tm = x_ref.shape[0]
    width = HEADS * HEAD_DIM
    nw = nw_ref[...]
    xn = (_rms_hat(x_ref[...]) * nw).astype(BF16)
    seq_start = (pl.program_id(0) % tiles_per_seq) == 0
    xh = jnp.where(seq_start, 0.0, _rms_hat(xh_ref[...]) * nw).astype(BF16)

    for ci, o_ref in enumerate((q_ref, k_ref, v_ref)):
        w = w_ref[:, ci * width:(ci + 1) * width]
        xp = jnp.concatenate([_dot(xh, w), _dot(xn, w)], axis=0)
        cw = cw_ref[:, ci * width:(ci + 1) * width]
        base = SUBLANES - (CONV_WIDTH - 1)
        c = cw[0:1, :] * xp[base:base + tm, :]
        for j in range(1, CONV_WIDTH):
            c = c + cw[j:j + 1, :] * xp[base + j:base + j + tm, :]
        a = c * _sigmoid(c)
        if ci == 2:
            o_ref[...] = a
        else:
            scale = HEAD_DIM ** -0.5 if ci == 0 else 1.0
            for h in range(HEADS):
                ah = a[:, h * HEAD_DIM:(h + 1) * HEAD_DIM]
                inv = lax.rsqrt(jnp.sum(ah * ah, axis=-1, keepdims=True) + EPS)
                o_ref[:, h * HEAD_DIM:(h + 1) * HEAD_DIM] = ah * (inv * scale)

    z_ref[...] = _dot(xn, w_ref[:, 3 * width:4 * width])

    gt = lax.dot_general(wgt_ref[...], xn, NT_DIMS, preferred_element_type=F32)
    beta = _sigmoid(gt)
    decay = -jnp.exp(alog_ref[...]) * _softplus(gt + dtb_ref[...])
    row = lax.broadcasted_iota(jnp.int32, gt.shape, 0)
    gate_ref[...] = jnp.where(row < HEADS, beta, decay)


def _gdn_in(x, norm_w, w_main, wg_t, conv_w, alog16, dtb16, seq_len):
    t, d = x.shape
    tm = ROW_TILE
    width = HEADS * HEAD_DIM
    act = jax.ShapeDtypeStruct((t, width), F32)
    row_spec = pl.BlockSpec((tm, width), lambda i: (i, 0))
    return pl.pallas_call(
        functools.partial(_gdn_in_kernel, tiles_per_seq=seq_len // tm),
        grid=(t // tm,),
        in_specs=[
            pl.BlockSpec((tm, d), lambda i: (i, 0)),
            pl.BlockSpec((SUBLANES, d), lambda i: (jnp.maximum(i * (tm // SUBLANES) - 1, 0), 0)),
            _const_spec((1, d)),
            _const_spec(w_main.shape),
            _const_spec(wg_t.shape),
            _const_spec(conv_w.shape),
            _const_spec(alog16.shape),
            _const_spec(dtb16.shape),
        ],
        out_specs=[row_spec, row_spec, row_spec, row_spec,
                   pl.BlockSpec((2 * HEADS, tm), lambda i: (0, i))],
        out_shape=[act, act, act, act, jax.ShapeDtypeStruct((2 * HEADS, t), F32)],
        compiler_params=pltpu.CompilerParams(dimension_semantics=("parallel",),
                                             vmem_limit_bytes=VMEM_LIMIT),
        name="gdn_in",
    )(x, x, norm_w, w_main, wg_t, conv_w, alog16, dtb16)


def _delta_kernel(q_ref, k_ref, v_ref, z_ref, beta_ref, g_ref, onw_ref, o_ref, gc_ref,
                  m1_a, bt_a, qp_a, op_a, cd_a, m1_b, bt_b, qp_b, op_b, cd_b, *, chunk, group):
    c_len = chunk
    n_groups = q_ref.shape[0] // (c_len * group)
    bufs = ((m1_a, bt_a, qp_a, op_a, cd_a), (m1_b, bt_b, qp_b, op_b, cd_b))
    row = lax.broadcasted_iota(jnp.int32, (c_len, c_len), 0)
    col = lax.broadcasted_iota(jnp.int32, (c_len, c_len), 1)
    causal = row >= col
    strict = row > col
    eye = row == col
    gc_ref[...] = jnp.dot(g_ref[0], (row <= col).astype(F32), precision=lax.Precision.HIGHEST,
                          preferred_element_type=F32)
    onw = onw_ref[...]
    n_steps = int(math.log2(c_len))

    def to_col(r):
        return jnp.sum(jnp.where(eye, r, 0.0), axis=1, keepdims=True)

    def chunk_rows(c):
        return pl.ds(pl.multiple_of(c * c_len, c_len), c_len)

    def run(prep, adv, st):
        todo = list(range(group)) if adv is not None else []

        def advance_one(st):
            if not todo:
                return st
            g = todo.pop(0)
            grp, (m1_ref, bt_ref, qp_ref, op_ref, cd_ref) = adv
            rows = chunk_rows(grp * group + g)
            st16 = st.astype(BF16)
            out = lax.dot_general(qp_ref[g], st16, NT_DIMS, preferred_element_type=F32) + op_ref[g]
            st = cd_ref[g, 0:1, :] * st - _dot(st16, m1_ref[g]) + bt_ref[g]
            zc = z_ref[rows, :]
            o_ref[rows, :] = _rms_hat(out) * onw * (zc * _sigmoid(zc))
            return st

        if prep is not None:
            grp, (m1_ref, bt_ref, qp_ref, op_ref, cd_ref) = prep
            chunks = [grp * group + g for g in range(group)]
            q = [q_ref[chunk_rows(c), :] for c in chunks]
            k = [k_ref[chunk_rows(c), :] for c in chunks]
            gc_r = [gc_ref[pl.ds(c, 1), :] for c in chunks]
            beta_c = [to_col(beta_ref[0, pl.ds(c, 1), :]) for c in chunks]
            gc_c = [to_col(r) for r in gc_r]
            gc_last = [r[:, c_len - 1:c_len] for r in gc_r]
            e_c = [jnp.exp(x) for x in gc_c]
            kb = [k[g] * beta_c[g] for g in range(group)]
            s = [lax.dot_general(jnp.concatenate([kb[g], q[g]], axis=0).astype(BF16),
                                 k[g].astype(BF16), NT_DIMS, preferred_element_type=F32)
                 for g in range(group)]
            st = advance_one(st)
            decay = [jnp.where(causal, jnp.exp(jnp.where(causal, gc_c[g] - gc_r[g], 0.0)), 0.0)
                     for g in range(group)]
            p = [jnp.where(strict, -(s[g][:c_len] * decay[g]), 0.0) for g in range(group)]
            intra = [s[g][c_len:] * decay[g] for g in range(group)]
            y = [jnp.concatenate([v_ref[chunk_rows(chunks[g]), :] * beta_c[g], kb[g] * e_c[g]], axis=1)
                 for g in range(group)]
            for step in range(n_steps):
                p16 = [x.astype(BF16) for x in p]
                y = [y[g] + _dot(p16[g], y[g].astype(BF16)) for g in range(group)]
                if step + 1 < n_steps:
                    p = [_dot(x, x) for x in p16]
                st = advance_one(st)
            y16 = [x.astype(BF16) for x in y]
            kd16 = [(k[g] * jnp.exp(gc_last[g] - gc_c[g])).astype(BF16) for g in range(group)]
            mb = [lax.dot_general(y16[g], kd16[g], TN_DIMS, preferred_element_type=F32)
                  for g in range(group)]
            iu = [_dot(intra[g].astype(BF16), y16[g]) for g in range(group)]
            st = advance_one(st)
            for g in range(group):
                bt_ref[g] = mb[g][:HEAD_DIM]
                m1_ref[g] = mb[g][HEAD_DIM:].astype(BF16)
                qp_ref[g] = (q[g] * e_c[g] - iu[g][:, HEAD_DIM:]).astype(BF16)
                op_ref[g] = iu[g][:, :HEAD_DIM]
                cd_ref[g] = jnp.broadcast_to(jnp.exp(gc_last[g]), (SUBLANES, LANES))
        while todo:
            st = advance_one(st)
        return st

    st = run((0, bufs[0]), None, jnp.zeros((HEAD_DIM, HEAD_DIM), F32))

    def body(i, st):
        st = run((2 * i + 1, bufs[1]), (2 * i, bufs[0]), st)
        return run((2 * i + 2, bufs[0]), (2 * i + 1, bufs[1]), st)

    st = lax.fori_loop(0, n_groups // 2 - 1, body, st)
    st = run((n_groups - 1, bufs[1]), (n_groups - 2, bufs[0]), st)
    run(None, (n_groups - 1, bufs[1]), st)


def _delta(q, k, v, z, gates, out_norm, batch, seq_len):
    t, width = q.shape
    c_len = DELTA_CHUNK
    n_chunks = seq_len // c_len
    assert n_chunks % (2 * DELTA_GROUP) == 0
    gates3 = gates.reshape(2 * HEADS, batch * n_chunks, c_len)
    seq_spec = pl.BlockSpec((seq_len, HEAD_DIM), lambda b, h: (b, h))
    buf_set = [pltpu.VMEM((DELTA_GROUP, HEAD_DIM, HEAD_DIM), BF16),
               pltpu.VMEM((DELTA_GROUP, HEAD_DIM, HEAD_DIM), F32),
               pltpu.VMEM((DELTA_GROUP, c_len, HEAD_DIM), BF16),
               pltpu.VMEM((DELTA_GROUP, c_len, HEAD_DIM), F32),
               pltpu.VMEM((DELTA_GROUP, SUBLANES, LANES), F32)]
    return pl.pallas_call(
        functools.partial(_delta_kernel, chunk=c_len, group=DELTA_GROUP),
        grid=(batch, HEADS),
        in_specs=[seq_spec, seq_spec, seq_spec, seq_spec,
                  pl.BlockSpec((1, n_chunks, c_len), lambda b, h: (h, b, 0)),
                  pl.BlockSpec((1, n_chunks, c_len), lambda b, h: (HEADS + h, b, 0)),
                  pl.BlockSpec((1, HEAD_DIM), lambda b, h: (0, 0))],
        out_specs=seq_spec,
        out_shape=jax.ShapeDtypeStruct((t, width), F32),
        scratch_shapes=[pltpu.VMEM((n_chunks, c_len), F32)] + buf_set + buf_set,
        compiler_params=pltpu.CompilerParams(dimension_semantics=("parallel", "parallel"),
                                             vmem_limit_bytes=VMEM_LIMIT),
        name="delta",
    )(q, k, v, z, gates3, gates3, out_norm)


def _proj_mlp_kernel(o_ref, x_ref, wo_ref, nw_ref, w1_ref, w2_ref, out_ref):
    x1 = x_ref[...] + _dot(o_ref[...].astype(BF16), wo_ref[...])
    xn = (_rms_hat(x1) * nw_ref[...]).astype(BF16)
    acc = x1
    for j in range(D_FF // FF_CHUNK):
        h = jnp.maximum(_dot(xn, w1_ref[:, j * FF_CHUNK:(j + 1) * FF_CHUNK]), 0.0)
        acc = acc + _dot((h * h).astype(BF16), w2_ref[j * FF_CHUNK:(j + 1) * FF_CHUNK, :])
    out_ref[...] = acc


def _proj_mlp(o, x, w_out, norm_w, w1, w2):
    t, d = x.shape
    tm = ROW_TILE
    return pl.pallas_call(
        _proj_mlp_kernel,
        grid=(t // tm,),
        in_specs=[pl.BlockSpec((tm, o.shape[1]), lambda i: (i, 0)),
                  pl.BlockSpec((tm, d), lambda i: (i, 0)),
                  _const_spec(w_out.shape), _const_spec((1, d)),
                  _const_spec(w1.shape), _const_spec(w2.shape)],
        out_specs=pl.BlockSpec((tm, d), lambda i: (i, 0)),
        out_shape=jax.ShapeDtypeStruct((t, d), F32),
        compiler_params=pltpu.CompilerParams(dimension_semantics=("parallel",),
                                             vmem_limit_bytes=VMEM_LIMIT),
        name="proj_mlp",
    )(o, x, w_out, norm_w, w1, w2)


def _rope_tab_kernel(pos_ref, freq_ref, cos_ref, s1_ref, s2_ref):
    ang = pos_ref[...].astype(F32) * freq_ref[...]
    sin = jnp.sin(ang)
    lane = lax.broadcasted_iota(jnp.int32, ang.shape, 1)
    first_half = (lane % MAP_DIM) < ROPE_HALF
    cos_ref[...] = jnp.cos(ang)
    s1_ref[...] = jnp.where(first_half, -sin, 0.0)
    s2_ref[...] = jnp.where(first_half, 0.0, sin)


def _rope_tab(pos_col, freq_row):
    t = pos_col.shape[0]
    tm = 2048
    tab = jax.ShapeDtypeStruct((t, LANES), F32)
    spec = pl.BlockSpec((tm, LANES), lambda i: (i, 0))
    return pl.pallas_call(
        _rope_tab_kernel,
        grid=(t // tm,),
        in_specs=[pl.BlockSpec((tm, 1), lambda i: (i, 0)), _const_spec((1, LANES))],
        out_specs=[spec, spec, spec],
        out_shape=[tab, tab, tab],
        compiler_params=pltpu.CompilerParams(dimension_semantics=("parallel",)),
        name="rope_tab",
    )(pos_col, freq_row)


def _attn_in_kernel(x_ref, kvn_ref, qnw_ref, wkv_ref, wq_ref, kg_ref, qg_ref,
                    cos_ref, s1_ref, s2_ref, k_ref, v_ref, q_ref):
    width = HEADS * HEAD_DIM
    xhat = _rms_hat(x_ref[...])
    kvn = (xhat * kvn_ref[...]).astype(BF16)
    qn = (xhat * qnw_ref[...]).astype(BF16)
    cos = cos_ref[...]
    s1 = s1_ref[...]
    s2 = s2_ref[...]
    r = lax.broadcasted_iota(jnp.int32, (MXU_DIM, MXU_DIM), 0) // MAP_DIM
    c = lax.broadcasted_iota(jnp.int32, (MXU_DIM, MXU_DIM), 1) // MAP_DIM
    group_ones = (r == c).astype(BF16)

    def norm_rope(raw, gain, scale, o_ref):
        for s in range(width // MXU_DIM):
            blk = raw[:, s * MXU_DIM:(s + 1) * MXU_DIM]
            ss = _dot((blk * blk).astype(BF16), group_ones)
            nb = blk * lax.rsqrt(ss * (1.0 / MAP_DIM) + EPS) * gain[:, s * MXU_DIM:(s + 1) * MXU_DIM]
            for hh in range(MXU_DIM // LANES):
                xb = nb[:, hh * LANES:(hh + 1) * LANES]
                rot = xb * cos + pltpu.roll(xb, LANES - ROPE_HALF, 1) * s1 + pltpu.roll(xb, ROPE_HALF, 1) * s2
                lo = s * MXU_DIM + hh * LANES
                o_ref[:, lo:lo + LANES] = (rot * scale).astype(o_ref.dtype)

    norm_rope(_dot(kvn, wkv_ref[:, :width]), kg_ref[...], 1.0, k_ref)
    v_ref[...] = _dot(kvn, wkv_ref[:, width:]).astype(v_ref.dtype)
    norm_rope(_dot(qn, wq_ref[...]), qg_ref[...], MAP_DIM ** -0.5 * LOG2E, q_ref)


def _attn_in(x, kv_norm, q_norm_w, w_kv, w_q, k_gain, q_gain, cos, s1, s2):
    t, d = x.shape
    tm = ROW_TILE
    width = HEADS * HEAD_DIM
    act = jax.ShapeDtypeStruct((t, width), BF16)
    row_spec = pl.BlockSpec((tm, width), lambda i: (i, 0))
    tab_spec = pl.BlockSpec((tm, LANES), lambda i: (i, 0))
    return pl.pallas_call(
        _attn_in_kernel,
        grid=(t // tm,),
        in_specs=[pl.BlockSpec((tm, d), lambda i: (i, 0)),
                  _const_spec((1, d)), _const_spec((1, d)),
                  _const_spec(w_kv.shape), _const_spec(w_q.shape),
                  _const_spec((1, width)), _const_spec((1, width)),
                  tab_spec, tab_spec, tab_spec],
        out_specs=[row_spec, row_spec, row_spec],
        out_shape=[act, act, act],
        compiler_params=pltpu.CompilerParams(dimension_semantics=("parallel",),
                                             vmem_limit_bytes=VMEM_LIMIT),
        name="attn_in",
    )(x, kv_norm, q_norm_w, w_kv, w_q, k_gain, q_gain, cos, s1, s2)


def _diff_attn_kernel(q_ref, k_ref, v_ref, qg_ref, kg_ref, lam_ref, snw_ref, o_ref,
                      m_ref, l_ref, acc_ref, *, tk, lam_init):
    tq = q_ref.shape[0]
    qi = pl.program_id(2)
    q = q_ref[...]
    lane = lax.broadcasted_iota(jnp.int32, q.shape, 1)
    zero = jnp.zeros_like(q)
    qm = (jnp.where(lane < MAP_DIM, q, zero), jnp.where(lane < MAP_DIM, zero, q))
    n_full = qi * (tq // tk)

    def kv_rows(j):
        return pl.ds(pl.multiple_of(j * tk, tk), tk)

    def scores(j):
        ks = k_ref[kv_rows(j), :]
        return [lax.dot_general(qm[mi], ks, NT_DIMS, preferred_element_type=F32) for mi in range(2)]

    def causal_mask():
        r = lax.broadcasted_iota(jnp.int32, (tq, tk), 0)
        c = lax.broadcasted_iota(jnp.int32, (tq, tk), 1)
        return r >= c

    def finish(l0, l1):
        lp = lam_ref[...]
        lam = (jnp.exp(jnp.sum(lp[0:1] * lp[1:2], axis=-1, keepdims=True))
               - jnp.exp(jnp.sum(lp[2:3] * lp[3:4], axis=-1, keepdims=True)) + lam_init)
        o = acc_ref[0] / l0 - lam * (acc_ref[1] / l1)
        o_ref[...] = (_rms_hat(o) * snw_ref[...] * (1.0 - lam_init)).astype(o_ref.dtype)

    bound = SCORE_BOUND_COEF * jnp.max(jnp.abs(qg_ref[...])) * jnp.max(jnp.abs(kg_ref[...]))
    bounded = bound <= SCORE_BOUND_LIMIT

    @pl.when(bounded)
    def _():
        l_ref[...] = jnp.zeros(l_ref.shape, F32)
        acc_ref[...] = jnp.zeros(acc_ref.shape, F32)

        def step(j, masked):
            s = scores(j)
            vs = v_ref[kv_rows(j), :]
            for mi in range(2):
                p = jnp.exp2(s[mi])
                if masked:
                    p = jnp.where(causal_mask(), p, 0.0)
                lanes_sum = p[:, 0:LANES]
                for b in range(1, tk // LANES):
                    lanes_sum = lanes_sum + p[:, b * LANES:(b + 1) * LANES]
                l_ref[mi] += lanes_sum
                acc_ref[mi] += _dot(p.astype(BF16), vs)

        def full_body(j, carry):
            step(j, False)
            return carry

        lax.fori_loop(0, n_full, full_body, 0)
        step(qi, True)
        finish(jnp.sum(l_ref[0], axis=-1, keepdims=True), jnp.sum(l_ref[1], axis=-1, keepdims=True))

    @pl.when(jnp.logical_not(bounded))
    def _():
        m_ref[...] = jnp.full(m_ref.shape, NEG_INF, F32)
        l_ref[...] = jnp.zeros(l_ref.shape, F32)
        acc_ref[...] = jnp.zeros(acc_ref.shape, F32)

        def step(j, masked):
            s = scores(j)
            vs = v_ref[kv_rows(j), :]
            for mi in range(2):
                sm = jnp.where(causal_mask(), s[mi], NEG_INF) if masked else s[mi]
                m_old = m_ref[mi]
                m_new = jnp.maximum(m_old, jnp.max(sm, axis=-1, keepdims=True))
                alpha = jnp.exp2(m_old - m_new)
                p = jnp.exp2(sm - m_new[:, 0:1])
                l_ref[mi] = alpha * l_ref[mi] + jnp.sum(p, axis=-1, keepdims=True)
                acc_ref[mi] = alpha * acc_ref[mi] + _dot(p.astype(BF16), vs)
                m_ref[mi] = m_new

        def full_body(j, carry):
            step(j, False)
            return carry

        lax.fori_loop(0, n_full, full_body, 0)
        step(qi, True)
        finish(l_ref[0], l_ref[1])


def _diff_attn(q, k, v, q_gain, k_gain, lam_params, sub_norm, batch, seq_len, lam_init):
    t, width = q.shape
    tq, tk = ATTN_TQ, ATTN_TK
    assert tq == tk, "the single masked diagonal step assumes square tiles"
    nq = seq_len // tq
    q_spec = pl.BlockSpec((tq, HEAD_DIM), lambda b, h, i: (b * nq + i, h))
    kv_spec = pl.BlockSpec((seq_len, HEAD_DIM), lambda b, h, i: (b, h))
    return pl.pallas_call(
        functools.partial(_diff_attn_kernel, tk=tk, lam_init=lam_init),
        grid=(batch, HEADS, nq),
        in_specs=[q_spec, kv_spec, kv_spec,
                  _const_spec(q_gain.shape), _const_spec(k_gain.shape),
                  _const_spec(lam_params.shape), _const_spec((1, HEAD_DIM))],
        out_specs=q_spec,
        out_shape=jax.ShapeDtypeStruct((t, width), BF16),
        scratch_shapes=[pltpu.VMEM((2, tq, LANES), F32), pltpu.VMEM((2, tq, LANES), F32),
                        pltpu.VMEM((2, tq, HEAD_DIM), F32)],
        compiler_params=pltpu.CompilerParams(
            dimension_semantics=("parallel", "parallel", "arbitrary"),
            vmem_limit_bytes=VMEM_LIMIT),
        name="diff_attn",
    )(q, k, v, q_gain, k_gain, lam_params, sub_norm)


def kernel(x, positions, a_norm, a_w_in, a_conv_w, a_a_log, a_dt_bias, a_out_norm, a_w_out,
           kv_norm, w_kv, k_norm, b_norm, b_w_q, b_q_norm, b_lambda, b_sub_norm, b_w_out,
           mlp_norm, mlp_w1, mlp_w2):
    batch, seq_len, d = x.shape
    assert d == D_MODEL and a_norm.shape[0] == 1 and b_norm.shape[0] == 1
    assert seq_len % ROW_TILE == 0 and seq_len % ATTN_TQ == 0 and seq_len % DELTA_CHUNK == 0
    t = batch * seq_len
    width = HEADS * HEAD_DIM
    xf = x.reshape(t, d)

    w_in = a_w_in[0]
    w_main = w_in[:, :4 * width].astype(BF16)
    wg_t = w_in[:, 4 * width:].T.astype(BF16)
    pad = jnp.zeros((HEADS, 1), F32)
    alog16 = jnp.concatenate([pad, a_a_log[0].reshape(HEADS, 1)], axis=0)
    dtb16 = jnp.concatenate([pad, a_dt_bias[0].reshape(HEADS, 1)], axis=0)
    q, k, v, z, gates = _gdn_in(xf, a_norm[0].reshape(1, d), w_main, wg_t, a_conv_w[0],
                                alog16, dtb16, seq_len)
    o = _delta(q, k, v, z, gates, a_out_norm[0].reshape(1, HEAD_DIM), batch, seq_len)
    xf = _proj_mlp(o, xf, a_w_out[0].astype(BF16), mlp_norm[0].reshape(1, d),
                   mlp_w1[0].astype(BF16), mlp_w2[0].astype(BF16))

    half = ROPE_HALF
    freqs = ROPE_THETA ** (-jnp.arange(half, dtype=F32) / half)
    freq_row = jnp.tile(freqs, LANES // half).reshape(1, LANES)
    cos, s1, s2 = _rope_tab(positions.reshape(t, 1), freq_row)
    k_gain = jnp.tile(k_norm, width // MAP_DIM).reshape(1, width)
    q_gain = jnp.tile(b_q_norm[0], width // MAP_DIM).reshape(1, width)
    kr, vv, qr = _attn_in(xf, kv_norm.reshape(1, d), b_norm[0].reshape(1, d),
                          w_kv.astype(BF16), b_w_q[0].astype(BF16), k_gain, q_gain, cos, s1, s2)
    lam_init = 0.8 - 0.6 * math.exp(-0.3 * 1)
    oa = _diff_attn(qr, kr, vv, b_q_norm[0].reshape(1, MAP_DIM), k_norm.reshape(1, MAP_DIM),
                    b_lambda[0], b_sub_norm[0].reshape(1, HEAD_DIM), batch, seq_len, lam_init)
    xf = _proj_mlp(oa, xf, b_w_out[0].astype(BF16), mlp_norm[1].reshape(1, d),
                   mlp_w1[1].astype(BF16), mlp_w2[1].astype(BF16))
    return xf.reshape(batch, seq_len, d)
```

```python
import functools
import math

import jax
import jax.numpy as jnp
from jax import lax
from jax.experimental import pallas as pl
from jax.experimental.pallas import tpu as pltpu

F32 = jnp.float32
BF16 = jnp.bfloat16

D_MODEL = 1024
HEADS = 8
HEAD_DIM = 128
MAP_DIM = 64
ROPE_HALF = MAP_DIM // 2
CONV_WIDTH = 4
D_FF = 4 * D_MODEL
ROPE_THETA = 10000.0
EPS = 1e-6
NEG_INF = -1e30
LOG2E = math.log2(math.e)
SCORE_BOUND_COEF = MAP_DIM * MAP_DIM ** -0.5 * LOG2E * 1.02
SCORE_BOUND_LIMIT = 100.0

LANES = 128
SUBLANES = 8
MXU_DIM = 256

ROW_TILE = 512
GDN_ROW_TILE = 512
FF_CHUNK = 1024
DELTA_CHUNK = 64
DELTA_GROUP = 8
DELTA_HEADS = 2
ATTN_TQ = 512
ATTN_TK = 512
VMEM_LIMIT = 56 * 1024 * 1024

NT_DIMS = (((1,), (1,)), ((), ()))
TN_DIMS = (((0,), (0,)), ((), ()))


def _rms_hat(x):
    return x * lax.rsqrt(jnp.mean(x * x, axis=-1, keepdims=True) + EPS)


def _sigmoid(x):
    return 1.0 / (1.0 + jnp.exp(-x))


def _silu(x):
    h = 0.5 * x
    return h * jnp.tanh(h) + h


def _softplus(x):
    return jnp.maximum(x, 0.0) + jnp.log(1.0 + jnp.exp(-jnp.abs(x)))


def _dot(a, b):
    return jnp.dot(a, b, preferred_element_type=F32)


def _const_spec(shape):
    zeros = (0,) * len(shape)
    return pl.BlockSpec(shape, lambda *_: zeros, pipeline_mode=pl.Buffered(1))


def _gdn_in_kernel(x_ref, xh_ref, nw_ref, w_ref, wgt_ref, cw_ref, alog_ref, dtb_ref,
                   q_ref, k_ref, v_ref, z_ref, gate_ref, *, tiles_per_seq):
    tm = x_ref.shape[0]
    width = HEADS * HEAD_DIM
    nw = nw_ref[...]
    xn = (_rms_hat(x_ref[...]) * nw).astype(BF16)
    seq_start = (pl.program_id(0) % tiles_per_seq) == 0
    xh = jnp.where(seq_start, 0.0, _rms_hat(xh_ref[...]) * nw).astype(BF16)

    for ci, o_ref in enumerate((q_ref, k_ref, v_ref)):
        w = w_ref[:, ci * width:(ci + 1) * width]
        xp = jnp.concatenate([_dot(xh, w), _dot(xn, w)], axis=0)
        cw = cw_ref[:, ci * width:(ci + 1) * width]
        c = cw[CONV_WIDTH - 1:CONV_WIDTH, :] * xp[SUBLANES:, :]
        for j in range(CONV_WIDTH - 1):
            c = c + cw[j:j + 1, :] * pltpu.roll(xp, CONV_WIDTH - 1 - j, 0)[SUBLANES:, :]
        a = _silu(c)
        if ci == 2:
            o_ref[...] = a
        else:
            scale = HEAD_DIM ** -0.5 if ci == 0 else 1.0
            for h in range(HEADS):
                ah = a[:, h * HEAD_DIM:(h + 1) * HEAD_DIM]
                inv = lax.rsqrt(jnp.sum(ah * ah, axis=-1, keepdims=True) + EPS)
                o_ref[:, h * HEAD_DIM:(h + 1) * HEAD_DIM] = ah * (inv * scale)

    z_ref[...] = _dot(xn, w_ref[:, 3 * width:4 * width])

    gt = lax.dot_general(wgt_ref[...], xn, NT_DIMS, preferred_element_type=F32)
    beta = _sigmoid(gt)
    decay = -jnp.exp(alog_ref[...]) * _softplus(gt + dtb_ref[...])
    row = lax.broadcasted_iota(jnp.int32, gt.shape, 0)
    gate_ref[...] = jnp.where(row < HEADS, beta, decay)


def _gdn_in(x, norm_w, w_main, wg_t, conv_w, alog16, dtb16, seq_len):
    t, d = x.shape
    tm = GDN_ROW_TILE
    width = HEADS * HEAD_DIM
    act = jax.ShapeDtypeStruct((t, width), F32)
    row_spec = pl.BlockSpec((tm, width), lambda i: (i, 0))
    return pl.pallas_call(
        functools.partial(_gdn_in_kernel, tiles_per_seq=seq_len // tm),
        grid=(t // tm,),
        in_specs=[
            pl.BlockSpec((tm, d), lambda i: (i, 0)),
            pl.BlockSpec((SUBLANES, d), lambda i: (jnp.maximum(i * (tm // SUBLANES) - 1, 0), 0)),
            _const_spec((1, d)),
            _const_spec(w_main.shape),
            _const_spec(wg_t.shape),
            _const_spec(conv_w.shape),
            _const_spec(alog16.shape),
            _const_spec(dtb16.shape),
        ],
        out_specs=[row_spec, row_spec, row_spec, row_spec,
                   pl.BlockSpec((2 * HEADS, tm), lambda i: (0, i))],
        out_shape=[act, act, act, act, jax.ShapeDtypeStruct((2 * HEADS, t), F32)],
        compiler_params=pltpu.CompilerParams(dimension_semantics=("parallel",),
                                             vmem_limit_bytes=VMEM_LIMIT),
        name="gdn_in",
    )(x, x, norm_w, w_main, wg_t, conv_w, alog16, dtb16)


def _delta_kernel(q_ref, k_ref, v_ref, z_ref, beta_ref, g_ref, onw_ref, o_ref, gc_ref,
                  m1_a, bt_a, qp_a, op_a, cd_a, m1_b, bt_b, qp_b, op_b, cd_b,
                  *, chunk, group, heads):
    c_len = chunk
    n_groups = q_ref.shape[0] // (c_len * group)
    bufs = ((m1_a, bt_a, qp_a, op_a, cd_a), (m1_b, bt_b, qp_b, op_b, cd_b))
    row = lax.broadcasted_iota(jnp.int32, (c_len, c_len), 0)
    col = lax.broadcasted_iota(jnp.int32, (c_len, c_len), 1)
    causal = row >= col
    strict = row > col
    eye = row == col
    upper = (row <= col).astype(F32)
    for hh in range(heads):
        gc_ref[hh] = jnp.dot(g_ref[hh], upper, precision=lax.Precision.HIGHEST,
                             preferred_element_type=F32)
    onw = onw_ref[...]
    n_steps = int(math.log2(c_len))
    chains = [(g, hh) for g in range(group) for hh in range(heads)]

    def to_col(r):
        return jnp.sum(jnp.where(eye, r, 0.0), axis=1, keepdims=True)

    def chunk_rows(c):
        return pl.ds(pl.multiple_of(c * c_len, c_len), c_len)

    def head_cols(hh):
        return slice(hh * HEAD_DIM, (hh + 1) * HEAD_DIM)

    def run(prep, adv, st):
        todo = list(range(group)) if adv is not None else []

        def advance_one(st):
            if not todo:
                return st
            g = todo.pop(0)
            grp, (m1_ref, bt_ref, qp_ref, op_ref, cd_ref) = adv
            rows = chunk_rows(grp * group + g)
            st16 = [x.astype(BF16) for x in st]
            slots = [g * heads + hh for hh in range(heads)]
            out = [lax.dot_general(qp_ref[slots[hh]], st16[hh], NT_DIMS, preferred_element_type=F32)
                   + op_ref[slots[hh]] for hh in range(heads)]
            st = [cd_ref[slots[hh], 0:1, :] * st[hh] - _dot(st16[hh], m1_ref[slots[hh]])
                  + bt_ref[slots[hh]] for hh in range(heads)]
            for hh in range(heads):
                zc = z_ref[rows, head_cols(hh)]
                o_ref[rows, head_cols(hh)] = _rms_hat(out[hh]) * onw * _silu(zc)
            return st

        if prep is not None:
            grp, (m1_ref, bt_ref, qp_ref, op_ref, cd_ref) = prep
            n = len(chains)
            cidx = [grp * group + g for g, _ in chains]
            q = [q_ref[chunk_rows(cidx[i]), head_cols(chains[i][1])] for i in range(n)]
            k = [k_ref[chunk_rows(cidx[i]), head_cols(chains[i][1])] for i in range(n)]
            gc_r = [gc_ref[chains[i][1], pl.ds(cidx[i], 1), :] for i in range(n)]
            beta_c = [to_col(beta_ref[chains[i][1], pl.ds(cidx[i], 1), :]) for i in range(n)]
            gc_c = [to_col(r) for r in gc_r]
            gc_last = [r[:, c_len - 1:c_len] for r in gc_r]
            e_c = [jnp.exp(x) for x in gc_c]
            kb = [k[i] * beta_c[i] for i in range(n)]
            s = [lax.dot_general(jnp.concatenate([kb[i], q[i]], axis=0).astype(BF16),
                                 k[i].astype(BF16), NT_DIMS, preferred_element_type=F32)
                 for i in range(n)]
            st = advance_one(st)
            decay = [jnp.where(causal, jnp.exp(jnp.where(causal, gc_c[i] - gc_r[i], 0.0)), 0.0)
                     for i in range(n)]
            p = [jnp.where(strict, -(s[i][:c_len] * decay[i]), 0.0) for i in range(n)]
            intra = [s[i][c_len:] * decay[i] for i in range(n)]
            y = [jnp.concatenate([v_ref[chunk_rows(cidx[i]), head_cols(chains[i][1])] * beta_c[i],
                                  kb[i] * e_c[i]], axis=1) for i in range(n)]
            for step in range(n_steps):
                p16 = [x.astype(BF16) for x in p]
                y = [y[i] + _dot(p16[i], y[i].astype(BF16)) for i in range(n)]
                if step + 1 < n_steps:
                    p = [_dot(x, x) for x in p16]
                st = advance_one(st)
            y16 = [x.astype(BF16) for x in y]
            kd16 = [(k[i] * jnp.exp(gc_last[i] - gc_c[i])).astype(BF16) for i in range(n)]
            mb = [lax.dot_general(y16[i], kd16[i], TN_DIMS, preferred_element_type=F32)
                  for i in range(n)]
            iu = [_dot(intra[i].astype(BF16), y16[i]) for i in range(n)]
            st = advance_one(st)
            for i in range(n):
                bt_ref[i] = mb[i][:HEAD_DIM]
                m1_ref[i] = mb[i][HEAD_DIM:].astype(BF16)
                qp_ref[i] = (q[i] * e_c[i] - iu[i][:, HEAD_DIM:]).astype(BF16)
                op_ref[i] = iu[i][:, :HEAD_DIM]
                cd_ref[i] = jnp.broadcast_to(jnp.exp(gc_last[i]), (SUBLANES, LANES))
        while todo:
            st = advance_one(st)
        return st

    st = run((0, bufs[0]), None, [jnp.zeros((HEAD_DIM, HEAD_DIM), F32) for _ in range(heads)])

    def body(i, st):
        st = run((2 * i + 1, bufs[1]), (2 * i, bufs[0]), st)
        return run((2 * i + 2, bufs[0]), (2 * i + 1, bufs[1]), st)

    st = lax.fori_loop(0, n_groups // 2 - 1, body, st)
    st = run((n_groups - 1, bufs[1]), (n_groups - 2, bufs[0]), st)
    run(None, (n_groups - 1, bufs[1]), st)


def _delta(q, k, v, z, gates, out_norm, batch, seq_len):
    t, width = q.shape
    c_len = DELTA_CHUNK
    nh = DELTA_HEADS
    n_chunks = seq_len // c_len
    assert n_chunks % (2 * DELTA_GROUP) == 0 and HEADS % nh == 0
    gates3 = gates.reshape(2 * HEADS, batch * n_chunks, c_len)
    seq_spec = pl.BlockSpec((seq_len, nh * HEAD_DIM), lambda b, h: (b, h))
    slots = DELTA_GROUP * nh
    buf_set = [pltpu.VMEM((slots, HEAD_DIM, HEAD_DIM), BF16),
               pltpu.VMEM((slots, HEAD_DIM, HEAD_DIM), F32),
               pltpu.VMEM((slots, c_len, HEAD_DIM), BF16),
               pltpu.VMEM((slots, c_len, HEAD_DIM), F32),
               pltpu.VMEM((slots, SUBLANES, LANES), F32)]
    return pl.pallas_call(
        functools.partial(_delta_kernel, chunk=c_len, group=DELTA_GROUP, heads=nh),
        grid=(batch, HEADS // nh),
        in_specs=[seq_spec, seq_spec, seq_spec, seq_spec,
                  pl.BlockSpec((nh, n_chunks, c_len), lambda b, h: (h, b, 0)),
                  pl.BlockSpec((nh, n_chunks, c_len), lambda b, h: (HEADS // nh + h, b, 0)),
                  pl.BlockSpec((1, HEAD_DIM), lambda b, h: (0, 0))],
        out_specs=seq_spec,
        out_shape=jax.ShapeDtypeStruct((t, width), F32),
        scratch_shapes=[pltpu.VMEM((nh, n_chunks, c_len), F32)] + buf_set + buf_set,
        compiler_params=pltpu.CompilerParams(dimension_semantics=("parallel", "parallel"),
                                             vmem_limit_bytes=VMEM_LIMIT),
        name="delta",
    )(q, k, v, z, gates3, gates3, out_norm)


def _proj_mlp_kernel(o_ref, x_ref, wo_ref, nw_ref, w1_ref, w2_ref, out_ref):
    x1 = x_ref[...] + _dot(o_ref[...].astype(BF16), wo_ref[...])
    xn = (_rms_hat(x1) * nw_ref[...]).astype(BF16)
    acc = x1
    for j in range(D_FF // FF_CHUNK):
        h = jnp.maximum(_dot(xn, w1_ref[:, j * FF_CHUNK:(j + 1) * FF_CHUNK]), 0.0)
        acc = acc + _dot((h * h).astype(BF16), w2_ref[j * FF_CHUNK:(j + 1) * FF_CHUNK, :])
    out_ref[...] = acc


def _proj_mlp(o, x, w_out, norm_w, w1, w2):
    t, d = x.shape
    tm = ROW_TILE
    return pl.pallas_call(
        _proj_mlp_kernel,
        grid=(t // tm,),
        in_specs=[pl.BlockSpec((tm, o.shape[1]), lambda i: (i, 0)),
                  pl.BlockSpec((tm, d), lambda i: (i, 0)),
                  _const_spec(w_out.shape), _const_spec((1, d)),
                  _const_spec(w1.shape), _const_spec(w2.shape)],
        out_specs=pl.BlockSpec((tm, d), lambda i: (i, 0)),
        out_shape=jax.ShapeDtypeStruct((t, d), F32),
        compiler_params=pltpu.CompilerParams(dimension_semantics=("parallel",),
                                             vmem_limit_bytes=VMEM_LIMIT),
        name="proj_mlp",
    )(o, x, w_out, norm_w, w1, w2)


def _rope_tab_kernel(pos_ref, freq_ref, cos_ref, s1_ref, s2_ref):
    ang = pos_ref[...].astype(F32) * freq_ref[...]
    sin = jnp.sin(ang)
    lane = lax.broadcasted_iota(jnp.int32, ang.shape, 1)
    first_half = (lane % MAP_DIM) < ROPE_HALF
    cos_ref[...] = jnp.cos(ang)
    s1_ref[...] = jnp.where(first_half, -sin, 0.0)
    s2_ref[...] = jnp.where(first_half, 0.0, sin)


def _rope_tab(pos_col, freq_row):
    t = pos_col.shape[0]
    tm = 2048
    tab = jax.ShapeDtypeStruct((t, LANES), F32)
    spec = pl.BlockSpec((tm, LANES), lambda i: (i, 0))
    return pl.pallas_call(
        _rope_tab_kernel,
        grid=(t // tm,),
        in_specs=[pl.BlockSpec((tm, 1), lambda i: (i, 0)), _const_spec((1, LANES))],
        out_specs=[spec, spec, spec],
        out_shape=[tab, tab, tab],
        compiler_params=pltpu.CompilerParams(dimension_semantics=("parallel",)),
        name="rope_tab",
    )(pos_col, freq_row)


def _attn_in_kernel(x_ref, kvn_ref, qnw_ref, wkv_ref, wq_ref, kg_ref, qg_ref,
                    cos_ref, s1_ref, s2_ref, k_ref, v_ref, q_ref):
    width = HEADS * HEAD_DIM
    xhat = _rms_hat(x_ref[...])
    kvn = (xhat * kvn_ref[...]).astype(BF16)
    qn = (xhat * qnw_ref[...]).astype(BF16)
    cos = cos_ref[...]
    s1 = s1_ref[...]
    s2 = s2_ref[...]
    r = lax.broadcasted_iota(jnp.int32, (MXU_DIM, MXU_DIM), 0) // MAP_DIM
    c = lax.broadcasted_iota(jnp.int32, (MXU_DIM, MXU_DIM), 1) // MAP_DIM
    group_ones = (r == c).astype(BF16)

    def norm_rope(raw, gain, scale, o_ref):
        for s in range(width // MXU_DIM):
            blk = raw[:, s * MXU_DIM:(s + 1) * MXU_DIM]
            ss = _dot((blk * blk).astype(BF16), group_ones)
            nb = blk * lax.rsqrt(ss * (1.0 / MAP_DIM) + EPS) * gain[:, s * MXU_DIM:(s + 1) * MXU_DIM]
            for hh in range(MXU_DIM // LANES):
                xb = nb[:, hh * LANES:(hh + 1) * LANES]
                rot = xb * cos + pltpu.roll(xb, LANES - ROPE_HALF, 1) * s1 + pltpu.roll(xb, ROPE_HALF, 1) * s2
                lo = s * MXU_DIM + hh * LANES
                o_ref[:, lo:lo + LANES] = (rot * scale).astype(o_ref.dtype)

    norm_rope(_dot(kvn, wkv_ref[:, :width]), kg_ref[...], 1.0, k_ref)
    v_ref[...] = _dot(kvn, wkv_ref[:, width:]).astype(v_ref.dtype)
    norm_rope(_dot(qn, wq_ref[...]), qg_ref[...], MAP_DIM ** -0.5 * LOG2E, q_ref)


def _attn_in(x, kv_norm, q_norm_w, w_kv, w_q, k_gain, q_gain, cos, s1, s2):
    t, d = x.shape
    tm = ROW_TILE
    width = HEADS * HEAD_DIM
    act = jax.ShapeDtypeStruct((t, width), BF16)
    row_spec = pl.BlockSpec((tm, width), lambda i: (i, 0))
    tab_spec = pl.BlockSpec((tm, LANES), lambda i: (i, 0))
    return pl.pallas_call(
        _attn_in_kernel,
        grid=(t // tm,),
        in_specs=[pl.BlockSpec((tm, d), lambda i: (i, 0)),
                  _const_spec((1, d)), _const_spec((1, d)),
                  _const_spec(w_kv.shape), _const_spec(w_q.shape),
                  _const_spec((1, width)), _const_spec((1, width)),
                  tab_spec, tab_spec, tab_spec],
        out_specs=[row_spec, row_spec, row_spec],
        out_shape=[act, act, act],
        compiler_params=pltpu.CompilerParams(dimension_semantics=("parallel",),
                                             vmem_limit_bytes=VMEM_LIMIT),
        name="attn_in",
    )(x, kv_norm, q_norm_w, w_kv, w_q, k_gain, q_gain, cos, s1, s2)


def _diff_attn_kernel(q_ref, k_ref, v_ref, qg_ref, kg_ref, lam_ref, snw_ref, o_ref,
                      m_ref, l_ref, acc_ref, *, tk, lam_init):
    tq = q_ref.shape[0]
    qi = pl.program_id(2)
    q = q_ref[...]
    lane = lax.broadcasted_iota(jnp.int32, q.shape, 1)
    zero = jnp.zeros_like(q)
    qm = (jnp.where(lane < MAP_DIM, q, zero), jnp.where(lane < MAP_DIM, zero, q))
    n_full = qi * (tq // tk)

    def kv_rows(j):
        return pl.ds(pl.multiple_of(j * tk, tk), tk)

    def scores(j):
        ks = k_ref[kv_rows(j), :]
        return [lax.dot_general(qm[mi], ks, NT_DIMS, preferred_element_type=F32) for mi in range(2)]

    def causal_mask():
        r = lax.broadcasted_iota(jnp.int32, (tq, tk), 0)
        c = lax.broadcasted_iota(jnp.int32, (tq, tk), 1)
        return r >= c

    def finish(l0, l1):
        lp = lam_ref[...]
        lam = (jnp.exp(jnp.sum(lp[0:1] * lp[1:2], axis=-1, keepdims=True))
               - jnp.exp(jnp.sum(lp[2:3] * lp[3:4], axis=-1, keepdims=True)) + lam_init)
        o = acc_ref[0] / l0 - lam * (acc_ref[1] / l1)
        o_ref[...] = (_rms_hat(o) * snw_ref[...] * (1.0 - lam_init)).astype(o_ref.dtype)

    bound = SCORE_BOUND_COEF * jnp.max(jnp.abs(qg_ref[...])) * jnp.max(jnp.abs(kg_ref[...]))
    bounded = bound <= SCORE_BOUND_LIMIT

    @pl.when(bounded)
    def _():
        l_ref[...] = jnp.zeros(l_ref.shape, F32)
        acc_ref[...] = jnp.zeros(acc_ref.shape, F32)

        def accumulate(mi, rows, p, vs):
            lanes_sum = p[:, 0:LANES]
            for b in range(1, p.shape[1] // LANES):
                lanes_sum = lanes_sum + p[:, b * LANES:(b + 1) * LANES]
            l_ref[mi, rows, :] += lanes_sum
            acc_ref[mi, rows, :] += _dot(p.astype(BF16), vs)

        def full_body(j, carry):
            s = scores(j)
            vs = v_ref[kv_rows(j), :]
            for mi in range(2):
                accumulate(mi, slice(None), jnp.exp2(s[mi]), vs)
            return carry

        lax.fori_loop(0, n_full, full_body, 0)

        half = tq // 2
        diag0 = pl.multiple_of(qi * tq, tq)
        blocks = ((slice(0, half), half), (slice(half, tq), tq))
        s = [[lax.dot_general(qm[mi][rows, :], k_ref[pl.ds(diag0, ncols), :], NT_DIMS,
                              preferred_element_type=F32) for mi in range(2)]
             for rows, ncols in blocks]
        for bi, (rows, ncols) in enumerate(blocks):
            r = lax.broadcasted_iota(jnp.int32, (half, ncols), 0) + bi * half
            c = lax.broadcasted_iota(jnp.int32, (half, ncols), 1)
            vs = v_ref[pl.ds(diag0, ncols), :]
            for mi in range(2):
                accumulate(mi, rows, jnp.where(r >= c, jnp.exp2(s[bi][mi]), 0.0), vs)
        finish(jnp.sum(l_ref[0], axis=-1, keepdims=True), jnp.sum(l_ref[1], axis=-1, keepdims=True))

    @pl.when(jnp.logical_not(bounded))
    def _():
        m_ref[...] = jnp.full(m_ref.shape, NEG_INF, F32)
        l_ref[...] = jnp.zeros(l_ref.shape, F32)
        acc_ref[...] = jnp.zeros(acc_ref.shape, F32)

        def step(j, masked):
            s = scores(j)
            vs = v_ref[kv_rows(j), :]
            for mi in range(2):
                sm = jnp.where(causal_mask(), s[mi], NEG_INF) if masked else s[mi]
                m_old = m_ref[mi]
                m_new = jnp.maximum(m_old, jnp.max(sm, axis=-1, keepdims=True))
                alpha = jnp.exp2(m_old - m_new)
                p = jnp.exp2(sm - m_new[:, 0:1])
                l_ref[mi] = alpha * l_ref[mi] + jnp.sum(p, axis=-1, keepdims=True)
                acc_ref[mi] = alpha * acc_ref[mi] + _dot(p.astype(BF16), vs)
                m_ref[mi] = m_new

        def full_body(j, carry):
            step(j, False)
            return carry

        lax.fori_loop(0, n_full, full_body, 0)
        step(qi, True)
        finish(l_ref[0], l_ref[1])


def _diff_attn(q, k, v, q_gain, k_gain, lam_params, sub_norm, batch, seq_len, lam_init):
    t, width = q.shape
    tq, tk = ATTN_TQ, ATTN_TK
    assert tq == tk, "the single masked diagonal step assumes square tiles"
    nq = seq_len // tq
    q_spec = pl.BlockSpec((tq, HEAD_DIM), lambda b, h, i: (b * nq + i, h))
    kv_spec = pl.BlockSpec((seq_len, HEAD_DIM), lambda b, h, i: (b, h))
    return pl.pallas_call(
        functools.partial(_diff_attn_kernel, tk=tk, lam_init=lam_init),
        grid=(batch, HEADS, nq),
        in_specs=[q_spec, kv_spec, kv_spec,
                  _const_spec(q_gain.shape), _const_spec(k_gain.shape),
                  _const_spec(lam_params.shape), _const_spec((1, HEAD_DIM))],
        out_specs=q_spec,
        out_shape=jax.ShapeDtypeStruct((t, width), BF16),
        scratch_shapes=[pltpu.VMEM((2, tq, LANES), F32), pltpu.VMEM((2, tq, LANES), F32),
                        pltpu.VMEM((2, tq, HEAD_DIM), F32)],
        compiler_params=pltpu.CompilerParams(
            dimension_semantics=("parallel", "parallel", "arbitrary"),
            vmem_limit_bytes=VMEM_LIMIT),
        name="diff_attn",
    )(q, k, v, q_gain, k_gain, lam_params, sub_norm)


def kernel(x, positions, a_norm, a_w_in, a_conv_w, a_a_log, a_dt_bias, a_out_norm, a_w_out,
           kv_norm, w_kv, k_norm, b_norm, b_w_q, b_q_norm, b_lambda, b_sub_norm, b_w_out,
           mlp_norm, mlp_w1, mlp_w2):
    batch, seq_len, d = x.shape
    assert d == D_MODEL and a_norm.shape[0] == 1 and b_norm.shape[0] == 1
    assert seq_len % ROW_TILE == 0 and seq_len % ATTN_TQ == 0 and seq_len % DELTA_CHUNK == 0
    t = batch * seq_len
    width = HEADS * HEAD_DIM
    xf = x.reshape(t, d)

    w_in = a_w_in[0]
    w_main = w_in[:, :4 * width].astype(BF16)
    wg_t = w_in[:, 4 * width:].T.astype(BF16)
    pad = jnp.zeros((HEADS, 1), F32)
    alog16 = jnp.concatenate([pad, a_a_log[0].reshape(HEADS, 1)], axis=0)
    dtb16 = jnp.concatenate([pad, a_dt_bias[0].reshape(HEADS, 1)], axis=0)
    q, k, v, z, gates = _gdn_in(xf, a_norm[0].reshape(1, d), w_main, wg_t, a_conv_w[0],
                                alog16, dtb16, seq_len)
    o = _delta(q, k, v, z, gates, a_out_norm[0].reshape(1, HEAD_DIM), batch, seq_len)
    xf = _proj_mlp(o, xf, a_w_out[0].astype(BF16), mlp_norm[0].reshape(1, d),
                   mlp_w1[0].astype(BF16), mlp_w2[0].astype(BF16))

    half = ROPE_HALF
    freqs = ROPE_THETA ** (-jnp.arange(half, dtype=F32) / half)
    freq_row = jnp.tile(freqs, LANES // half).reshape(1, LANES)
    cos, s1, s2 = _rope_tab(positions.reshape(t, 1), freq_row)
    k_gain = jnp.tile(k_norm, width // MAP_DIM).reshape(1, width)
    q_gain = jnp.tile(b_q_norm[0], width // MAP_DIM).reshape(1, width)
    kr, vv, qr = _attn_in(xf, kv_norm.reshape(1, d), b_norm[0].reshape(1, d),
                          w_kv.astype(BF16), b_w_q[0].astype(BF16), k_gain, q_gain, cos, s1, s2)
    lam_init = 0.8 - 0.6 * math.exp(-0.3 * 1)
    oa = _diff_attn(qr, kr, vv, b_q_norm[0].reshape(1, MAP_DIM), k_norm.reshape(1, MAP_DIM),
                    b_lambda[0], b_sub_norm[0].reshape(1, HEAD_DIM), batch, seq_len, lam_init)
    xf = _proj_mlp(oa, xf, b_w_out[0].astype(BF16), mlp_norm[1].reshape(1, d),
                   mlp_w1[1].astype(BF16), mlp_w2[1].astype(BF16))
    return xf.reshape(batch, seq_len, d)
```

```python
import functools
import math

import jax
import jax.numpy as jnp
from jax import lax
from jax.experimental import pallas as pl
from jax.experimental.pallas import tpu as pltpu

F32 = jnp.float32
BF16 = jnp.bfloat16

D_MODEL = 1024
HEADS = 8
HEAD_DIM = 128
MAP_DIM = 64
ROPE_HALF = MAP_DIM // 2
CONV_WIDTH = 4
D_FF = 4 * D_MODEL
ROPE_THETA = 10000.0
EPS = 1e-6
NEG_INF = -1e30
LOG2E = math.log2(math.e)
SCORE_BOUND_COEF = MAP_DIM * MAP_DIM ** -0.5 * LOG2E * 1.02
SCORE_BOUND_LIMIT = 100.0

LANES = 128
SUBLANES = 8
MXU_DIM = 256

ROW_TILE = 512
GDN_ROW_TILE = 512
FF_CHUNK = 1024
DELTA_CHUNK = 64
DELTA_GROUP = 8
DELTA_HEADS = 2
ATTN_TILE = 512
VMEM_LIMIT = 56 * 1024 * 1024

NT_DIMS = (((1,), (1,)), ((), ()))
TN_DIMS = (((0,), (0,)), ((), ()))


def _rms_hat(x):
    return x * lax.rsqrt(jnp.mean(x * x, axis=-1, keepdims=True) + EPS)


def _sigmoid(x):
    return 1.0 / (1.0 + jnp.exp(-x))


def _silu(x):
    h = 0.5 * x
    return h * jnp.tanh(h) + h


def _softplus(x):
    return jnp.maximum(x, 0.0) + jnp.log(1.0 + jnp.exp(-jnp.abs(x)))


def _dot(a, b):
    return jnp.dot(a, b, preferred_element_type=F32)


def _const_spec(shape):
    zeros = (0,) * len(shape)
    return pl.BlockSpec(shape, lambda *_: zeros, pipeline_mode=pl.Buffered(1))


def _gdn_in_kernel(x_ref, xh_ref, nw_ref, w_ref, wgt_ref, cw_ref, alog_ref, dtb_ref,
                   q_ref, k_ref, v_ref, z_ref, gate_ref, *, tiles_per_seq):
    tm = x_ref.shape[0]
    width = HEADS * HEAD_DIM
    nw = nw_ref[...]
    xn = (_rms_hat(x_ref[...]) * nw).astype(BF16)
    seq_start = (pl.program_id(0) % tiles_per_seq) == 0
    xh = jnp.where(seq_start, 0.0, _rms_hat(xh_ref[...]) * nw).astype(BF16)

    blocks = 3 * width // MXU_DIM

    def project(blk):
        w = w_ref[:, blk * MXU_DIM:(blk + 1) * MXU_DIM]
        return jnp.concatenate([_dot(xh, w), _dot(xn, w)], axis=0)

    xp = project(0)
    for blk in range(blocks):
        xp_next = project(blk + 1) if blk + 1 < blocks else None
        kind, col = divmod(blk * MXU_DIM, width)
        o_ref = (q_ref, k_ref, v_ref)[kind]
        cw = cw_ref[:, blk * MXU_DIM:(blk + 1) * MXU_DIM]
        c = cw[CONV_WIDTH - 1:CONV_WIDTH, :] * xp[SUBLANES:, :]
        for j in range(CONV_WIDTH - 1):
            c = c + cw[j:j + 1, :] * pltpu.roll(xp, CONV_WIDTH - 1 - j, 0)[SUBLANES:, :]
        a = _silu(c)
        if kind == 2:
            o_ref[:, col:col + MXU_DIM] = a
        else:
            scale = HEAD_DIM ** -0.5 if kind == 0 else 1.0
            for h in range(MXU_DIM // HEAD_DIM):
                ah = a[:, h * HEAD_DIM:(h + 1) * HEAD_DIM]
                inv = lax.rsqrt(jnp.sum(ah * ah, axis=-1, keepdims=True) + EPS)
                lo = col + h * HEAD_DIM
                o_ref[:, lo:lo + HEAD_DIM] = ah * (inv * scale)
        xp = xp_next

    z_ref[...] = _dot(xn, w_ref[:, 3 * width:4 * width])

    gt = lax.dot_general(wgt_ref[...], xn, NT_DIMS, preferred_element_type=F32)
    beta = _sigmoid(gt)
    decay = -jnp.exp(alog_ref[...]) * _softplus(gt + dtb_ref[...])
    row = lax.broadcasted_iota(jnp.int32, gt.shape, 0)
    gate_ref[...] = jnp.where(row < HEADS, beta, decay)


def _gdn_in(x, norm_w, w_main, wg_t, conv_w, alog16, dtb16, seq_len):
    t, d = x.shape
    tm = GDN_ROW_TILE
    width = HEADS * HEAD_DIM
    act = jax.ShapeDtypeStruct((t, width), F32)
    row_spec = pl.BlockSpec((tm, width), lambda i: (i, 0))
    return pl.pallas_call(
        functools.partial(_gdn_in_kernel, tiles_per_seq=seq_len // tm),
        grid=(t // tm,),
        in_specs=[
            pl.BlockSpec((tm, d), lambda i: (i, 0)),
            pl.BlockSpec((SUBLANES, d), lambda i: (jnp.maximum(i * (tm // SUBLANES) - 1, 0), 0)),
            _const_spec((1, d)),
            _const_spec(w_main.shape),
            _const_spec(wg_t.shape),
            _const_spec(conv_w.shape),
            _const_spec(alog16.shape),
            _const_spec(dtb16.shape),
        ],
        out_specs=[row_spec, row_spec, row_spec, row_spec,
                   pl.BlockSpec((2 * HEADS, tm), lambda i: (0, i))],
        out_shape=[act, act, act, act, jax.ShapeDtypeStruct((2 * HEADS, t), F32)],
        compiler_params=pltpu.CompilerParams(dimension_semantics=("parallel",),
                                             vmem_limit_bytes=VMEM_LIMIT),
        name="gdn_in",
    )(x, x, norm_w, w_main, wg_t, conv_w, alog16, dtb16)


def _delta_kernel(q_ref, k_ref, v_ref, z_ref, beta_ref, g_ref, onw_ref, o_ref, gc_ref,
                  m1_a, bt_a, qp_a, op_a, cd_a, m1_b, bt_b, qp_b, op_b, cd_b,
                  *, chunk, group, heads):
    c_len = chunk
    n_groups = q_ref.shape[0] // (c_len * group)
    bufs = ((m1_a, bt_a, qp_a, op_a, cd_a), (m1_b, bt_b, qp_b, op_b, cd_b))
    row = lax.broadcasted_iota(jnp.int32, (c_len, c_len), 0)
    col = lax.broadcasted_iota(jnp.int32, (c_len, c_len), 1)
    causal = row >= col
    strict = row > col
    eye = row == col
    upper = (row <= col).astype(F32)
    for hh in range(heads):
        gc_ref[hh] = jnp.dot(g_ref[hh], upper, precision=lax.Precision.HIGHEST,
                             preferred_element_type=F32)
    onw = onw_ref[...]
    n_steps = int(math.log2(c_len))
    chains = [(g, hh) for g in range(group) for hh in range(heads)]

    def to_col(r):
        return jnp.sum(jnp.where(eye, r, 0.0), axis=1, keepdims=True)

    def chunk_rows(c):
        return pl.ds(pl.multiple_of(c * c_len, c_len), c_len)

    def head_cols(hh):
        return slice(hh * HEAD_DIM, (hh + 1) * HEAD_DIM)

    def run(prep, adv, st):
        todo = list(range(group)) if adv is not None else []

        def advance_one(st):
            if not todo:
                return st
            g = todo.pop(0)
            grp, (m1_ref, bt_ref, qp_ref, op_ref, cd_ref) = adv
            rows = chunk_rows(grp * group + g)
            st16 = [x.astype(BF16) for x in st]
            slots = [g * heads + hh for hh in range(heads)]
            out = [lax.dot_general(qp_ref[slots[hh]], st16[hh], NT_DIMS, preferred_element_type=F32)
                   + op_ref[slots[hh]] for hh in range(heads)]
            st = [cd_ref[slots[hh], 0:1, :] * st[hh] - _dot(st16[hh], m1_ref[slots[hh]])
                  + bt_ref[slots[hh]] for hh in range(heads)]
            for hh in range(heads):
                zc = z_ref[rows, head_cols(hh)]
                o_ref[rows, head_cols(hh)] = _rms_hat(out[hh]) * onw * _silu(zc)
            return st

        if prep is not None:
            grp, (m1_ref, bt_ref, qp_ref, op_ref, cd_ref) = prep
            n = len(chains)
            cidx = [grp * group + g for g, _ in chains]
            q = [q_ref[chunk_rows(cidx[i]), head_cols(chains[i][1])] for i in range(n)]
            k = [k_ref[chunk_rows(cidx[i]), head_cols(chains[i][1])] for i in range(n)]
            gc_r = [gc_ref[chains[i][1], pl.ds(cidx[i], 1), :] for i in range(n)]
            beta_c = [to_col(beta_ref[chains[i][1], pl.ds(cidx[i], 1), :]) for i in range(n)]
            gc_c = [to_col(r) for r in gc_r]
            gc_last = [r[:, c_len - 1:c_len] for r in gc_r]
            e_c = [jnp.exp(x) for x in gc_c]
            kb = [k[i] * beta_c[i] for i in range(n)]
            s = [lax.dot_general(jnp.concatenate([kb[i], q[i]], axis=0).astype(BF16),
                                 k[i].astype(BF16), NT_DIMS, preferred_element_type=F32)
                 for i in range(n)]
            st = advance_one(st)
            decay = [jnp.where(causal, jnp.exp(jnp.where(causal, gc_c[i] - gc_r[i], 0.0)), 0.0)
                     for i in range(n)]
            p = [jnp.where(strict, -(s[i][:c_len] * decay[i]), 0.0) for i in range(n)]
            intra = [s[i][c_len:] * decay[i] for i in range(n)]
            y = [jnp.concatenate([v_ref[chunk_rows(cidx[i]), head_cols(chains[i][1])] * beta_c[i],
                                  kb[i] * e_c[i]], axis=1) for i in range(n)]
            for step in range(n_steps):
                p16 = [x.astype(BF16) for x in p]
                y = [y[i] + _dot(p16[i], y[i].astype(BF16)) for i in range(n)]
                if step + 1 < n_steps:
                    p = [_dot(x, x) for x in p16]
                st = advance_one(st)
            y16 = [x.astype(BF16) for x in y]
            kd16 = [(k[i] * jnp.exp(gc_last[i] - gc_c[i])).astype(BF16) for i in range(n)]
            mb = [lax.dot_general(y16[i], kd16[i], TN_DIMS, preferred_element_type=F32)
                  for i in range(n)]
            iu = [_dot(intra[i].astype(BF16), y16[i]) for i in range(n)]
            st = advance_one(st)
            for i in range(n):
                bt_ref[i] = mb[i][:HEAD_DIM]
                m1_ref[i] = mb[i][HEAD_DIM:].astype(BF16)
                qp_ref[i] = (q[i] * e_c[i] - iu[i][:, HEAD_DIM:]).astype(BF16)
                op_ref[i] = iu[i][:, :HEAD_DIM]
                cd_ref[i] = jnp.broadcast_to(jnp.exp(gc_last[i]), (SUBLANES, LANES))
        while todo:
            st = advance_one(st)
        return st

    st = run((0, bufs[0]), None, [jnp.zeros((HEAD_DIM, HEAD_DIM), F32) for _ in range(heads)])

    def body(i, st):
        st = run((2 * i + 1, bufs[1]), (2 * i, bufs[0]), st)
        return run((2 * i + 2, bufs[0]), (2 * i + 1, bufs[1]), st)

    st = lax.fori_loop(0, n_groups // 2 - 1, body, st)
    st = run((n_groups - 1, bufs[1]), (n_groups - 2, bufs[0]), st)
    run(None, (n_groups - 1, bufs[1]), st)


def _delta(q, k, v, z, gates, out_norm, batch, seq_len):
    t, width = q.shape
    c_len = DELTA_CHUNK
    nh = DELTA_HEADS
    n_chunks = seq_len // c_len
    assert n_chunks % (2 * DELTA_GROUP) == 0 and HEADS % nh == 0
    gates3 = gates.reshape(2 * HEADS, batch * n_chunks, c_len)
    seq_spec = pl.BlockSpec((seq_len, nh * HEAD_DIM), lambda b, h: (b, h))
    slots = DELTA_GROUP * nh
    buf_set = [pltpu.VMEM((slots, HEAD_DIM, HEAD_DIM), BF16),
               pltpu.VMEM((slots, HEAD_DIM, HEAD_DIM), F32),
               pltpu.VMEM((slots, c_len, HEAD_DIM), BF16),
               pltpu.VMEM((slots, c_len, HEAD_DIM), F32),
               pltpu.VMEM((slots, SUBLANES, LANES), F32)]
    return pl.pallas_call(
        functools.partial(_delta_kernel, chunk=c_len, group=DELTA_GROUP, heads=nh),
        grid=(batch, HEADS // nh),
        in_specs=[seq_spec, seq_spec, seq_spec, seq_spec,
                  pl.BlockSpec((nh, n_chunks, c_len), lambda b, h: (h, b, 0)),
                  pl.BlockSpec((nh, n_chunks, c_len), lambda b, h: (HEADS // nh + h, b, 0)),
                  pl.BlockSpec((1, HEAD_DIM), lambda b, h: (0, 0))],
        out_specs=seq_spec,
        out_shape=jax.ShapeDtypeStruct((t, width), F32),
        scratch_shapes=[pltpu.VMEM((nh, n_chunks, c_len), F32)] + buf_set + buf_set,
        compiler_params=pltpu.CompilerParams(dimension_semantics=("parallel", "parallel"),
                                             vmem_limit_bytes=VMEM_LIMIT),
        name="delta",
    )(q, k, v, z, gates3, gates3, out_norm)


def _proj_mlp_kernel(o_ref, x_ref, wo_ref, nw_ref, w1_ref, w2_ref, out_ref):
    x1 = x_ref[...] + _dot(o_ref[...].astype(BF16), wo_ref[...])
    xn = (_rms_hat(x1) * nw_ref[...]).astype(BF16)
    acc = x1
    for j in range(D_FF // FF_CHUNK):
        h = jnp.maximum(_dot(xn, w1_ref[:, j * FF_CHUNK:(j + 1) * FF_CHUNK]), 0.0)
        acc = acc + _dot((h * h).astype(BF16), w2_ref[j * FF_CHUNK:(j + 1) * FF_CHUNK, :])
    out_ref[...] = acc


def _proj_mlp(o, x, w_out, norm_w, w1, w2):
    t, d = x.shape
    tm = ROW_TILE
    return pl.pallas_call(
        _proj_mlp_kernel,
        grid=(t // tm,),
        in_specs=[pl.BlockSpec((tm, o.shape[1]), lambda i: (i, 0)),
                  pl.BlockSpec((tm, d), lambda i: (i, 0)),
                  _const_spec(w_out.shape), _const_spec((1, d)),
                  _const_spec(w1.shape), _const_spec(w2.shape)],
        out_specs=pl.BlockSpec((tm, d), lambda i: (i, 0)),
        out_shape=jax.ShapeDtypeStruct((t, d), F32),
        compiler_params=pltpu.CompilerParams(dimension_semantics=("parallel",),
                                             vmem_limit_bytes=VMEM_LIMIT),
        name="proj_mlp",
    )(o, x, w_out, norm_w, w1, w2)


def _rope_tab_kernel(pos_ref, freq_ref, cos_ref, s1_ref, s2_ref):
    ang = pos_ref[...].astype(F32) * freq_ref[...]
    sin = jnp.sin(ang)
    lane = lax.broadcasted_iota(jnp.int32, ang.shape, 1)
    first_half = (lane % MAP_DIM) < ROPE_HALF
    cos_ref[...] = jnp.cos(ang)
    s1_ref[...] = jnp.where(first_half, -sin, 0.0)
    s2_ref[...] = jnp.where(first_half, 0.0, sin)


def _rope_tab(pos_col, freq_row):
    t = pos_col.shape[0]
    tm = 2048
    tab = jax.ShapeDtypeStruct((t, LANES), F32)
    spec = pl.BlockSpec((tm, LANES), lambda i: (i, 0))
    return pl.pallas_call(
        _rope_tab_kernel,
        grid=(t // tm,),
        in_specs=[pl.BlockSpec((tm, 1), lambda i: (i, 0)), _const_spec((1, LANES))],
        out_specs=[spec, spec, spec],
        out_shape=[tab, tab, tab],
        compiler_params=pltpu.CompilerParams(dimension_semantics=("parallel",)),
        name="rope_tab",
    )(pos_col, freq_row)


def _attn_in_kernel(x_ref, kvn_ref, qnw_ref, wkv_ref, wq_ref, kg_ref, qg_ref,
                    cos_ref, s1_ref, s2_ref, k_ref, v_ref, q_ref):
    width = HEADS * HEAD_DIM
    xhat = _rms_hat(x_ref[...])
    kvn = (xhat * kvn_ref[...]).astype(BF16)
    qn = (xhat * qnw_ref[...]).astype(BF16)
    cos = cos_ref[...]
    s1 = s1_ref[...]
    s2 = s2_ref[...]
    r = lax.broadcasted_iota(jnp.int32, (MXU_DIM, MXU_DIM), 0) // MAP_DIM
    c = lax.broadcasted_iota(jnp.int32, (MXU_DIM, MXU_DIM), 1) // MAP_DIM
    group_ones = (r == c).astype(BF16)

    def norm_rope(raw, gain, scale, o_ref):
        for s in range(width // MXU_DIM):
            blk = raw[:, s * MXU_DIM:(s + 1) * MXU_DIM]
            ss = _dot((blk * blk).astype(BF16), group_ones)
            nb = blk * lax.rsqrt(ss * (1.0 / MAP_DIM) + EPS) * gain[:, s * MXU_DIM:(s + 1) * MXU_DIM]
            for hh in range(MXU_DIM // LANES):
                xb = nb[:, hh * LANES:(hh + 1) * LANES]
                rot = xb * cos + pltpu.roll(xb, LANES - ROPE_HALF, 1) * s1 + pltpu.roll(xb, ROPE_HALF, 1) * s2
                lo = s * MXU_DIM + hh * LANES
                o_ref[:, lo:lo + LANES] = (rot * scale).astype(o_ref.dtype)

    norm_rope(_dot(kvn, wkv_ref[:, :width]), kg_ref[...], 1.0, k_ref)
    v_ref[...] = _dot(kvn, wkv_ref[:, width:]).astype(v_ref.dtype)
    norm_rope(_dot(qn, wq_ref[...]), qg_ref[...], MAP_DIM ** -0.5 * LOG2E, q_ref)


def _attn_in(x, kv_norm, q_norm_w, w_kv, w_q, k_gain, q_gain, cos, s1, s2):
    t, d = x.shape
    tm = ROW_TILE
    width = HEADS * HEAD_DIM
    act = jax.ShapeDtypeStruct((t, width), BF16)
    row_spec = pl.BlockSpec((tm, width), lambda i: (i, 0))
    tab_spec = pl.BlockSpec((tm, LANES), lambda i: (i, 0))
    return pl.pallas_call(
        _attn_in_kernel,
        grid=(t // tm,),
        in_specs=[pl.BlockSpec((tm, d), lambda i: (i, 0)),
                  _const_spec((1, d)), _const_spec((1, d)),
                  _const_spec(w_kv.shape), _const_spec(w_q.shape),
                  _const_spec((1, width)), _const_spec((1, width)),
                  tab_spec, tab_spec, tab_spec],
        out_specs=[row_spec, row_spec, row_spec],
        out_shape=[act, act, act],
        compiler_params=pltpu.CompilerParams(dimension_semantics=("parallel",),
                                             vmem_limit_bytes=VMEM_LIMIT),
        name="attn_in",
    )(x, kv_norm, q_norm_w, w_kv, w_q, k_gain, q_gain, cos, s1, s2)


def _diff_attn_kernel(q_ref, k_ref, v_ref, qg_ref, kg_ref, lam_ref, snw_ref, o_ref,
                      m_ref, l_ref, acc_ref, *, tq, lam_init):
    n_q = q_ref.shape[0] // tq
    half = tq // 2
    lane = lax.broadcasted_iota(jnp.int32, (tq, HEAD_DIM), 1)
    lp = lam_ref[...]
    lam = (jnp.exp(jnp.sum(lp[0:1] * lp[1:2], axis=-1, keepdims=True))
           - jnp.exp(jnp.sum(lp[2:3] * lp[3:4], axis=-1, keepdims=True)) + lam_init)
    snw = snw_ref[...]

    def q_maps(rows):
        q = q_ref[rows, :]
        zero = jnp.zeros_like(q)
        return (jnp.where(lane < MAP_DIM, q, zero), jnp.where(lane < MAP_DIM, zero, q))

    def scores(qm, rows):
        ks = k_ref[rows, :]
        return [lax.dot_general(x, ks, NT_DIMS, preferred_element_type=F32) for x in qm]

    def finish(rows, acc0, acc1, l0, l1):
        o = acc0 / l0 - lam * (acc1 / l1)
        o_ref[rows, :] = (_rms_hat(o) * snw * (1.0 - lam_init)).astype(o_ref.dtype)

    bound = SCORE_BOUND_COEF * jnp.max(jnp.abs(qg_ref[...])) * jnp.max(jnp.abs(kg_ref[...]))
    bounded = bound <= SCORE_BOUND_LIMIT

    @pl.when(bounded)
    def _():
        def lane_sums(p):
            out = p[:, 0:LANES]
            for b in range(1, p.shape[1] // LANES):
                out = out + p[:, b * LANES:(b + 1) * LANES]
            return out

        def plus(a, b):
            return b if a is None else a + b

        for qi in range(n_q):
            q0 = qi * tq
            qm = q_maps(slice(q0, q0 + tq))
            acc = [None, None]
            lsum = [None, None]
            s_next = scores(qm, slice(0, tq)) if qi > 0 else None
            for j in range(qi):
                s = s_next
                s_next = scores(qm, slice((j + 1) * tq, (j + 2) * tq)) if j + 1 < qi else None
                vs = v_ref[j * tq:(j + 1) * tq, :]
                for mi in range(2):
                    p = jnp.exp2(s[mi])
                    lsum[mi] = plus(lsum[mi], lane_sums(p))
                    acc[mi] = plus(acc[mi], _dot(p.astype(BF16), vs))
            for bi, ncols in enumerate((half, tq)):
                rows = slice(bi * half, (bi + 1) * half)
                sd = scores([x[rows, :] for x in qm], slice(q0, q0 + ncols))
                r = lax.broadcasted_iota(jnp.int32, (half, ncols), 0) + bi * half
                c = lax.broadcasted_iota(jnp.int32, (half, ncols), 1)
                vs = v_ref[q0:q0 + ncols, :]
                fin = []
                for mi in range(2):
                    p = jnp.where(r >= c, jnp.exp2(sd[mi]), 0.0)
                    below_l = None if lsum[mi] is None else lsum[mi][rows, :]
                    below_a = None if acc[mi] is None else acc[mi][rows, :]
                    fin.append((plus(below_a, _dot(p.astype(BF16), vs)),
                                jnp.sum(plus(below_l, lane_sums(p)), axis=-1, keepdims=True)))
                finish(slice(q0 + bi * half, q0 + (bi + 1) * half),
                       fin[0][0], fin[1][0], fin[0][1], fin[1][1])

    @pl.when(jnp.logical_not(bounded))
    def _():
        r = lax.broadcasted_iota(jnp.int32, (tq, tq), 0)
        c = lax.broadcasted_iota(jnp.int32, (tq, tq), 1)

        def q_tile(qi, carry):
            q_rows = pl.ds(pl.multiple_of(qi * tq, tq), tq)
            qm = q_maps(q_rows)
            m_ref[...] = jnp.full(m_ref.shape, NEG_INF, F32)
            l_ref[...] = jnp.zeros(l_ref.shape, F32)
            acc_ref[...] = jnp.zeros(acc_ref.shape, F32)

            def step(j, masked):
                kv_rows = pl.ds(pl.multiple_of(j * tq, tq), tq)
                s = scores(qm, kv_rows)
                vs = v_ref[kv_rows, :]
                for mi in range(2):
                    sm = jnp.where(r >= c, s[mi], NEG_INF) if masked else s[mi]
                    m_old = m_ref[mi]
                    m_new = jnp.maximum(m_old, jnp.max(sm, axis=-1, keepdims=True))
                    alpha = jnp.exp2(m_old - m_new)
                    p = jnp.exp2(sm - m_new[:, 0:1])
                    l_ref[mi] = alpha * l_ref[mi] + jnp.sum(p, axis=-1, keepdims=True)
                    acc_ref[mi] = alpha * acc_ref[mi] + _dot(p.astype(BF16), vs)
                    m_ref[mi] = m_new

            def full_body(j, carry):
                step(j, False)
                return carry

            lax.fori_loop(0, qi, full_body, 0)
            step(qi, True)
            finish(q_rows, acc_ref[0], acc_ref[1], l_ref[0], l_ref[1])
            return carry

        lax.fori_loop(0, n_q, q_tile, 0)


def _diff_attn(q, k, v, q_gain, k_gain, lam_params, sub_norm, batch, seq_len, lam_init):
    t, width = q.shape
    tq = ATTN_TILE
    seq_spec = pl.BlockSpec((seq_len, HEAD_DIM), lambda b, h: (b, h))
    return pl.pallas_call(
        functools.partial(_diff_attn_kernel, tq=tq, lam_init=lam_init),
        grid=(batch, HEADS),
        in_specs=[seq_spec, seq_spec, seq_spec,
                  _const_spec(q_gain.shape), _const_spec(k_gain.shape),
                  _const_spec(lam_params.shape), _const_spec((1, HEAD_DIM))],
        out_specs=seq_spec,
        out_shape=jax.ShapeDtypeStruct((t, width), BF16),
        scratch_shapes=[pltpu.VMEM((2, tq, LANES), F32), pltpu.VMEM((2, tq, LANES), F32),
                        pltpu.VMEM((2, tq, HEAD_DIM), F32)],
        compiler_params=pltpu.CompilerParams(dimension_semantics=("parallel", "parallel"),
                                             vmem_limit_bytes=VMEM_LIMIT),
        name="diff_attn",
    )(q, k, v, q_gain, k_gain, lam_params, sub_norm)


def kernel(x, positions, a_norm, a_w_in, a_conv_w, a_a_log, a_dt_bias, a_out_norm, a_w_out,
           kv_norm, w_kv, k_norm, b_norm, b_w_q, b_q_norm, b_lambda, b_sub_norm, b_w_out,
           mlp_norm, mlp_w1, mlp_w2):
    batch, seq_len, d = x.shape
    assert d == D_MODEL and a_norm.shape[0] == 1 and b_norm.shape[0] == 1
    assert seq_len % ROW_TILE == 0 and seq_len % ATTN_TILE == 0 and seq_len % DELTA_CHUNK == 0
    t = batch * seq_len
    width = HEADS * HEAD_DIM
    xf = x.reshape(t, d)

    w_in = a_w_in[0]
    w_main = w_in[:, :4 * width].astype(BF16)
    wg_t = w_in[:, 4 * width:].T.astype(BF16)
    pad = jnp.zeros((HEADS, 1), F32)
    alog16 = jnp.concatenate([pad, a_a_log[0].reshape(HEADS, 1)], axis=0)
    dtb16 = jnp.concatenate([pad, a_dt_bias[0].reshape(HEADS, 1)], axis=0)
    q, k, v, z, gates = _gdn_in(xf, a_norm[0].reshape(1, d), w_main, wg_t, a_conv_w[0],
                                alog16, dtb16, seq_len)
    o = _delta(q, k, v, z, gates, a_out_norm[0].reshape(1, HEAD_DIM), batch, seq_len)
    xf = _proj_mlp(o, xf, a_w_out[0].astype(BF16), mlp_norm[0].reshape(1, d),
                   mlp_w1[0].astype(BF16), mlp_w2[0].astype(BF16))

    half = ROPE_HALF
    freqs = ROPE_THETA ** (-jnp.arange(half, dtype=F32) / half)
    freq_row = jnp.tile(freqs, LANES // half).reshape(1, LANES)
    cos, s1, s2 = _rope_tab(positions.reshape(t, 1), freq_row)
    k_gain = jnp.tile(k_norm, width // MAP_DIM).reshape(1, width)
    q_gain = jnp.tile(b_q_norm[0], width // MAP_DIM).reshape(1, width)
    kr, vv, qr = _attn_in(xf, kv_norm.reshape(1, d), b_norm[0].reshape(1, d),
                          w_kv.astype(BF16), b_w_q[0].astype(BF16), k_gain, q_gain, cos, s1, s2)
    lam_init = 0.8 - 0.6 * math.exp(-0.3 * 1)
    oa = _diff_attn(qr, kr, vv, b_q_norm[0].reshape(1, MAP_DIM), k_norm.reshape(1, MAP_DIM),
                    b_lambda[0], b_sub_norm[0].reshape(1, HEAD_DIM), batch, seq_len, lam_init)
    xf = _proj_mlp(oa, xf, b_w_out[0].astype(BF16), mlp_norm[1].reshape(1, d),
                   mlp_w1[1].astype(BF16), mlp_w2[1].astype(BF16))
    return xf.reshape(batch, seq_len, d)
```

```python
import functools
import math

import jax
import jax.numpy as jnp
from jax import lax
from jax.experimental import pallas as pl
from jax.experimental.pallas import tpu as pltpu

F32 = jnp.float32
BF16 = jnp.bfloat16

D_MODEL = 1024
HEADS = 8
HEAD_DIM = 128
MAP_DIM = 64
ROPE_HALF = MAP_DIM // 2
CONV_WIDTH = 4
D_FF = 4 * D_MODEL
ROPE_THETA = 10000.0
EPS = 1e-6
NEG_INF = -1e30
LOG2E = math.log2(math.e)
SCORE_BOUND_COEF = MAP_DIM * MAP_DIM ** -0.5 * LOG2E * 1.02
SCORE_BOUND_LIMIT = 100.0

LANES = 128
SUBLANES = 8
MXU_DIM = 256

ROW_TILE = 512
GDN_ROW_TILE = 512
GDN_COL_BLOCK = 2 * MXU_DIM
FF_CHUNK = 1024
DELTA_CHUNK = 64
DELTA_GROUP = 8
DELTA_HEADS = 2
ATTN_TILE = 512
VMEM_LIMIT = 56 * 1024 * 1024

NT_DIMS = (((1,), (1,)), ((), ()))
TN_DIMS = (((0,), (0,)), ((), ()))


def _rms_hat(x):
    return x * lax.rsqrt(jnp.mean(x * x, axis=-1, keepdims=True) + EPS)


def _sigmoid(x):
    return 1.0 / (1.0 + jnp.exp(-x))


def _silu(x):
    h = 0.5 * x
    return h * jnp.tanh(h) + h


def _softplus(x):
    return jnp.maximum(x, 0.0) + jnp.log(1.0 + jnp.exp(-jnp.abs(x)))


def _dot(a, b):
    return jnp.dot(a, b, preferred_element_type=F32)


def _const_spec(shape):
    zeros = (0,) * len(shape)
    return pl.BlockSpec(shape, lambda *_: zeros, pipeline_mode=pl.Buffered(1))


def _gdn_in_kernel(x_ref, nw_ref, w_ref, wgt_ref, cw_ref, alog_ref, dtb_ref,
                   q_ref, k_ref, v_ref, z_ref, gate_ref, xn_ref, tail_ref, *, tiles_per_seq):
    tm = x_ref.shape[0]
    width = HEADS * HEAD_DIM
    step = pl.program_id(0)

    @pl.when(step == 0)
    def _():
        tail_ref[...] = jnp.zeros(tail_ref.shape, F32)

    xn_ref[...] = (_rms_hat(x_ref[...]) * nw_ref[...]).astype(BF16)
    seq_start = (step % tiles_per_seq) == 0

    cb = GDN_COL_BLOCK
    blocks = 3 * width // cb

    def project(blk):
        return _dot(xn_ref[...], w_ref[:, blk * cb:(blk + 1) * cb])

    p = project(0)
    for blk in range(blocks):
        p_next = project(blk + 1) if blk + 1 < blocks else None
        cols = slice(blk * cb, (blk + 1) * cb)
        prev = jnp.where(seq_start, 0.0, tail_ref[:, cols])
        tail_ref[:, cols] = p[tm - SUBLANES:, :]
        xp = jnp.concatenate([prev, p], axis=0)
        cw = cw_ref[:, cols]
        c = cw[CONV_WIDTH - 1:CONV_WIDTH, :] * p
        for j in range(CONV_WIDTH - 1):
            c = c + cw[j:j + 1, :] * pltpu.roll(xp, CONV_WIDTH - 1 - j, 0)[SUBLANES:, :]
        kind, col = divmod(blk * cb, width)
        (q_ref, k_ref, v_ref)[kind][:, col:col + cb] = c
        p = p_next

    z_ref[...] = _dot(xn_ref[...], w_ref[:, 3 * width:4 * width])

    gt = lax.dot_general(wgt_ref[...], xn_ref[...], NT_DIMS, preferred_element_type=F32)
    beta = _sigmoid(gt)
    decay = -jnp.exp(alog_ref[...]) * _softplus(gt + dtb_ref[...])
    row = lax.broadcasted_iota(jnp.int32, gt.shape, 0)
    gate_ref[...] = jnp.where(row < HEADS, beta, decay)


def _gdn_in(x, norm_w, w_main, wg_t, conv_w, alog16, dtb16, seq_len):
    t, d = x.shape
    tm = GDN_ROW_TILE
    width = HEADS * HEAD_DIM
    act = jax.ShapeDtypeStruct((t, width), F32)
    row_spec = pl.BlockSpec((tm, width), lambda i: (i, 0))
    return pl.pallas_call(
        functools.partial(_gdn_in_kernel, tiles_per_seq=seq_len // tm),
        grid=(t // tm,),
        in_specs=[
            pl.BlockSpec((tm, d), lambda i: (i, 0)),
            _const_spec((1, d)),
            _const_spec(w_main.shape),
            _const_spec(wg_t.shape),
            _const_spec(conv_w.shape),
            _const_spec(alog16.shape),
            _const_spec(dtb16.shape),
        ],
        out_specs=[row_spec, row_spec, row_spec, row_spec,
                   pl.BlockSpec((2 * HEADS, tm), lambda i: (0, i))],
        out_shape=[act, act, act, act, jax.ShapeDtypeStruct((2 * HEADS, t), F32)],
        scratch_shapes=[pltpu.VMEM((tm, d), BF16), pltpu.VMEM((SUBLANES, 3 * width), F32)],
        compiler_params=pltpu.CompilerParams(dimension_semantics=("arbitrary",),
                                             vmem_limit_bytes=VMEM_LIMIT),
        name="gdn_in",
    )(x, norm_w, w_main, wg_t, conv_w, alog16, dtb16)


def _delta_kernel(q_ref, k_ref, v_ref, z_ref, beta_ref, g_ref, onw_ref, o_ref, gc_ref,
                  m1_a, bt_a, qp_a, op_a, cd_a, m1_b, bt_b, qp_b, op_b, cd_b,
                  *, chunk, group, heads):
    c_len = chunk
    n_groups = q_ref.shape[0] // (c_len * group)
    bufs = ((m1_a, bt_a, qp_a, op_a, cd_a), (m1_b, bt_b, qp_b, op_b, cd_b))
    row = lax.broadcasted_iota(jnp.int32, (c_len, c_len), 0)
    col = lax.broadcasted_iota(jnp.int32, (c_len, c_len), 1)
    causal = row >= col
    strict = row > col
    eye = row == col
    upper = (row <= col).astype(F32)
    for hh in range(heads):
        gc_ref[hh] = jnp.dot(g_ref[hh], upper, precision=lax.Precision.HIGHEST,
                             preferred_element_type=F32)
    onw = onw_ref[...]
    n_steps = int(math.log2(c_len))
    chains = [(g, hh) for g in range(group) for hh in range(heads)]

    def to_col(r):
        return jnp.sum(jnp.where(eye, r, 0.0), axis=1, keepdims=True)

    def chunk_rows(c):
        return pl.ds(pl.multiple_of(c * c_len, c_len), c_len)

    def head_cols(hh):
        return slice(hh * HEAD_DIM, (hh + 1) * HEAD_DIM)

    def unit_rows(a, scale):
        return a * (lax.rsqrt(jnp.sum(a * a, axis=-1, keepdims=True) + EPS) * scale)

    def run(prep, adv, st):
        todo = list(range(group)) if adv is not None else []

        def advance_one(st):
            if not todo:
                return st
            g = todo.pop(0)
            grp, (m1_ref, bt_ref, qp_ref, op_ref, cd_ref) = adv
            rows = chunk_rows(grp * group + g)
            st16 = [x.astype(BF16) for x in st]
            slots = [g * heads + hh for hh in range(heads)]
            out = [lax.dot_general(qp_ref[slots[hh]], st16[hh], NT_DIMS, preferred_element_type=F32)
                   + op_ref[slots[hh]] for hh in range(heads)]
            st = [cd_ref[slots[hh], 0:1, :] * st[hh] - _dot(st16[hh], m1_ref[slots[hh]])
                  + bt_ref[slots[hh]] for hh in range(heads)]
            for hh in range(heads):
                zc = z_ref[rows, head_cols(hh)]
                o_ref[rows, head_cols(hh)] = _rms_hat(out[hh]) * onw * _silu(zc)
            return st

        if prep is not None:
            grp, (m1_ref, bt_ref, qp_ref, op_ref, cd_ref) = prep
            n = len(chains)
            cidx = [grp * group + g for g, _ in chains]
            q = [unit_rows(_silu(q_ref[chunk_rows(cidx[i]), head_cols(chains[i][1])]),
                           HEAD_DIM ** -0.5) for i in range(n)]
            k = [unit_rows(_silu(k_ref[chunk_rows(cidx[i]), head_cols(chains[i][1])]), 1.0)
                 for i in range(n)]
            gc_r = [gc_ref[chains[i][1], pl.ds(cidx[i], 1), :] for i in range(n)]
            beta_c = [to_col(beta_ref[chains[i][1], pl.ds(cidx[i], 1), :]) for i in range(n)]
            gc_c = [to_col(r) for r in gc_r]
            gc_last = [r[:, c_len - 1:c_len] for r in gc_r]
            e_c = [jnp.exp(x) for x in gc_c]
            kb = [k[i] * beta_c[i] for i in range(n)]
            s = [lax.dot_general(jnp.concatenate([kb[i], q[i]], axis=0).astype(BF16),
                                 k[i].astype(BF16), NT_DIMS, preferred_element_type=F32)
                 for i in range(n)]
            st = advance_one(st)
            decay = [jnp.where(causal, jnp.exp(jnp.where(causal, gc_c[i] - gc_r[i], 0.0)), 0.0)
                     for i in range(n)]
            p = [jnp.where(strict, -(s[i][:c_len] * decay[i]), 0.0) for i in range(n)]
            intra = [s[i][c_len:] * decay[i] for i in range(n)]
            y = [jnp.concatenate([_silu(v_ref[chunk_rows(cidx[i]), head_cols(chains[i][1])]) * beta_c[i],
                                  kb[i] * e_c[i]], axis=1) for i in range(n)]
            for step in range(n_steps):
                p16 = [x.astype(BF16) for x in p]
                y = [y[i] + _dot(p16[i], y[i].astype(BF16)) for i in range(n)]
                if step + 1 < n_steps:
                    p = [_dot(x, x) for x in p16]
                st = advance_one(st)
            y16 = [x.astype(BF16) for x in y]
            kd16 = [(k[i] * jnp.exp(gc_last[i] - gc_c[i])).astype(BF16) for i in range(n)]
            mb = [lax.dot_general(y16[i], kd16[i], TN_DIMS, preferred_element_type=F32)
                  for i in range(n)]
            iu = [_dot(intra[i].astype(BF16), y16[i]) for i in range(n)]
            st = advance_one(st)
            for i in range(n):
                bt_ref[i] = mb[i][:HEAD_DIM]
                m1_ref[i] = mb[i][HEAD_DIM:].astype(BF16)
                qp_ref[i] = (q[i] * e_c[i] - iu[i][:, HEAD_DIM:]).astype(BF16)
                op_ref[i] = iu[i][:, :HEAD_DIM]
                cd_ref[i] = jnp.broadcast_to(jnp.exp(gc_last[i]), (SUBLANES, LANES))
        while todo:
            st = advance_one(st)
        return st

    st = run((0, bufs[0]), None, [jnp.zeros((HEAD_DIM, HEAD_DIM), F32) for _ in range(heads)])

    def body(i, st):
        st = run((2 * i + 1, bufs[1]), (2 * i, bufs[0]), st)
        return run((2 * i + 2, bufs[0]), (2 * i + 1, bufs[1]), st)

    st = lax.fori_loop(0, n_groups // 2 - 1, body, st)
    st = run((n_groups - 1, bufs[1]), (n_groups - 2, bufs[0]), st)
    run(None, (n_groups - 1, bufs[1]), st)


def _delta(q, k, v, z, gates, out_norm, batch, seq_len):
    t, width = q.shape
    c_len = DELTA_CHUNK
    nh = DELTA_HEADS
    n_chunks = seq_len // c_len
    assert n_chunks % (2 * DELTA_GROUP) == 0 and HEADS % nh == 0
    gates3 = gates.reshape(2 * HEADS, batch * n_chunks, c_len)
    seq_spec = pl.BlockSpec((seq_len, nh * HEAD_DIM), lambda b, h: (b, h))
    slots = DELTA_GROUP * nh
    buf_set = [pltpu.VMEM((slots, HEAD_DIM, HEAD_DIM), BF16),
               pltpu.VMEM((slots, HEAD_DIM, HEAD_DIM), F32),
               pltpu.VMEM((slots, c_len, HEAD_DIM), BF16),
               pltpu.VMEM((slots, c_len, HEAD_DIM), F32),
               pltpu.VMEM((slots, SUBLANES, LANES), F32)]
    return pl.pallas_call(
        functools.partial(_delta_kernel, chunk=c_len, group=DELTA_GROUP, heads=nh),
        grid=(batch, HEADS // nh),
        in_specs=[seq_spec, seq_spec, seq_spec, seq_spec,
                  pl.BlockSpec((nh, n_chunks, c_len), lambda b, h: (h, b, 0)),
                  pl.BlockSpec((nh, n_chunks, c_len), lambda b, h: (HEADS // nh + h, b, 0)),
                  pl.BlockSpec((1, HEAD_DIM), lambda b, h: (0, 0))],
        out_specs=seq_spec,
        out_shape=jax.ShapeDtypeStruct((t, width), F32),
        scratch_shapes=[pltpu.VMEM((nh, n_chunks, c_len), F32)] + buf_set + buf_set,
        compiler_params=pltpu.CompilerParams(dimension_semantics=("parallel", "parallel"),
                                             vmem_limit_bytes=VMEM_LIMIT),
        name="delta",
    )(q, k, v, z, gates3, gates3, out_norm)


def _proj_mlp_kernel(o_ref, x_ref, wo_ref, nw_ref, w1_ref, w2_ref, out_ref):
    x1 = x_ref[...] + _dot(o_ref[...].astype(BF16), wo_ref[...])
    xn = (_rms_hat(x1) * nw_ref[...]).astype(BF16)
    acc = x1
    for j in range(D_FF // FF_CHUNK):
        h = jnp.maximum(_dot(xn, w1_ref[:, j * FF_CHUNK:(j + 1) * FF_CHUNK]), 0.0)
        acc = acc + _dot((h * h).astype(BF16), w2_ref[j * FF_CHUNK:(j + 1) * FF_CHUNK, :])
    out_ref[...] = acc


def _layer_spec(stacked, layer):
    return pl.BlockSpec((None,) + stacked.shape[1:], lambda *_: (layer, 0, 0),
                        pipeline_mode=pl.Buffered(1))


def _proj_mlp(o, x, w_out, norm_w, w1, w2, layer):
    t, d = x.shape
    tm = ROW_TILE
    return pl.pallas_call(
        _proj_mlp_kernel,
        grid=(t // tm,),
        in_specs=[pl.BlockSpec((tm, o.shape[1]), lambda i: (i, 0)),
                  pl.BlockSpec((tm, d), lambda i: (i, 0)),
                  _const_spec(w_out.shape), _const_spec((1, d)),
                  _layer_spec(w1, layer), _layer_spec(w2, layer)],
        out_specs=pl.BlockSpec((tm, d), lambda i: (i, 0)),
        out_shape=jax.ShapeDtypeStruct((t, d), F32),
        compiler_params=pltpu.CompilerParams(dimension_semantics=("parallel",),
                                             vmem_limit_bytes=VMEM_LIMIT),
        name="proj_mlp",
    )(o, x, w_out, norm_w, w1, w2)


def _rope_tab_kernel(pos_ref, freq_ref, cos_ref, s1_ref, s2_ref):
    ang = pos_ref[...].astype(F32) * freq_ref[...]
    sin = jnp.sin(ang)
    lane = lax.broadcasted_iota(jnp.int32, ang.shape, 1)
    first_half = (lane % MAP_DIM) < ROPE_HALF
    cos_ref[...] = jnp.cos(ang)
    s1_ref[...] = jnp.where(first_half, -sin, 0.0)
    s2_ref[...] = jnp.where(first_half, 0.0, sin)


def _rope_tab(pos_col, freq_row):
    t = pos_col.shape[0]
    tm = 2048
    tab = jax.ShapeDtypeStruct((t, LANES), F32)
    spec = pl.BlockSpec((tm, LANES), lambda i: (i, 0))
    return pl.pallas_call(
        _rope_tab_kernel,
        grid=(t // tm,),
        in_specs=[pl.BlockSpec((tm, 1), lambda i: (i, 0)), _const_spec((1, LANES))],
        out_specs=[spec, spec, spec],
        out_shape=[tab, tab, tab],
        compiler_params=pltpu.CompilerParams(dimension_semantics=("parallel",)),
        name="rope_tab",
    )(pos_col, freq_row)


def _attn_in_kernel(x_ref, kvn_ref, qnw_ref, wkv_ref, wq_ref, kg_ref, qg_ref,
                    cos_ref, s1_ref, s2_ref, k_ref, v_ref, q_ref, kvx_ref, qx_ref):
    width = HEADS * HEAD_DIM
    cb = 2 * MXU_DIM
    xhat = _rms_hat(x_ref[...])
    kvx_ref[...] = (xhat * kvn_ref[...]).astype(BF16)
    qx_ref[...] = (xhat * qnw_ref[...]).astype(BF16)
    cos = cos_ref[...]
    s1 = s1_ref[...]
    s2 = s2_ref[...]
    r = lax.broadcasted_iota(jnp.int32, (MXU_DIM, MXU_DIM), 0) // MAP_DIM
    c = lax.broadcasted_iota(jnp.int32, (MXU_DIM, MXU_DIM), 1) // MAP_DIM
    group_ones = (r == c).astype(BF16)

    def norm_rope(raw, gain, scale, o_ref, col):
        for s in range(raw.shape[1] // MXU_DIM):
            blk = raw[:, s * MXU_DIM:(s + 1) * MXU_DIM]
            ss = _dot((blk * blk).astype(BF16), group_ones)
            lo = col + s * MXU_DIM
            nb = blk * lax.rsqrt(ss * (1.0 / MAP_DIM) + EPS) * gain[:, lo:lo + MXU_DIM]
            for hh in range(MXU_DIM // LANES):
                xb = nb[:, hh * LANES:(hh + 1) * LANES]
                rot = xb * cos + pltpu.roll(xb, LANES - ROPE_HALF, 1) * s1 + pltpu.roll(xb, ROPE_HALF, 1) * s2
                o_ref[:, lo + hh * LANES:lo + (hh + 1) * LANES] = (rot * scale).astype(o_ref.dtype)

    plan = []
    for col in range(0, width, cb):
        plan += [("k", col), ("q", col)]
    plan += [("v", col) for col in range(0, width, cb)]

    def project(item):
        kind, col = item
        if kind == "q":
            return _dot(qx_ref[...], wq_ref[:, col:col + cb])
        base = 0 if kind == "k" else width
        return _dot(kvx_ref[...], wkv_ref[:, base + col:base + col + cb])

    raw = project(plan[0])
    for i, (kind, col) in enumerate(plan):
        raw_next = project(plan[i + 1]) if i + 1 < len(plan) else None
        if kind == "v":
            v_ref[:, col:col + cb] = raw.astype(v_ref.dtype)
        elif kind == "k":
            norm_rope(raw, kg_ref[...], 1.0, k_ref, col)
        else:
            norm_rope(raw, qg_ref[...], MAP_DIM ** -0.5 * LOG2E, q_ref, col)
        raw = raw_next


def _attn_in(x, kv_norm, q_norm_w, w_kv, w_q, k_gain, q_gain, cos, s1, s2):
    t, d = x.shape
    tm = ROW_TILE
    width = HEADS * HEAD_DIM
    act = jax.ShapeDtypeStruct((t, width), BF16)
    row_spec = pl.BlockSpec((tm, width), lambda i: (i, 0))
    tab_spec = pl.BlockSpec((tm, LANES), lambda i: (i, 0))
    return pl.pallas_call(
        _attn_in_kernel,
        grid=(t // tm,),
        in_specs=[pl.BlockSpec((tm, d), lambda i: (i, 0)),
                  _const_spec((1, d)), _const_spec((1, d)),
                  _const_spec(w_kv.shape), _const_spec(w_q.shape),
                  _const_spec((1, width)), _const_spec((1, width)),
                  tab_spec, tab_spec, tab_spec],
        out_specs=[row_spec, row_spec, row_spec],
        out_shape=[act, act, act],
        scratch_shapes=[pltpu.VMEM((tm, d), BF16), pltpu.VMEM((tm, d), BF16)],
        compiler_params=pltpu.CompilerParams(dimension_semantics=("parallel",),
                                             vmem_limit_bytes=VMEM_LIMIT),
        name="attn_in",
    )(x, kv_norm, q_norm_w, w_kv, w_q, k_gain, q_gain, cos, s1, s2)


def _diff_attn_kernel(q_ref, k_ref, v_ref, qg_ref, kg_ref, lam_ref, snw_ref, o_ref,
                      m_ref, l_ref, acc_ref, *, tq, lam_init):
    n_q = q_ref.shape[0] // tq
    half = tq // 2
    lane = lax.broadcasted_iota(jnp.int32, (tq, HEAD_DIM), 1)
    lp = lam_ref[...]
    lam = (jnp.exp(jnp.sum(lp[0:1] * lp[1:2], axis=-1, keepdims=True))
           - jnp.exp(jnp.sum(lp[2:3] * lp[3:4], axis=-1, keepdims=True)) + lam_init)
    snw = snw_ref[...]

    def q_maps(rows):
        q = q_ref[rows, :]
        zero = jnp.zeros_like(q)
        return (jnp.where(lane < MAP_DIM, q, zero), jnp.where(lane < MAP_DIM, zero, q))

    def scores(qm, rows):
        ks = k_ref[rows, :]
        return [lax.dot_general(x, ks, NT_DIMS, preferred_element_type=F32) for x in qm]

    def finish(rows, acc0, acc1, l0, l1):
        o = acc0 / l0 - lam * (acc1 / l1)
        o_ref[rows, :] = (_rms_hat(o) * snw * (1.0 - lam_init)).astype(o_ref.dtype)

    bound = SCORE_BOUND_COEF * jnp.max(jnp.abs(qg_ref[...])) * jnp.max(jnp.abs(kg_ref[...]))
    bounded = bound <= SCORE_BOUND_LIMIT

    @pl.when(bounded)
    def _():
        def lane_sums(p):
            out = p[:, 0:LANES]
            for b in range(1, p.shape[1] // LANES):
                out = out + p[:, b * LANES:(b + 1) * LANES]
            return out

        def plus(a, b):
            return b if a is None else a + b

        for qi in range(n_q):
            q0 = qi * tq
            qm = q_maps(slice(q0, q0 + tq))
            acc = [None, None]
            lsum = [None, None]
            s_next = scores(qm, slice(0, tq)) if qi > 0 else None
            for j in range(qi):
                s = s_next
                s_next = scores(qm, slice((j + 1) * tq, (j + 2) * tq)) if j + 1 < qi else None
                vs = v_ref[j * tq:(j + 1) * tq, :]
                for mi in range(2):
                    p = jnp.exp2(s[mi])
                    lsum[mi] = plus(lsum[mi], lane_sums(p))
                    acc[mi] = plus(acc[mi], _dot(p.astype(BF16), vs))
            for bi, ncols in enumerate((half, tq)):
                rows = slice(bi * half, (bi + 1) * half)
                sd = scores([x[rows, :] for x in qm], slice(q0, q0 + ncols))
                r = lax.broadcasted_iota(jnp.int32, (half, ncols), 0) + bi * half
                c = lax.broadcasted_iota(jnp.int32, (half, ncols), 1)
                vs = v_ref[q0:q0 + ncols, :]
                fin = []
                for mi in range(2):
                    p = jnp.where(r >= c, jnp.exp2(sd[mi]), 0.0)
                    below_l = None if lsum[mi] is None else lsum[mi][rows, :]
                    below_a = None if acc[mi] is None else acc[mi][rows, :]
                    fin.append((plus(below_a, _dot(p.astype(BF16), vs)),
                                jnp.sum(plus(below_l, lane_sums(p)), axis=-1, keepdims=True)))
                finish(slice(q0 + bi * half, q0 + (bi + 1) * half),
                       fin[0][0], fin[1][0], fin[0][1], fin[1][1])

    @pl.when(jnp.logical_not(bounded))
    def _():
        r = lax.broadcasted_iota(jnp.int32, (tq, tq), 0)
        c = lax.broadcasted_iota(jnp.int32, (tq, tq), 1)

        def q_tile(qi, carry):
            q_rows = pl.ds(pl.multiple_of(qi * tq, tq), tq)
            qm = q_maps(q_rows)
            m_ref[...] = jnp.full(m_ref.shape, NEG_INF, F32)
            l_ref[...] = jnp.zeros(l_ref.shape, F32)
            acc_ref[...] = jnp.zeros(acc_ref.shape, F32)

            def step(j, masked):
                kv_rows = pl.ds(pl.multiple_of(j * tq, tq), tq)
                s = scores(qm, kv_rows)
                vs = v_ref[kv_rows, :]
                for mi in range(2):
                    sm = jnp.where(r >= c, s[mi], NEG_INF) if masked else s[mi]
                    m_old = m_ref[mi]
                    m_new = jnp.maximum(m_old, jnp.max(sm, axis=-1, keepdims=True))
                    alpha = jnp.exp2(m_old - m_new)
                    p = jnp.exp2(sm - m_new[:, 0:1])
                    l_ref[mi] = alpha * l_ref[mi] + jnp.sum(p, axis=-1, keepdims=True)
                    acc_ref[mi] = alpha * acc_ref[mi] + _dot(p.astype(BF16), vs)
                    m_ref[mi] = m_new

            def full_body(j, carry):
                step(j, False)
                return carry

            lax.fori_loop(0, qi, full_body, 0)
            step(qi, True)
            finish(q_rows, acc_ref[0], acc_ref[1], l_ref[0], l_ref[1])
            return carry

        lax.fori_loop(0, n_q, q_tile, 0)


def _diff_attn(q, k, v, q_gain, k_gain, lam_params, sub_norm, batch, seq_len, lam_init):
    t, width = q.shape
    tq = ATTN_TILE
    seq_spec = pl.BlockSpec((seq_len, HEAD_DIM), lambda b, h: (b, h))
    return pl.pallas_call(
        functools.partial(_diff_attn_kernel, tq=tq, lam_init=lam_init),
        grid=(batch, HEADS),
        in_specs=[seq_spec, seq_spec, seq_spec,
                  _const_spec(q_gain.shape), _const_spec(k_gain.shape),
                  _const_spec(lam_params.shape), _const_spec((1, HEAD_DIM))],
        out_specs=seq_spec,
        out_shape=jax.ShapeDtypeStruct((t, width), BF16),
        scratch_shapes=[pltpu.VMEM((2, tq, LANES), F32), pltpu.VMEM((2, tq, LANES), F32),
                        pltpu.VMEM((2, tq, HEAD_DIM), F32)],
        compiler_params=pltpu.CompilerParams(dimension_semantics=("parallel", "parallel"),
                                             vmem_limit_bytes=VMEM_LIMIT),
        name="diff_attn",
    )(q, k, v, q_gain, k_gain, lam_params, sub_norm)


def kernel(x, positions, a_norm, a_w_in, a_conv_w, a_a_log, a_dt_bias, a_out_norm, a_w_out,
           kv_norm, w_kv, k_norm, b_norm, b_w_q, b_q_norm, b_lambda, b_sub_norm, b_w_out,
           mlp_norm, mlp_w1, mlp_w2):
    batch, seq_len, d = x.shape
    assert d == D_MODEL and a_norm.shape[0] == 1 and b_norm.shape[0] == 1
    assert seq_len % ROW_TILE == 0 and seq_len % ATTN_TILE == 0 and seq_len % DELTA_CHUNK == 0
    t = batch * seq_len
    width = HEADS * HEAD_DIM
    xf = x.reshape(t, d)

    w_in = a_w_in[0]
    w_main = w_in[:, :4 * width].astype(BF16)
    wg_t = w_in[:, 4 * width:].T.astype(BF16)
    pad = jnp.zeros((HEADS, 1), F32)
    alog16 = jnp.concatenate([pad, a_a_log[0].reshape(HEADS, 1)], axis=0)
    dtb16 = jnp.concatenate([pad, a_dt_bias[0].reshape(HEADS, 1)], axis=0)
    q, k, v, z, gates = _gdn_in(xf, a_norm[0].reshape(1, d), w_main, wg_t, a_conv_w[0],
                                alog16, dtb16, seq_len)
    o = _delta(q, k, v, z, gates, a_out_norm[0].reshape(1, HEAD_DIM), batch, seq_len)
    w1_all = mlp_w1.astype(BF16)
    w2_all = mlp_w2.astype(BF16)
    xf = _proj_mlp(o, xf, a_w_out[0].astype(BF16), mlp_norm[0].reshape(1, d), w1_all, w2_all, 0)

    half = ROPE_HALF
    freqs = ROPE_THETA ** (-jnp.arange(half, dtype=F32) / half)
    freq_row = jnp.tile(freqs, LANES // half).reshape(1, LANES)
    cos, s1, s2 = _rope_tab(positions.reshape(t, 1), freq_row)
    k_gain = jnp.tile(k_norm, width // MAP_DIM).reshape(1, width)
    q_gain = jnp.tile(b_q_norm[0], width // MAP_DIM).reshape(1, width)
    kr, vv, qr = _attn_in(xf, kv_norm.reshape(1, d), b_norm[0].reshape(1, d),
                          w_kv.astype(BF16), b_w_q[0].astype(BF16), k_gain, q_gain, cos, s1, s2)
    lam_init = 0.8 - 0.6 * math.exp(-0.3 * 1)
    oa = _diff_attn(qr, kr, vv, b_q_norm[0].reshape(1, MAP_DIM), k_norm.reshape(1, MAP_DIM),
                    b_lambda[0], b_sub_norm[0].reshape(1, HEAD_DIM), batch, seq_len, lam_init)
    xf = _proj_mlp(oa, xf, b_w_out[0].astype(BF16), mlp_norm[1].reshape(1, d), w1_all, w2_all, 1)
    return xf.reshape(batch, seq_len, d)
```

```python
import functools
import math

import jax
import jax.numpy as jnp
from jax import lax
from jax.experimental import pallas as pl
from jax.experimental.pallas import tpu as pltpu

F32 = jnp.float32
BF16 = jnp.bfloat16

D_MODEL = 1024
HEADS = 8
HEAD_DIM = 128
MAP_DIM = 64
ROPE_HALF = MAP_DIM // 2
CONV_WIDTH = 4
D_FF = 4 * D_MODEL
ROPE_THETA = 10000.0
EPS = 1e-6
NEG_INF = -1e30
LOG2E = math.log2(math.e)
SCORE_BOUND_COEF = MAP_DIM * MAP_DIM ** -0.5 * LOG2E * 1.02
SCORE_BOUND_LIMIT = 100.0

LANES = 128
SUBLANES = 8
MXU_DIM = 256

ROW_TILE = 512
GDN_ROW_TILE = 512
GDN_COL_BLOCK = 2 * MXU_DIM
FF_CHUNK = 1024
DELTA_CHUNK = 64
DELTA_GROUP = 8
DELTA_HEADS = 2
ATTN_TILE = 512
VMEM_LIMIT = 56 * 1024 * 1024

NT_DIMS = (((1,), (1,)), ((), ()))
TN_DIMS = (((0,), (0,)), ((), ()))


def _rms_hat(x):
    return x * lax.rsqrt(jnp.mean(x * x, axis=-1, keepdims=True) + EPS)


def _sigmoid(x):
    return 1.0 / (1.0 + jnp.exp(-x))


def _silu(x):
    h = 0.5 * x
    return h * jnp.tanh(h) + h


def _softplus(x):
    return jnp.maximum(x, 0.0) + jnp.log(1.0 + jnp.exp(-jnp.abs(x)))


def _dot(a, b):
    return jnp.dot(a, b, preferred_element_type=F32)


def _const_spec(shape):
    zeros = (0,) * len(shape)
    return pl.BlockSpec(shape, lambda *_: zeros, pipeline_mode=pl.Buffered(1))


def _gdn_in_kernel(x_ref, nw_ref, w_ref, wgt_ref, cw_ref, alog_ref, dtb_ref,
                   q_ref, k_ref, v_ref, z_ref, gate_ref, xn_ref, tail_ref, *, tiles_per_seq):
    tm = x_ref.shape[0]
    width = HEADS * HEAD_DIM
    step = pl.program_id(0)

    @pl.when(step == 0)
    def _():
        tail_ref[...] = jnp.zeros(tail_ref.shape, F32)

    xn_ref[...] = (_rms_hat(x_ref[...]) * nw_ref[...]).astype(BF16)
    seq_start = (step % tiles_per_seq) == 0

    cb = GDN_COL_BLOCK
    blocks = 3 * width // cb

    def project(blk):
        return _dot(xn_ref[...], w_ref[:, blk * cb:(blk + 1) * cb])

    p = project(0)
    for blk in range(blocks):
        p_next = project(blk + 1) if blk + 1 < blocks else None
        cols = slice(blk * cb, (blk + 1) * cb)
        prev = jnp.where(seq_start, 0.0, tail_ref[:, cols])
        tail_ref[:, cols] = p[tm - SUBLANES:, :]
        xp = jnp.concatenate([prev, p], axis=0)
        cw = cw_ref[:, cols]
        c = cw[CONV_WIDTH - 1:CONV_WIDTH, :] * p
        for j in range(CONV_WIDTH - 1):
            c = c + cw[j:j + 1, :] * pltpu.roll(xp, CONV_WIDTH - 1 - j, 0)[SUBLANES:, :]
        kind, col = divmod(blk * cb, width)
        (q_ref, k_ref, v_ref)[kind][:, col:col + cb] = c
        p = p_next

    z_ref[...] = _dot(xn_ref[...], w_ref[:, 3 * width:4 * width])

    gt = lax.dot_general(wgt_ref[...], xn_ref[...], NT_DIMS, preferred_element_type=F32)
    beta = _sigmoid(gt)
    decay = -jnp.exp(alog_ref[...]) * _softplus(gt + dtb_ref[...])
    row = lax.broadcasted_iota(jnp.int32, gt.shape, 0)
    gate_ref[...] = jnp.where(row < HEADS, beta, decay)


def _gdn_in(x, norm_w, w_main, wg_t, conv_w, alog16, dtb16, seq_len):
    t, d = x.shape
    tm = GDN_ROW_TILE
    width = HEADS * HEAD_DIM
    act = jax.ShapeDtypeStruct((t, width), F32)
    row_spec = pl.BlockSpec((tm, width), lambda i: (i, 0))
    return pl.pallas_call(
        functools.partial(_gdn_in_kernel, tiles_per_seq=seq_len // tm),
        grid=(t // tm,),
        in_specs=[
            pl.BlockSpec((tm, d), lambda i: (i, 0)),
            _const_spec((1, d)),
            _const_spec(w_main.shape),
            _const_spec(wg_t.shape),
            _const_spec(conv_w.shape),
            _const_spec(alog16.shape),
            _const_spec(dtb16.shape),
        ],
        out_specs=[row_spec, row_spec, row_spec, row_spec,
                   pl.BlockSpec((2 * HEADS, tm), lambda i: (0, i))],
        out_shape=[act, act, act, act, jax.ShapeDtypeStruct((2 * HEADS, t), F32)],
        scratch_shapes=[pltpu.VMEM((tm, d), BF16), pltpu.VMEM((SUBLANES, 3 * width), F32)],
        compiler_params=pltpu.CompilerParams(dimension_semantics=("arbitrary",),
                                             vmem_limit_bytes=VMEM_LIMIT),
        name="gdn_in",
    )(x, norm_w, w_main, wg_t, conv_w, alog16, dtb16)


def _delta_kernel(q_ref, k_ref, v_ref, z_ref, beta_ref, g_ref, onw_ref, o_ref, gc_ref,
                  qw_a, b_a, op_a, cd_a, qw_b, b_b, op_b, cd_b, *, chunk, group, heads):
    c_len = chunk
    n_groups = q_ref.shape[0] // (c_len * group)
    bufs = ((qw_a, b_a, op_a, cd_a), (qw_b, b_b, op_b, cd_b))
    row = lax.broadcasted_iota(jnp.int32, (c_len, c_len), 0)
    col = lax.broadcasted_iota(jnp.int32, (c_len, c_len), 1)
    causal = row >= col
    strict = row > col
    eye = row == col
    upper = (row <= col).astype(F32)
    for hh in range(heads):
        gc_ref[hh] = jnp.dot(g_ref[hh], upper, precision=lax.Precision.HIGHEST,
                             preferred_element_type=F32)
    onw = onw_ref[...]
    n_steps = int(math.log2(c_len))
    chains = [(g, hh) for g in range(group) for hh in range(heads)]

    def to_col(r):
        return jnp.sum(jnp.where(eye, r, 0.0), axis=1, keepdims=True)

    def chunk_rows(c):
        return pl.ds(pl.multiple_of(c * c_len, c_len), c_len)

    def head_cols(hh):
        return slice(hh * HEAD_DIM, (hh + 1) * HEAD_DIM)

    def unit_rows(a, scale):
        return a * (lax.rsqrt(jnp.sum(a * a, axis=-1, keepdims=True) + EPS) * scale)

    def run(prep, adv, st):
        todo = list(range(group)) if adv is not None else []

        def advance_one(st):
            if not todo:
                return st
            g = todo.pop(0)
            grp, (qw_ref, b_ref, op_ref, cd_ref) = adv
            rows = chunk_rows(grp * group + g)
            slots = [g * heads + hh for hh in range(heads)]
            res = [_dot(qw_ref[slots[hh]], st[hh].astype(BF16)) for hh in range(heads)]
            out = [res[hh][:c_len] + op_ref[slots[hh]] for hh in range(heads)]
            st = [cd_ref[slots[hh], 0:1, :] * st[hh] - res[hh][c_len:] + b_ref[slots[hh]]
                  for hh in range(heads)]
            for hh in range(heads):
                zc = z_ref[rows, head_cols(hh)]
                o_ref[rows, head_cols(hh)] = (_rms_hat(out[hh]) * onw * _silu(zc)).astype(o_ref.dtype)
            return st

        if prep is not None:
            grp, (qw_ref, b_ref, op_ref, cd_ref) = prep
            n = len(chains)
            cidx = [grp * group + g for g, _ in chains]
            q = [unit_rows(_silu(q_ref[chunk_rows(cidx[i]), head_cols(chains[i][1])]),
                           HEAD_DIM ** -0.5) for i in range(n)]
            k = [unit_rows(_silu(k_ref[chunk_rows(cidx[i]), head_cols(chains[i][1])]), 1.0)
                 for i in range(n)]
            gc_r = [gc_ref[chains[i][1], pl.ds(cidx[i], 1), :] for i in range(n)]
            beta_c = [to_col(beta_ref[chains[i][1], pl.ds(cidx[i], 1), :]) for i in range(n)]
            gc_c = [to_col(r) for r in gc_r]
            gc_last = [r[:, c_len - 1:c_len] for r in gc_r]
            e_c = [jnp.exp(x) for x in gc_c]
            kb = [k[i] * beta_c[i] for i in range(n)]
            s = [lax.dot_general(jnp.concatenate([kb[i], q[i]], axis=0).astype(BF16),
                                 k[i].astype(BF16), NT_DIMS, preferred_element_type=F32)
                 for i in range(n)]
            st = advance_one(st)
            decay = [jnp.where(causal, jnp.exp(jnp.where(causal, gc_c[i] - gc_r[i], 0.0)), 0.0)
                     for i in range(n)]
            p = [jnp.where(strict, -(s[i][:c_len] * decay[i]), 0.0) for i in range(n)]
            intra = [s[i][c_len:] * decay[i] for i in range(n)]
            y = [jnp.concatenate([_silu(v_ref[chunk_rows(cidx[i]), head_cols(chains[i][1])]) * beta_c[i],
                                  kb[i] * e_c[i]], axis=1) for i in range(n)]
            for step in range(n_steps):
                p16 = [x.astype(BF16) for x in p]
                y = [y[i] + _dot(p16[i], y[i].astype(BF16)) for i in range(n)]
                if step + 1 < n_steps:
                    p = [_dot(x, x) for x in p16]
                st = advance_one(st)
            y16 = [x.astype(BF16) for x in y]
            kd16 = [(k[i] * jnp.exp(gc_last[i] - gc_c[i])).astype(BF16) for i in range(n)]
            mb = [lax.dot_general(kd16[i], y16[i], TN_DIMS, preferred_element_type=F32)
                  for i in range(n)]
            iu = [_dot(intra[i].astype(BF16), y16[i]) for i in range(n)]
            st = advance_one(st)
            for i in range(n):
                b_ref[i] = mb[i][:, :HEAD_DIM]
                qw_ref[i, 0:c_len, :] = (q[i] * e_c[i] - iu[i][:, HEAD_DIM:]).astype(BF16)
                qw_ref[i, c_len:, :] = mb[i][:, HEAD_DIM:].astype(BF16)
                op_ref[i] = iu[i][:, :HEAD_DIM]
                cd_ref[i] = jnp.broadcast_to(jnp.exp(gc_last[i]), (SUBLANES, LANES))
        while todo:
            st = advance_one(st)
        return st

    st = run((0, bufs[0]), None, [jnp.zeros((HEAD_DIM, HEAD_DIM), F32) for _ in range(heads)])

    def body(i, st):
        st = run((2 * i + 1, bufs[1]), (2 * i, bufs[0]), st)
        return run((2 * i + 2, bufs[0]), (2 * i + 1, bufs[1]), st)

    st = lax.fori_loop(0, n_groups // 2 - 1, body, st)
    st = run((n_groups - 1, bufs[1]), (n_groups - 2, bufs[0]), st)
    run(None, (n_groups - 1, bufs[1]), st)


def _delta(q, k, v, z, gates, out_norm, batch, seq_len):
    t, width = q.shape
    c_len = DELTA_CHUNK
    nh = DELTA_HEADS
    n_chunks = seq_len // c_len
    assert n_chunks % (2 * DELTA_GROUP) == 0 and HEADS % nh == 0
    gates3 = gates.reshape(2 * HEADS, batch * n_chunks, c_len)
    seq_spec = pl.BlockSpec((seq_len, nh * HEAD_DIM), lambda b, h: (b, h))
    slots = DELTA_GROUP * nh
    buf_set = [pltpu.VMEM((slots, c_len + HEAD_DIM, HEAD_DIM), BF16),
               pltpu.VMEM((slots, HEAD_DIM, HEAD_DIM), F32),
               pltpu.VMEM((slots, c_len, HEAD_DIM), F32),
               pltpu.VMEM((slots, SUBLANES, LANES), F32)]
    return pl.pallas_call(
        functools.partial(_delta_kernel, chunk=c_len, group=DELTA_GROUP, heads=nh),
        grid=(batch, HEADS // nh),
        in_specs=[seq_spec, seq_spec, seq_spec, seq_spec,
                  pl.BlockSpec((nh, n_chunks, c_len), lambda b, h: (h, b, 0)),
                  pl.BlockSpec((nh, n_chunks, c_len), lambda b, h: (HEADS // nh + h, b, 0)),
                  pl.BlockSpec((1, HEAD_DIM), lambda b, h: (0, 0))],
        out_specs=seq_spec,
        out_shape=jax.ShapeDtypeStruct((t, width), BF16),
        scratch_shapes=[pltpu.VMEM((nh, n_chunks, c_len), F32)] + buf_set + buf_set,
        compiler_params=pltpu.CompilerParams(dimension_semantics=("parallel", "parallel"),
                                             vmem_limit_bytes=VMEM_LIMIT),
        name="delta",
    )(q, k, v, z, gates3, gates3, out_norm)


def _proj_mlp_kernel(o_ref, x_ref, wo_ref, nw_ref, w1_ref, w2_ref, out_ref):
    x1 = x_ref[...] + _dot(o_ref[...].astype(BF16), wo_ref[...])
    xn = (_rms_hat(x1) * nw_ref[...]).astype(BF16)
    acc = x1
    for j in range(D_FF // FF_CHUNK):
        h = jnp.maximum(_dot(xn, w1_ref[:, j * FF_CHUNK:(j + 1) * FF_CHUNK]), 0.0)
        acc = acc + _dot((h * h).astype(BF16), w2_ref[j * FF_CHUNK:(j + 1) * FF_CHUNK, :])
    out_ref[...] = acc


def _layer_spec(stacked, layer):
    return pl.BlockSpec((None,) + stacked.shape[1:], lambda *_: (layer, 0, 0),
                        pipeline_mode=pl.Buffered(1))


def _proj_mlp(o, x, w_out, norm_w, w1, w2, layer):
    t, d = x.shape
    tm = ROW_TILE
    return pl.pallas_call(
        _proj_mlp_kernel,
        grid=(t // tm,),
        in_specs=[pl.BlockSpec((tm, o.shape[1]), lambda i: (i, 0)),
                  pl.BlockSpec((tm, d), lambda i: (i, 0)),
                  _const_spec(w_out.shape), _const_spec((1, d)),
                  _layer_spec(w1, layer), _layer_spec(w2, layer)],
        out_specs=pl.BlockSpec((tm, d), lambda i: (i, 0)),
        out_shape=jax.ShapeDtypeStruct((t, d), F32),
        compiler_params=pltpu.CompilerParams(dimension_semantics=("parallel",),
                                             vmem_limit_bytes=VMEM_LIMIT),
        name="proj_mlp",
    )(o, x, w_out, norm_w, w1, w2)


def _rope_tab_kernel(pos_ref, freq_ref, cos_ref, s1_ref, s2_ref):
    ang = pos_ref[...].astype(F32) * freq_ref[...]
    sin = jnp.sin(ang)
    lane = lax.broadcasted_iota(jnp.int32, ang.shape, 1)
    first_half = (lane % MAP_DIM) < ROPE_HALF
    cos_ref[...] = jnp.cos(ang)
    s1_ref[...] = jnp.where(first_half, -sin, 0.0)
    s2_ref[...] = jnp.where(first_half, 0.0, sin)


def _rope_tab(pos_col, freq_row):
    t = pos_col.shape[0]
    tm = 2048
    tab = jax.ShapeDtypeStruct((t, LANES), F32)
    spec = pl.BlockSpec((tm, LANES), lambda i: (i, 0))
    return pl.pallas_call(
        _rope_tab_kernel,
        grid=(t // tm,),
        in_specs=[pl.BlockSpec((tm, 1), lambda i: (i, 0)), _const_spec((1, LANES))],
        out_specs=[spec, spec, spec],
        out_shape=[tab, tab, tab],
        compiler_params=pltpu.CompilerParams(dimension_semantics=("parallel",)),
        name="rope_tab",
    )(pos_col, freq_row)


def _attn_in_kernel(x_ref, kvn_ref, qnw_ref, wkv_ref, wq_ref, kg_ref, qg_ref,
                    cos_ref, s1_ref, s2_ref, k_ref, v_ref, q_ref, kvx_ref, qx_ref):
    width = HEADS * HEAD_DIM
    cb = 2 * MXU_DIM
    xhat = _rms_hat(x_ref[...])
    kvx_ref[...] = (xhat * kvn_ref[...]).astype(BF16)
    qx_ref[...] = (xhat * qnw_ref[...]).astype(BF16)
    cos = cos_ref[...]
    s1 = s1_ref[...]
    s2 = s2_ref[...]
    r = lax.broadcasted_iota(jnp.int32, (MXU_DIM, MXU_DIM), 0) // MAP_DIM
    c = lax.broadcasted_iota(jnp.int32, (MXU_DIM, MXU_DIM), 1) // MAP_DIM
    group_ones = (r == c).astype(BF16)

    def norm_rope(raw, gain, scale, o_ref, col):
        for s in range(raw.shape[1] // MXU_DIM):
            blk = raw[:, s * MXU_DIM:(s + 1) * MXU_DIM]
            ss = _dot((blk * blk).astype(BF16), group_ones)
            lo = col + s * MXU_DIM
            nb = blk * lax.rsqrt(ss * (1.0 / MAP_DIM) + EPS) * gain[:, lo:lo + MXU_DIM]
            for hh in range(MXU_DIM // LANES):
                xb = nb[:, hh * LANES:(hh + 1) * LANES]
                rot = xb * cos + pltpu.roll(xb, LANES - ROPE_HALF, 1) * s1 + pltpu.roll(xb, ROPE_HALF, 1) * s2
                o_ref[:, lo + hh * LANES:lo + (hh + 1) * LANES] = (rot * scale).astype(o_ref.dtype)

    plan = []
    for col in range(0, width, cb):
        plan += [("k", col), ("q", col)]
    plan += [("v", col) for col in range(0, width, cb)]

    def project(item):
        kind, col = item
        if kind == "q":
            return _dot(qx_ref[...], wq_ref[:, col:col + cb])
        base = 0 if kind == "k" else width
        return _dot(kvx_ref[...], wkv_ref[:, base + col:base + col + cb])

    raw = project(plan[0])
    for i, (kind, col) in enumerate(plan):
        raw_next = project(plan[i + 1]) if i + 1 < len(plan) else None
        if kind == "v":
            v_ref[:, col:col + cb] = raw.astype(v_ref.dtype)
        elif kind == "k":
            norm_rope(raw, kg_ref[...], 1.0, k_ref, col)
        else:
            norm_rope(raw, qg_ref[...], MAP_DIM ** -0.5 * LOG2E, q_ref, col)
        raw = raw_next


def _attn_in(x, kv_norm, q_norm_w, w_kv, w_q, k_gain, q_gain, cos, s1, s2):
    t, d = x.shape
    tm = ROW_TILE
    width = HEADS * HEAD_DIM
    act = jax.ShapeDtypeStruct((t, width), BF16)
    row_spec = pl.BlockSpec((tm, width), lambda i: (i, 0))
    tab_spec = pl.BlockSpec((tm, LANES), lambda i: (i, 0))
    return pl.pallas_call(
        _attn_in_kernel,
        grid=(t // tm,),
        in_specs=[pl.BlockSpec((tm, d), lambda i: (i, 0)),
                  _const_spec((1, d)), _const_spec((1, d)),
                  _const_spec(w_kv.shape), _const_spec(w_q.shape),
                  _const_spec((1, width)), _const_spec((1, width)),
                  tab_spec, tab_spec, tab_spec],
        out_specs=[row_spec, row_spec, row_spec],
        out_shape=[act, act, act],
        scratch_shapes=[pltpu.VMEM((tm, d), BF16), pltpu.VMEM((tm, d), BF16)],
        compiler_params=pltpu.CompilerParams(dimension_semantics=("parallel",),
                                             vmem_limit_bytes=VMEM_LIMIT),
        name="attn_in",
    )(x, kv_norm, q_norm_w, w_kv, w_q, k_gain, q_gain, cos, s1, s2)


def _diff_attn_kernel(q_ref, k_ref, v_ref, qg_ref, kg_ref, lam_ref, snw_ref, o_ref,
                      m_ref, l_ref, acc_ref, *, tq, lam_init):
    n_q = q_ref.shape[0] // tq
    half = tq // 2
    lane = lax.broadcasted_iota(jnp.int32, (tq, HEAD_DIM), 1)
    lp = lam_ref[...]
    lam = (jnp.exp(jnp.sum(lp[0:1] * lp[1:2], axis=-1, keepdims=True))
           - jnp.exp(jnp.sum(lp[2:3] * lp[3:4], axis=-1, keepdims=True)) + lam_init)
    snw = snw_ref[...]

    def q_maps(rows):
        q = q_ref[rows, :]
        zero = jnp.zeros_like(q)
        return (jnp.where(lane < MAP_DIM, q, zero), jnp.where(lane < MAP_DIM, zero, q))

    def scores(qm, rows):
        ks = k_ref[rows, :]
        return [lax.dot_general(x, ks, NT_DIMS, preferred_element_type=F32) for x in qm]

    def finish(rows, acc0, acc1, l0, l1):
        o = acc0 / l0 - lam * (acc1 / l1)
        o_ref[rows, :] = (_rms_hat(o) * snw * (1.0 - lam_init)).astype(o_ref.dtype)

    bound = SCORE_BOUND_COEF * jnp.max(jnp.abs(qg_ref[...])) * jnp.max(jnp.abs(kg_ref[...]))
    bounded = bound <= SCORE_BOUND_LIMIT

    @pl.when(bounded)
    def _():
        def lane_sums(p):
            out = p[:, 0:LANES]
            for b in range(1, p.shape[1] // LANES):
                out = out + p[:, b * LANES:(b + 1) * LANES]
            return out

        def plus(a, b):
            return b if a is None else a + b

        for qi in range(n_q):
            q0 = qi * tq
            qm = q_maps(slice(q0, q0 + tq))
            acc = [None, None]
            lsum = [None, None]
            s_next = scores(qm, slice(0, tq)) if qi > 0 else None
            for j in range(qi):
                s = s_next
                s_next = scores(qm, slice((j + 1) * tq, (j + 2) * tq)) if j + 1 < qi else None
                vs = v_ref[j * tq:(j + 1) * tq, :]
                for mi in range(2):
                    p = jnp.exp2(s[mi])
                    lsum[mi] = plus(lsum[mi], lane_sums(p))
                    acc[mi] = plus(acc[mi], _dot(p.astype(BF16), vs))
            for bi, ncols in enumerate((half, tq)):
                rows = slice(bi * half, (bi + 1) * half)
                sd = scores([x[rows, :] for x in qm], slice(q0, q0 + ncols))
                r = lax.broadcasted_iota(jnp.int32, (half, ncols), 0) + bi * half
                c = lax.broadcasted_iota(jnp.int32, (half, ncols), 1)
                vs = v_ref[q0:q0 + ncols, :]
                fin = []
                for mi in range(2):
                    p = jnp.where(r >= c, jnp.exp2(sd[mi]), 0.0)
                    below_l = None if lsum[mi] is None else lsum[mi][rows, :]
                    below_a = None if acc[mi] is None else acc[mi][rows, :]
                    fin.append((plus(below_a, _dot(p.astype(BF16), vs)),
                                jnp.sum(plus(below_l, lane_sums(p)), axis=-1, keepdims=True)))
                finish(slice(q0 + bi * half, q0 + (bi + 1) * half),
                       fin[0][0], fin[1][0], fin[0][1], fin[1][1])

    @pl.when(jnp.logical_not(bounded))
    def _():
        r = lax.broadcasted_iota(jnp.int32, (tq, tq), 0)
        c = lax.broadcasted_iota(jnp.int32, (tq, tq), 1)

        def q_tile(qi, carry):
            q_rows = pl.ds(pl.multiple_of(qi * tq, tq), tq)
            qm = q_maps(q_rows)
            m_ref[...] = jnp.full(m_ref.shape, NEG_INF, F32)
            l_ref[...] = jnp.zeros(l_ref.shape, F32)
            acc_ref[...] = jnp.zeros(acc_ref.shape, F32)

            def step(j, masked):
                kv_rows = pl.ds(pl.multiple_of(j * tq, tq), tq)
                s = scores(qm, kv_rows)
                vs = v_ref[kv_rows, :]
                for mi in range(2):
                    sm = jnp.where(r >= c, s[mi], NEG_INF) if masked else s[mi]
                    m_old = m_ref[mi]
                    m_new = jnp.maximum(m_old, jnp.max(sm, axis=-1, keepdims=True))
                    alpha = jnp.exp2(m_old - m_new)
                    p = jnp.exp2(sm - m_new[:, 0:1])
                    l_ref[mi] = alpha * l_ref[mi] + jnp.sum(p, axis=-1, keepdims=True)
                    acc_ref[mi] = alpha * acc_ref[mi] + _dot(p.astype(BF16), vs)
                    m_ref[mi] = m_new

            def full_body(j, carry):
                step(j, False)
                return carry

            lax.fori_loop(0, qi, full_body, 0)
            step(qi, True)
            finish(q_rows, acc_ref[0], acc_ref[1], l_ref[0], l_ref[1])
            return carry

        lax.fori_loop(0, n_q, q_tile, 0)


def _diff_attn(q, k, v, q_gain, k_gain, lam_params, sub_norm, batch, seq_len, lam_init):
    t, width = q.shape
    tq = ATTN_TILE
    seq_spec = pl.BlockSpec((seq_len, HEAD_DIM), lambda b, h: (b, h))
    return pl.pallas_call(
        functools.partial(_diff_attn_kernel, tq=tq, lam_init=lam_init),
        grid=(batch, HEADS),
        in_specs=[seq_spec, seq_spec, seq_spec,
                  _const_spec(q_gain.shape), _const_spec(k_gain.shape),
                  _const_spec(lam_params.shape), _const_spec((1, HEAD_DIM))],
        out_specs=seq_spec,
        out_shape=jax.ShapeDtypeStruct((t, width), BF16),
        scratch_shapes=[pltpu.VMEM((2, tq, LANES), F32), pltpu.VMEM((2, tq, LANES), F32),
                        pltpu.VMEM((2, tq, HEAD_DIM), F32)],
        compiler_params=pltpu.CompilerParams(dimension_semantics=("parallel", "parallel"),
                                             vmem_limit_bytes=VMEM_LIMIT),
        name="diff_attn",
    )(q, k, v, q_gain, k_gain, lam_params, sub_norm)


def kernel(x, positions, a_norm, a_w_in, a_conv_w, a_a_log, a_dt_bias, a_out_norm, a_w_out,
           kv_norm, w_kv, k_norm, b_norm, b_w_q, b_q_norm, b_lambda, b_sub_norm, b_w_out,
           mlp_norm, mlp_w1, mlp_w2):
    batch, seq_len, d = x.shape
    assert d == D_MODEL and a_norm.shape[0] == 1 and b_norm.shape[0] == 1
    assert seq_len % ROW_TILE == 0 and seq_len % ATTN_TILE == 0 and seq_len % DELTA_CHUNK == 0
    t = batch * seq_len
    width = HEADS * HEAD_DIM
    xf = x.reshape(t, d)

    w_in = a_w_in[0]
    w_main = w_in.astype(BF16)
    wg_t = w_in[:, 4 * width:].T.astype(BF16)
    pad = jnp.zeros((HEADS, 1), F32)
    alog16 = jnp.concatenate([pad, a_a_log[0].reshape(HEADS, 1)], axis=0)
    dtb16 = jnp.concatenate([pad, a_dt_bias[0].reshape(HEADS, 1)], axis=0)
    q, k, v, z, gates = _gdn_in(xf, a_norm[0].reshape(1, d), w_main, wg_t, a_conv_w[0],
                                alog16, dtb16, seq_len)
    o = _delta(q, k, v, z, gates, a_out_norm[0].reshape(1, HEAD_DIM), batch, seq_len)
    w1_all = mlp_w1.astype(BF16)
    w2_all = mlp_w2.astype(BF16)
    xf = _proj_mlp(o, xf, a_w_out[0].astype(BF16), mlp_norm[0].reshape(1, d), w1_all, w2_all, 0)

    half = ROPE_HALF
    freqs = ROPE_THETA ** (-jnp.arange(half, dtype=F32) / half)
    freq_row = jnp.tile(freqs, LANES // half).reshape(1, LANES)
    cos, s1, s2 = _rope_tab(positions.reshape(t, 1), freq_row)
    k_gain = jnp.tile(k_norm, width // MAP_DIM).reshape(1, width)
    q_gain = jnp.tile(b_q_norm[0], width // MAP_DIM).reshape(1, width)
    kr, vv, qr = _attn_in(xf, kv_norm.reshape(1, d), b_norm[0].reshape(1, d),
                          w_kv.astype(BF16), b_w_q[0].astype(BF16), k_gain, q_gain, cos, s1, s2)
    lam_init = 0.8 - 0.6 * math.exp(-0.3 * 1)
    oa = _diff_attn(qr, kr, vv, b_q_norm[0].reshape(1, MAP_DIM), k_norm.reshape(1, MAP_DIM),
                    b_lambda[0], b_sub_norm[0].reshape(1, HEAD_DIM), batch, seq_len, lam_init)
    xf = _proj_mlp(oa, xf, b_w_out[0].astype(BF16), mlp_norm[1].reshape(1, d), w1_all, w2_all, 1)
    return xf.reshape(batch, seq_len, d)
```

```python
import functools
import math

import jax
import jax.numpy as jnp
from jax import lax
from jax.experimental import pallas as pl
from jax.experimental.pallas import tpu as pltpu

F32 = jnp.float32
BF16 = jnp.bfloat16

D_MODEL = 1024
HEADS = 8
HEAD_DIM = 128
MAP_DIM = 64
ROPE_HALF = MAP_DIM // 2
CONV_WIDTH = 4
D_FF = 4 * D_MODEL
ROPE_THETA = 10000.0
EPS = 1e-6
NEG_INF = -1e30
LOG2E = math.log2(math.e)
SCORE_BOUND_COEF = MAP_DIM * MAP_DIM ** -0.5 * LOG2E * 1.02
SCORE_BOUND_LIMIT = 100.0

LANES = 128
SUBLANES = 8
MXU_DIM = 256

ROW_TILE = 512
GDN_ROW_TILE = 512
GDN_COL_BLOCK = 2 * MXU_DIM
FF_CHUNK = 1024
DELTA_CHUNK = 64
DELTA_GROUP = 8
DELTA_HEADS = 2
ATTN_TILE = 512
VMEM_LIMIT = 56 * 1024 * 1024

NT_DIMS = (((1,), (1,)), ((), ()))
TN_DIMS = (((0,), (0,)), ((), ()))


def _rms_hat(x):
    return x * lax.rsqrt(jnp.mean(x * x, axis=-1, keepdims=True) + EPS)


def _sigmoid(x):
    return 1.0 / (1.0 + jnp.exp(-x))


def _silu(x):
    h = 0.5 * x
    return h * jnp.tanh(h) + h


def _softplus(x):
    return jnp.maximum(x, 0.0) + jnp.log(1.0 + jnp.exp(-jnp.abs(x)))


def _dot(a, b):
    return jnp.dot(a, b, preferred_element_type=F32)


def _const_spec(shape):
    zeros = (0,) * len(shape)
    return pl.BlockSpec(shape, lambda *_: zeros, pipeline_mode=pl.Buffered(1))


def _gdn_in_kernel(x_ref, nw_ref, w_ref, wgt_ref, cw_ref, alog_ref, dtb_ref,
                   q_ref, k_ref, v_ref, z_ref, gate_ref, xn_ref, tail_ref, *, tiles_per_seq):
    tm = x_ref.shape[0]
    width = HEADS * HEAD_DIM
    step = pl.program_id(0)

    @pl.when(step == 0)
    def _():
        tail_ref[...] = jnp.zeros(tail_ref.shape, F32)

    xn_ref[...] = (_rms_hat(x_ref[...]) * nw_ref[...]).astype(BF16)
    seq_start = (step % tiles_per_seq) == 0

    cb = GDN_COL_BLOCK
    blocks = 3 * width // cb

    def project(blk):
        return _dot(xn_ref[...], w_ref[:, blk * cb:(blk + 1) * cb])

    p = project(0)
    for blk in range(blocks):
        p_next = project(blk + 1) if blk + 1 < blocks else None
        cols = slice(blk * cb, (blk + 1) * cb)
        prev = jnp.where(seq_start, 0.0, tail_ref[:, cols])
        tail_ref[:, cols] = p[tm - SUBLANES:, :]
        xp = jnp.concatenate([prev, p], axis=0)
        cw = cw_ref[:, cols]
        c = cw[CONV_WIDTH - 1:CONV_WIDTH, :] * p
        for j in range(CONV_WIDTH - 1):
            c = c + cw[j:j + 1, :] * pltpu.roll(xp, CONV_WIDTH - 1 - j, 0)[SUBLANES:, :]
        kind, col = divmod(blk * cb, width)
        (q_ref, k_ref, v_ref)[kind][:, col:col + cb] = c
        p = p_next

    z_ref[...] = _dot(xn_ref[...], w_ref[:, 3 * width:4 * width])

    gt = lax.dot_general(wgt_ref[...], xn_ref[...], NT_DIMS, preferred_element_type=F32)
    beta = _sigmoid(gt)
    decay = -jnp.exp(alog_ref[...]) * _softplus(gt + dtb_ref[...])
    row = lax.broadcasted_iota(jnp.int32, gt.shape, 0)
    gate_ref[...] = jnp.where(row < HEADS, beta, decay)


def _gdn_in(x, norm_w, w_main, wg_t, conv_w, alog16, dtb16, seq_len):
    t, d = x.shape
    tm = GDN_ROW_TILE
    width = HEADS * HEAD_DIM
    act = jax.ShapeDtypeStruct((t, width), F32)
    row_spec = pl.BlockSpec((tm, width), lambda i: (i, 0))
    return pl.pallas_call(
        functools.partial(_gdn_in_kernel, tiles_per_seq=seq_len // tm),
        grid=(t // tm,),
        in_specs=[
            pl.BlockSpec((tm, d), lambda i: (i, 0)),
            _const_spec((1, d)),
            _const_spec(w_main.shape),
            _const_spec(wg_t.shape),
            _const_spec(conv_w.shape),
            _const_spec(alog16.shape),
            _const_spec(dtb16.shape),
        ],
        out_specs=[row_spec, row_spec, row_spec, row_spec,
                   pl.BlockSpec((2 * HEADS, tm), lambda i: (0, i))],
        out_shape=[act, act, act, act, jax.ShapeDtypeStruct((2 * HEADS, t), F32)],
        scratch_shapes=[pltpu.VMEM((tm, d), BF16), pltpu.VMEM((SUBLANES, 3 * width), F32)],
        compiler_params=pltpu.CompilerParams(dimension_semantics=("arbitrary",),
                                             vmem_limit_bytes=VMEM_LIMIT),
        name="gdn_in",
    )(x, norm_w, w_main, wg_t, conv_w, alog16, dtb16)


def _delta_kernel(q_ref, k_ref, v_ref, z_ref, beta_ref, g_ref, onw_ref, o_ref, gc_ref,
                  qw_a, b_a, op_a, cd_a, qw_b, b_b, op_b, cd_b, *, chunk, group, heads):
    c_len = chunk
    n_groups = q_ref.shape[0] // (c_len * group)
    bufs = ((qw_a, b_a, op_a, cd_a), (qw_b, b_b, op_b, cd_b))
    row = lax.broadcasted_iota(jnp.int32, (c_len, c_len), 0)
    col = lax.broadcasted_iota(jnp.int32, (c_len, c_len), 1)
    causal = row >= col
    strict = row > col
    eye = row == col
    upper = (row <= col).astype(F32)
    for hh in range(heads):
        gc_ref[hh] = jnp.dot(g_ref[hh], upper, precision=lax.Precision.HIGHEST,
                             preferred_element_type=F32)
    onw = onw_ref[...]
    n_steps = int(math.log2(c_len))
    chains = [(g, hh) for g in range(group) for hh in range(heads)]

    def to_col(r):
        return jnp.sum(jnp.where(eye, r, 0.0), axis=1, keepdims=True)

    def chunk_rows(c):
        return pl.ds(pl.multiple_of(c * c_len, c_len), c_len)

    def head_cols(hh):
        return slice(hh * HEAD_DIM, (hh + 1) * HEAD_DIM)

    def unit_rows(a, scale):
        return a * (lax.rsqrt(jnp.sum(a * a, axis=-1, keepdims=True) + EPS) * scale)

    def run(prep, adv, st):
        todo = list(range(group)) if adv is not None else []

        def advance_one(st):
            if not todo:
                return st
            g = todo.pop(0)
            grp, (qw_ref, b_ref, op_ref, cd_ref) = adv
            rows = chunk_rows(grp * group + g)
            slots = [g * heads + hh for hh in range(heads)]
            res = [_dot(qw_ref[slots[hh]], st[hh].astype(BF16)) for hh in range(heads)]
            out = [res[hh][:c_len] + op_ref[slots[hh]] for hh in range(heads)]
            st = [cd_ref[slots[hh], 0:1, :] * st[hh] - res[hh][c_len:] + b_ref[slots[hh]]
                  for hh in range(heads)]
            for hh in range(heads):
                zc = z_ref[rows, head_cols(hh)]
                o_ref[rows, head_cols(hh)] = (_rms_hat(out[hh]) * onw * _silu(zc)).astype(o_ref.dtype)
            return st

        if prep is not None:
            grp, (qw_ref, b_ref, op_ref, cd_ref) = prep
            n = len(chains)
            cidx = [grp * group + g for g, _ in chains]
            q = [unit_rows(_silu(q_ref[chunk_rows(cidx[i]), head_cols(chains[i][1])]),
                           HEAD_DIM ** -0.5) for i in range(n)]
            k = [unit_rows(_silu(k_ref[chunk_rows(cidx[i]), head_cols(chains[i][1])]), 1.0)
                 for i in range(n)]
            gc_r = [gc_ref[chains[i][1], pl.ds(cidx[i], 1), :] for i in range(n)]
            beta_c = [to_col(beta_ref[chains[i][1], pl.ds(cidx[i], 1), :]) for i in range(n)]
            gc_c = [to_col(r) for r in gc_r]
            gc_last = [r[:, c_len - 1:c_len] for r in gc_r]
            e_c = [jnp.exp(x) for x in gc_c]
            kb = [k[i] * beta_c[i] for i in range(n)]
            s = [lax.dot_general(jnp.concatenate([kb[i], q[i]], axis=0).astype(BF16),
                                 k[i].astype(BF16), NT_DIMS, preferred_element_type=F32)
                 for i in range(n)]
            st = advance_one(st)
            decay = [jnp.where(causal, jnp.exp(jnp.where(causal, gc_c[i] - gc_r[i], 0.0)), 0.0)
                     for i in range(n)]
            p = [jnp.where(strict, -(s[i][:c_len] * decay[i]), 0.0) for i in range(n)]
            intra = [s[i][c_len:] * decay[i] for i in range(n)]
            r = p
            for step in range(1, n_steps):
                p16 = [x.astype(BF16) for x in p]
                p = [_dot(x, x) for x in p16]
                p16 = [x.astype(BF16) for x in p]
                r = [r[i] + p[i] + _dot(p16[i], r[i].astype(BF16)) for i in range(n)]
                st = advance_one(st)
            rhs = [jnp.concatenate([_silu(v_ref[chunk_rows(cidx[i]), head_cols(chains[i][1])]) * beta_c[i],
                                    kb[i] * e_c[i]], axis=1) for i in range(n)]
            y = [rhs[i] + _dot(r[i].astype(BF16), rhs[i].astype(BF16)) for i in range(n)]
            st = advance_one(st)
            y16 = [x.astype(BF16) for x in y]
            kd16 = [(k[i] * jnp.exp(gc_last[i] - gc_c[i])).astype(BF16) for i in range(n)]
            mb = [lax.dot_general(kd16[i], y16[i], TN_DIMS, preferred_element_type=F32)
                  for i in range(n)]
            iu = [_dot(intra[i].astype(BF16), y16[i]) for i in range(n)]
            st = advance_one(st)
            for i in range(n):
                b_ref[i] = mb[i][:, :HEAD_DIM]
                qw_ref[i, 0:c_len, :] = (q[i] * e_c[i] - iu[i][:, HEAD_DIM:]).astype(BF16)
                qw_ref[i, c_len:, :] = mb[i][:, HEAD_DIM:].astype(BF16)
                op_ref[i] = iu[i][:, :HEAD_DIM]
                cd_ref[i] = jnp.broadcast_to(jnp.exp(gc_last[i]), (SUBLANES, LANES))
        while todo:
            st = advance_one(st)
        return st

    st = run((0, bufs[0]), None, [jnp.zeros((HEAD_DIM, HEAD_DIM), F32) for _ in range(heads)])

    def body(i, st):
        st = run((2 * i + 1, bufs[1]), (2 * i, bufs[0]), st)
        return run((2 * i + 2, bufs[0]), (2 * i + 1, bufs[1]), st)

    st = lax.fori_loop(0, n_groups // 2 - 1, body, st)
    st = run((n_groups - 1, bufs[1]), (n_groups - 2, bufs[0]), st)
    run(None, (n_groups - 1, bufs[1]), st)


def _delta(q, k, v, z, gates, out_norm, batch, seq_len):
    t, width = q.shape
    c_len = DELTA_CHUNK
    nh = DELTA_HEADS
    n_chunks = seq_len // c_len
    assert n_chunks % (2 * DELTA_GROUP) == 0 and HEADS % nh == 0
    gates3 = gates.reshape(2 * HEADS, batch * n_chunks, c_len)
    seq_spec = pl.BlockSpec((seq_len, nh * HEAD_DIM), lambda b, h: (b, h))
    slots = DELTA_GROUP * nh
    buf_set = [pltpu.VMEM((slots, c_len + HEAD_DIM, HEAD_DIM), BF16),
               pltpu.VMEM((slots, HEAD_DIM, HEAD_DIM), F32),
               pltpu.VMEM((slots, c_len, HEAD_DIM), F32),
               pltpu.VMEM((slots, SUBLANES, LANES), F32)]
    return pl.pallas_call(
        functools.partial(_delta_kernel, chunk=c_len, group=DELTA_GROUP, heads=nh),
        grid=(batch, HEADS // nh),
        in_specs=[seq_spec, seq_spec, seq_spec, seq_spec,
                  pl.BlockSpec((nh, n_chunks, c_len), lambda b, h: (h, b, 0)),
                  pl.BlockSpec((nh, n_chunks, c_len), lambda b, h: (HEADS // nh + h, b, 0)),
                  pl.BlockSpec((1, HEAD_DIM), lambda b, h: (0, 0))],
        out_specs=seq_spec,
        out_shape=jax.ShapeDtypeStruct((t, width), BF16),
        scratch_shapes=[pltpu.VMEM((nh, n_chunks, c_len), F32)] + buf_set + buf_set,
        compiler_params=pltpu.CompilerParams(dimension_semantics=("parallel", "parallel"),
                                             vmem_limit_bytes=VMEM_LIMIT),
        name="delta",
    )(q, k, v, z, gates3, gates3, out_norm)


def _proj_mlp_kernel(o_ref, x_ref, wo_ref, nw_ref, w1_ref, w2_ref, out_ref):
    x1 = x_ref[...] + _dot(o_ref[...].astype(BF16), wo_ref[...])
    xn = (_rms_hat(x1) * nw_ref[...]).astype(BF16)
    acc = x1
    for j in range(D_FF // FF_CHUNK):
        h = jnp.maximum(_dot(xn, w1_ref[:, j * FF_CHUNK:(j + 1) * FF_CHUNK]), 0.0)
        acc = acc + _dot((h * h).astype(BF16), w2_ref[j * FF_CHUNK:(j + 1) * FF_CHUNK, :])
    out_ref[...] = acc


def _layer_spec(stacked, layer):
    return pl.BlockSpec((None,) + stacked.shape[1:], lambda *_: (layer, 0, 0),
                        pipeline_mode=pl.Buffered(1))


def _proj_mlp(o, x, w_out, norm_w, w1, w2, layer):
    t, d = x.shape
    tm = ROW_TILE
    return pl.pallas_call(
        _proj_mlp_kernel,
        grid=(t // tm,),
        in_specs=[pl.BlockSpec((tm, o.shape[1]), lambda i: (i, 0)),
                  pl.BlockSpec((tm, d), lambda i: (i, 0)),
                  _const_spec(w_out.shape), _const_spec((1, d)),
                  _layer_spec(w1, layer), _layer_spec(w2, layer)],
        out_specs=pl.BlockSpec((tm, d), lambda i: (i, 0)),
        out_shape=jax.ShapeDtypeStruct((t, d), F32),
        compiler_params=pltpu.CompilerParams(dimension_semantics=("parallel",),
                                             vmem_limit_bytes=VMEM_LIMIT),
        name="proj_mlp",
    )(o, x, w_out, norm_w, w1, w2)


def _rope_tab_kernel(pos_ref, freq_ref, cos_ref, s1_ref, s2_ref):
    ang = pos_ref[...].astype(F32) * freq_ref[...]
    sin = jnp.sin(ang)
    lane = lax.broadcasted_iota(jnp.int32, ang.shape, 1)
    first_half = (lane % MAP_DIM) < ROPE_HALF
    cos_ref[...] = jnp.cos(ang)
    s1_ref[...] = jnp.where(first_half, -sin, 0.0)
    s2_ref[...] = jnp.where(first_half, 0.0, sin)


def _rope_tab(pos_col, freq_row):
    t = pos_col.shape[0]
    tm = 2048
    tab = jax.ShapeDtypeStruct((t, LANES), F32)
    spec = pl.BlockSpec((tm, LANES), lambda i: (i, 0))
    return pl.pallas_call(
        _rope_tab_kernel,
        grid=(t // tm,),
        in_specs=[pl.BlockSpec((tm, 1), lambda i: (i, 0)), _const_spec((1, LANES))],
        out_specs=[spec, spec, spec],
        out_shape=[tab, tab, tab],
        compiler_params=pltpu.CompilerParams(dimension_semantics=("parallel",)),
        name="rope_tab",
    )(pos_col, freq_row)


def _attn_in_kernel(x_ref, kvn_ref, qnw_ref, wkv_ref, wq_ref, kg_ref, qg_ref,
                    cos_ref, s1_ref, s2_ref, k_ref, v_ref, q_ref, kvx_ref, qx_ref):
    width = HEADS * HEAD_DIM
    cb = 2 * MXU_DIM
    xhat = _rms_hat(x_ref[...])
    kvx_ref[...] = (xhat * kvn_ref[...]).astype(BF16)
    qx_ref[...] = (xhat * qnw_ref[...]).astype(BF16)
    cos = cos_ref[...]
    s1 = s1_ref[...]
    s2 = s2_ref[...]
    r = lax.broadcasted_iota(jnp.int32, (MXU_DIM, MXU_DIM), 0) // MAP_DIM
    c = lax.broadcasted_iota(jnp.int32, (MXU_DIM, MXU_DIM), 1) // MAP_DIM
    group_ones = (r == c).astype(BF16)

    def norm_rope(raw, gain, scale, o_ref, col):
        for s in range(raw.shape[1] // MXU_DIM):
            blk = raw[:, s * MXU_DIM:(s + 1) * MXU_DIM]
            ss = _dot((blk * blk).astype(BF16), group_ones)
            lo = col + s * MXU_DIM
            nb = blk * lax.rsqrt(ss * (1.0 / MAP_DIM) + EPS) * gain[:, lo:lo + MXU_DIM]
            for hh in range(MXU_DIM // LANES):
                xb = nb[:, hh * LANES:(hh + 1) * LANES]
                rot = xb * cos + pltpu.roll(xb, LANES - ROPE_HALF, 1) * s1 + pltpu.roll(xb, ROPE_HALF, 1) * s2
                o_ref[:, lo + hh * LANES:lo + (hh + 1) * LANES] = (rot * scale).astype(o_ref.dtype)

    plan = []
    for col in range(0, width, cb):
        plan += [("k", col), ("q", col)]
    plan += [("v", col) for col in range(0, width, cb)]

    def project(item):
        kind, col = item
        if kind == "q":
            return _dot(qx_ref[...], wq_ref[:, col:col + cb])
        base = 0 if kind == "k" else width
        return _dot(kvx_ref[...], wkv_ref[:, base + col:base + col + cb])

    raw = project(plan[0])
    for i, (kind, col) in enumerate(plan):
        raw_next = project(plan[i + 1]) if i + 1 < len(plan) else None
        if kind == "v":
            v_ref[:, col:col + cb] = raw.astype(v_ref.dtype)
        elif kind == "k":
            norm_rope(raw, kg_ref[...], 1.0, k_ref, col)
        else:
            norm_rope(raw, qg_ref[...], MAP_DIM ** -0.5 * LOG2E, q_ref, col)
        raw = raw_next


def _attn_in(x, kv_norm, q_norm_w, w_kv, w_q, k_gain, q_gain, cos, s1, s2):
    t, d = x.shape
    tm = ROW_TILE
    width = HEADS * HEAD_DIM
    act = jax.ShapeDtypeStruct((t, width), BF16)
    row_spec = pl.BlockSpec((tm, width), lambda i: (i, 0))
    tab_spec = pl.BlockSpec((tm, LANES), lambda i: (i, 0))
    return pl.pallas_call(
        _attn_in_kernel,
        grid=(t // tm,),
        in_specs=[pl.BlockSpec((tm, d), lambda i: (i, 0)),
                  _const_spec((1, d)), _const_spec((1, d)),
                  _const_spec(w_kv.shape), _const_spec(w_q.shape),
                  _const_spec((1, width)), _const_spec((1, width)),
                  tab_spec, tab_spec, tab_spec],
        out_specs=[row_spec, row_spec, row_spec],
        out_shape=[act, act, act],
        scratch_shapes=[pltpu.VMEM((tm, d), BF16), pltpu.VMEM((tm, d), BF16)],
        compiler_params=pltpu.CompilerParams(dimension_semantics=("parallel",),
                                             vmem_limit_bytes=VMEM_LIMIT),
        name="attn_in",
    )(x, kv_norm, q_norm_w, w_kv, w_q, k_gain, q_gain, cos, s1, s2)


def _diff_attn_kernel(q_ref, k_ref, v_ref, qg_ref, kg_ref, lam_ref, snw_ref, o_ref,
                      m_ref, l_ref, acc_ref, *, tq, lam_init):
    n_q = q_ref.shape[0] // tq
    half = tq // 2
    lane = lax.broadcasted_iota(jnp.int32, (tq, HEAD_DIM), 1)
    lp = lam_ref[...]
    lam = (jnp.exp(jnp.sum(lp[0:1] * lp[1:2], axis=-1, keepdims=True))
           - jnp.exp(jnp.sum(lp[2:3] * lp[3:4], axis=-1, keepdims=True)) + lam_init)
    snw = snw_ref[...]

    def q_maps(rows):
        q = q_ref[rows, :]
        zero = jnp.zeros_like(q)
        return (jnp.where(lane < MAP_DIM, q, zero), jnp.where(lane < MAP_DIM, zero, q))

    def scores(qm, rows):
        ks = k_ref[rows, :]
        return [lax.dot_general(x, ks, NT_DIMS, preferred_element_type=F32) for x in qm]

    def finish(rows, acc0, acc1, l0, l1):
        o = acc0 / l0 - lam * (acc1 / l1)
        o_ref[rows, :] = (_rms_hat(o) * snw * (1.0 - lam_init)).astype(o_ref.dtype)

    bound = SCORE_BOUND_COEF * jnp.max(jnp.abs(qg_ref[...])) * jnp.max(jnp.abs(kg_ref[...]))
    bounded = bound <= SCORE_BOUND_LIMIT

    @pl.when(bounded)
    def _():
        def lane_sums(p):
            out = p[:, 0:LANES]
            for b in range(1, p.shape[1] // LANES):
                out = out + p[:, b * LANES:(b + 1) * LANES]
            return out

        def plus(a, b):
            return b if a is None else a + b

        for qi in range(n_q):
            q0 = qi * tq
            qm = q_maps(slice(q0, q0 + tq))
            acc = [None, None]
            lsum = [None, None]
            s_next = scores(qm, slice(0, tq)) if qi > 0 else None
            for j in range(qi):
                s = s_next
                s_next = scores(qm, slice((j + 1) * tq, (j + 2) * tq)) if j + 1 < qi else None
                vs = v_ref[j * tq:(j + 1) * tq, :]
                for mi in range(2):
                    p = jnp.exp2(s[mi])
                    lsum[mi] = plus(lsum[mi], lane_sums(p))
                    acc[mi] = plus(acc[mi], _dot(p.astype(BF16), vs))
            for bi, ncols in enumerate((half, tq)):
                rows = slice(bi * half, (bi + 1) * half)
                sd = scores([x[rows, :] for x in qm], slice(q0, q0 + ncols))
                r = lax.broadcasted_iota(jnp.int32, (half, ncols), 0) + bi * half
                c = lax.broadcasted_iota(jnp.int32, (half, ncols), 1)
                vs = v_ref[q0:q0 + ncols, :]
                fin = []
                for mi in range(2):
                    p = jnp.where(r >= c, jnp.exp2(sd[mi]), 0.0)
                    below_l = None if lsum[mi] is None else lsum[mi][rows, :]
                    below_a = None if acc[mi] is None else acc[mi][rows, :]
                    fin.append((plus(below_a, _dot(p.astype(BF16), vs)),
                                jnp.sum(plus(below_l, lane_sums(p)), axis=-1, keepdims=True)))
                finish(slice(q0 + bi * half, q0 + (bi + 1) * half),
                       fin[0][0], fin[1][0], fin[0][1], fin[1][1])

    @pl.when(jnp.logical_not(bounded))
    def _():
        r = lax.broadcasted_iota(jnp.int32, (tq, tq), 0)
        c = lax.broadcasted_iota(jnp.int32, (tq, tq), 1)

        def q_tile(qi, carry):
            q_rows = pl.ds(pl.multiple_of(qi * tq, tq), tq)
            qm = q_maps(q_rows)
            m_ref[...] = jnp.full(m_ref.shape, NEG_INF, F32)
            l_ref[...] = jnp.zeros(l_ref.shape, F32)
            acc_ref[...] = jnp.zeros(acc_ref.shape, F32)

            def step(j, masked):
                kv_rows = pl.ds(pl.multiple_of(j * tq, tq), tq)
                s = scores(qm, kv_rows)
                vs = v_ref[kv_rows, :]
                for mi in range(2):
                    sm = jnp.where(r >= c, s[mi], NEG_INF) if masked else s[mi]
                    m_old = m_ref[mi]
                    m_new = jnp.maximum(m_old, jnp.max(sm, axis=-1, keepdims=True))
                    alpha = jnp.exp2(m_old - m_new)
                    p = jnp.exp2(sm - m_new[:, 0:1])
                    l_ref[mi] = alpha * l_ref[mi] + jnp.sum(p, axis=-1, keepdims=True)
                    acc_ref[mi] = alpha * acc_ref[mi] + _dot(p.astype(BF16), vs)
                    m_ref[mi] = m_new

            def full_body(j, carry):
                step(j, False)
                return carry

            lax.fori_loop(0, qi, full_body, 0)
            step(qi, True)
            finish(q_rows, acc_ref[0], acc_ref[1], l_ref[0], l_ref[1])
            return carry

        lax.fori_loop(0, n_q, q_tile, 0)


def _diff_attn(q, k, v, q_gain, k_gain, lam_params, sub_norm, batch, seq_len, lam_init):
    t, width = q.shape
    tq = ATTN_TILE
    seq_spec = pl.BlockSpec((seq_len, HEAD_DIM), lambda b, h: (b, h))
    return pl.pallas_call(
        functools.partial(_diff_attn_kernel, tq=tq, lam_init=lam_init),
        grid=(batch, HEADS),
        in_specs=[seq_spec, seq_spec, seq_spec,
                  _const_spec(q_gain.shape), _const_spec(k_gain.shape),
                  _const_spec(lam_params.shape), _const_spec((1, HEAD_DIM))],
        out_specs=seq_spec,
        out_shape=jax.ShapeDtypeStruct((t, width), BF16),
        scratch_shapes=[pltpu.VMEM((2, tq, LANES), F32), pltpu.VMEM((2, tq, LANES), F32),
                        pltpu.VMEM((2, tq, HEAD_DIM), F32)],
        compiler_params=pltpu.CompilerParams(dimension_semantics=("parallel", "parallel"),
                                             vmem_limit_bytes=VMEM_LIMIT),
        name="diff_attn",
    )(q, k, v, q_gain, k_gain, lam_params, sub_norm)


def kernel(x, positions, a_norm, a_w_in, a_conv_w, a_a_log, a_dt_bias, a_out_norm, a_w_out,
           kv_norm, w_kv, k_norm, b_norm, b_w_q, b_q_norm, b_lambda, b_sub_norm, b_w_out,
           mlp_norm, mlp_w1, mlp_w2):
    batch, seq_len, d = x.shape
    assert d == D_MODEL and a_norm.shape[0] == 1 and b_norm.shape[0] == 1
    assert seq_len % ROW_TILE == 0 and seq_len % ATTN_TILE == 0 and seq_len % DELTA_CHUNK == 0
    t = batch * seq_len
    width = HEADS * HEAD_DIM
    xf = x.reshape(t, d)

    w_in = a_w_in[0]
    w_main = w_in.astype(BF16)
    wg_t = w_in[:, 4 * width:].T.astype(BF16)
    pad = jnp.zeros((HEADS, 1), F32)
    alog16 = jnp.concatenate([pad, a_a_log[0].reshape(HEADS, 1)], axis=0)
    dtb16 = jnp.concatenate([pad, a_dt_bias[0].reshape(HEADS, 1)], axis=0)
    q, k, v, z, gates = _gdn_in(xf, a_norm[0].reshape(1, d), w_main, wg_t, a_conv_w[0],
                                alog16, dtb16, seq_len)
    o = _delta(q, k, v, z, gates, a_out_norm[0].reshape(1, HEAD_DIM), batch, seq_len)
    w1_all = mlp_w1.astype(BF16)
    w2_all = mlp_w2.astype(BF16)
    xf = _proj_mlp(o, xf, a_w_out[0].astype(BF16), mlp_norm[0].reshape(1, d), w1_all, w2_all, 0)

    half = ROPE_HALF
    freqs = ROPE_THETA ** (-jnp.arange(half, dtype=F32) / half)
    freq_row = jnp.tile(freqs, LANES // half).reshape(1, LANES)
    cos, s1, s2 = _rope_tab(positions.reshape(t, 1), freq_row)
    k_gain = jnp.tile(k_norm, width // MAP_DIM).reshape(1, width)
    q_gain = jnp.tile(b_q_norm[0], width // MAP_DIM).reshape(1, width)
    kr, vv, qr = _attn_in(xf, kv_norm.reshape(1, d), b_norm[0].reshape(1, d),
                          w_kv.astype(BF16), b_w_q[0].astype(BF16), k_gain, q_gain, cos, s1, s2)
    lam_init = 0.8 - 0.6 * math.exp(-0.3 * 1)
    oa = _diff_attn(qr, kr, vv, b_q_norm[0].reshape(1, MAP_DIM), k_norm.reshape(1, MAP_DIM),
                    b_lambda[0], b_sub_norm[0].reshape(1, HEAD_DIM), batch, seq_len, lam_init)
    xf = _proj_mlp(oa, xf, b_w_out[0].astype(BF16), mlp_norm[1].reshape(1, d), w1_all, w2_all, 1)
    return xf.reshape(batch, seq_len, d)
```

```python
import functools
import math

import jax
import jax.numpy as jnp
from jax import lax
from jax.experimental import pallas as pl
from jax.experimental.pallas import tpu as pltpu

F32 = jnp.float32
BF16 = jnp.bfloat16

D_MODEL = 1024
HEADS = 8
HEAD_DIM = 128
MAP_DIM = 64
ROPE_HALF = MAP_DIM // 2
CONV_WIDTH = 4
D_FF = 4 * D_MODEL
ROPE_THETA = 10000.0
EPS = 1e-6
NEG_INF = -1e30
LOG2E = math.log2(math.e)
SCORE_BOUND_COEF = MAP_DIM * MAP_DIM ** -0.5 * LOG2E * 1.02
SCORE_BOUND_LIMIT = 100.0

LANES = 128
SUBLANES = 8
MXU_DIM = 256

ROW_TILE = 512
GDN_ROW_TILE = 512
GDN_COL_BLOCK = 2 * MXU_DIM
FF_CHUNK = 1024
DELTA_CHUNK = 64
DELTA_GROUP = 8
DELTA_HEADS = 2
ATTN_TILE = 512
ATTN_DIAG_PIECE = 256
VMEM_LIMIT = 56 * 1024 * 1024

NT_DIMS = (((1,), (1,)), ((), ()))
TN_DIMS = (((0,), (0,)), ((), ()))


def _rms_hat(x):
    return x * lax.rsqrt(jnp.mean(x * x, axis=-1, keepdims=True) + EPS)


def _sigmoid(x):
    return 1.0 / (1.0 + jnp.exp(-x))


def _silu(x):
    h = 0.5 * x
    return h * jnp.tanh(h) + h


def _softplus(x):
    return jnp.maximum(x, 0.0) + jnp.log(1.0 + jnp.exp(-jnp.abs(x)))


def _dot(a, b):
    return jnp.dot(a, b, preferred_element_type=F32)


def _const_spec(shape):
    zeros = (0,) * len(shape)
    return pl.BlockSpec(shape, lambda *_: zeros, pipeline_mode=pl.Buffered(1))


def _gdn_in_kernel(x_ref, nw_ref, w_ref, cw_ref, alog_ref, dtb_ref,
                   q_ref, k_ref, v_ref, z_ref, gate_ref, xn_ref, tail_ref, *, tiles_per_seq):
    tm = x_ref.shape[0]
    width = HEADS * HEAD_DIM
    step = pl.program_id(0)

    @pl.when(step == 0)
    def _():
        tail_ref[...] = jnp.zeros(tail_ref.shape, F32)

    xn_ref[...] = (_rms_hat(x_ref[...]) * nw_ref[...]).astype(BF16)
    seq_start = (step % tiles_per_seq) == 0

    cb = GDN_COL_BLOCK
    blocks = 3 * width // cb

    def project(blk):
        return _dot(xn_ref[...], w_ref[:, blk * cb:(blk + 1) * cb])

    p = project(0)
    for blk in range(blocks):
        p_next = project(blk + 1) if blk + 1 < blocks else None
        cols = slice(blk * cb, (blk + 1) * cb)
        prev = jnp.where(seq_start, 0.0, tail_ref[:, cols])
        tail_ref[:, cols] = p[tm - SUBLANES:, :]
        xp = jnp.concatenate([prev, p], axis=0)
        cw = cw_ref[:, cols]
        c = cw[CONV_WIDTH - 1:CONV_WIDTH, :] * p
        for j in range(CONV_WIDTH - 1):
            c = c + cw[j:j + 1, :] * pltpu.roll(xp, CONV_WIDTH - 1 - j, 0)[SUBLANES:, :]
        kind, col = divmod(blk * cb, width)
        (q_ref, k_ref, v_ref)[kind][:, col:col + cb] = c
        p = p_next

    z_ref[...] = _dot(xn_ref[...], w_ref[:, 3 * width:4 * width])

    g_rows = _dot(xn_ref[...], w_ref[:, 4 * width:])
    eye_g = (lax.broadcasted_iota(jnp.int32, (2 * HEADS, 2 * HEADS), 0)
             == lax.broadcasted_iota(jnp.int32, (2 * HEADS, 2 * HEADS), 1)).astype(F32)
    gt = lax.dot_general(eye_g, g_rows, NT_DIMS, precision=lax.Precision.HIGHEST,
                         preferred_element_type=F32)
    beta = _sigmoid(gt)
    decay = -jnp.exp(alog_ref[...]) * _softplus(gt + dtb_ref[...])
    row = lax.broadcasted_iota(jnp.int32, gt.shape, 0)
    gate_ref[...] = jnp.where(row < HEADS, beta, decay)


def _gdn_in(x, norm_w, w_in, conv_w, alog16, dtb16, seq_len):
    t, d = x.shape
    tm = GDN_ROW_TILE
    width = HEADS * HEAD_DIM
    act = jax.ShapeDtypeStruct((t, width), F32)
    row_spec = pl.BlockSpec((tm, width), lambda i: (i, 0))
    return pl.pallas_call(
        functools.partial(_gdn_in_kernel, tiles_per_seq=seq_len // tm),
        grid=(t // tm,),
        in_specs=[
            pl.BlockSpec((tm, d), lambda i: (i, 0)),
            _const_spec((1, d)),
            _const_spec(w_in.shape),
            _const_spec(conv_w.shape),
            _const_spec(alog16.shape),
            _const_spec(dtb16.shape),
        ],
        out_specs=[row_spec, row_spec, row_spec, row_spec,
                   pl.BlockSpec((2 * HEADS, tm), lambda i: (0, i))],
        out_shape=[act, act, act, act, jax.ShapeDtypeStruct((2 * HEADS, t), F32)],
        scratch_shapes=[pltpu.VMEM((tm, d), BF16), pltpu.VMEM((SUBLANES, 3 * width), F32)],
        compiler_params=pltpu.CompilerParams(dimension_semantics=("arbitrary",),
                                             vmem_limit_bytes=VMEM_LIMIT),
        name="gdn_in",
    )(x, norm_w, w_in, conv_w, alog16, dtb16)


def _delta_kernel(q_ref, k_ref, v_ref, z_ref, beta_ref, g_ref, onw_ref, o_ref, gc_ref,
                  qw_a, b_a, op_a, cd_a, qw_b, b_b, op_b, cd_b, *, chunk, group, heads):
    c_len = chunk
    n_groups = q_ref.shape[0] // (c_len * group)
    bufs = ((qw_a, b_a, op_a, cd_a), (qw_b, b_b, op_b, cd_b))
    row = lax.broadcasted_iota(jnp.int32, (c_len, c_len), 0)
    col = lax.broadcasted_iota(jnp.int32, (c_len, c_len), 1)
    causal = row >= col
    strict = row > col
    eye = row == col
    upper = (row <= col).astype(F32)
    for hh in range(heads):
        gc_ref[hh] = jnp.dot(g_ref[hh], upper, precision=lax.Precision.HIGHEST,
                             preferred_element_type=F32)
    onw = onw_ref[...]
    n_steps = int(math.log2(c_len))
    chains = [(g, hh) for g in range(group) for hh in range(heads)]

    def to_col(r):
        return jnp.sum(jnp.where(eye, r, 0.0), axis=1, keepdims=True)

    def chunk_rows(c):
        return pl.ds(pl.multiple_of(c * c_len, c_len), c_len)

    def head_cols(hh):
        return slice(hh * HEAD_DIM, (hh + 1) * HEAD_DIM)

    def unit_rows(a, scale):
        return a * (lax.rsqrt(jnp.sum(a * a, axis=-1, keepdims=True) + EPS) * scale)

    def run(prep, adv, st):
        todo = list(range(group)) if adv is not None else []

        def advance_one(st):
            if not todo:
                return st
            g = todo.pop(0)
            grp, (qw_ref, b_ref, op_ref, cd_ref) = adv
            rows = chunk_rows(grp * group + g)
            slots = [g * heads + hh for hh in range(heads)]
            res = [_dot(qw_ref[slots[hh]], st[hh].astype(BF16)) for hh in range(heads)]
            out = [res[hh][:c_len] + op_ref[slots[hh]] for hh in range(heads)]
            st = [cd_ref[slots[hh], 0:1, :] * st[hh] - res[hh][c_len:] + b_ref[slots[hh]]
                  for hh in range(heads)]
            for hh in range(heads):
                zc = z_ref[rows, head_cols(hh)]
                o_ref[rows, head_cols(hh)] = (_rms_hat(out[hh]) * onw * _silu(zc)).astype(o_ref.dtype)
            return st

        if prep is not None:
            grp, (qw_ref, b_ref, op_ref, cd_ref) = prep
            n = len(chains)
            cidx = [grp * group + g for g, _ in chains]
            q = [unit_rows(_silu(q_ref[chunk_rows(cidx[i]), head_cols(chains[i][1])]),
                           HEAD_DIM ** -0.5) for i in range(n)]
            k = [unit_rows(_silu(k_ref[chunk_rows(cidx[i]), head_cols(chains[i][1])]), 1.0)
                 for i in range(n)]
            gc_r = [gc_ref[chains[i][1], pl.ds(cidx[i], 1), :] for i in range(n)]
            beta_c = [to_col(beta_ref[chains[i][1], pl.ds(cidx[i], 1), :]) for i in range(n)]
            gc_c = [to_col(r) for r in gc_r]
            gc_last = [r[:, c_len - 1:c_len] for r in gc_r]
            e_c = [jnp.exp(x) for x in gc_c]
            kb = [k[i] * beta_c[i] for i in range(n)]
            s = [lax.dot_general(jnp.concatenate([kb[i], q[i]], axis=0).astype(BF16),
                                 k[i].astype(BF16), NT_DIMS, preferred_element_type=F32)
                 for i in range(n)]
            st = advance_one(st)
            decay = [jnp.where(causal, jnp.exp(jnp.where(causal, gc_c[i] - gc_r[i], 0.0)), 0.0)
                     for i in range(n)]
            p = [jnp.where(strict, -(s[i][:c_len] * decay[i]), 0.0) for i in range(n)]
            intra = [s[i][c_len:] * decay[i] for i in range(n)]
            r = p
            for step in range(1, n_steps):
                p16 = [x.astype(BF16) for x in p]
                p = [_dot(x, x) for x in p16]
                p16 = [x.astype(BF16) for x in p]
                r = [r[i] + p[i] + _dot(p16[i], r[i].astype(BF16)) for i in range(n)]
                st = advance_one(st)
            rhs = [jnp.concatenate([_silu(v_ref[chunk_rows(cidx[i]), head_cols(chains[i][1])]) * beta_c[i],
                                    kb[i] * e_c[i]], axis=1) for i in range(n)]
            y = [rhs[i] + _dot(r[i].astype(BF16), rhs[i].astype(BF16)) for i in range(n)]
            st = advance_one(st)
            y16 = [x.astype(BF16) for x in y]
            kd16 = [(k[i] * jnp.exp(gc_last[i] - gc_c[i])).astype(BF16) for i in range(n)]
            mb = [lax.dot_general(kd16[i], y16[i], TN_DIMS, preferred_element_type=F32)
                  for i in range(n)]
            iu = [_dot(intra[i].astype(BF16), y16[i]) for i in range(n)]
            st = advance_one(st)
            for i in range(n):
                b_ref[i] = mb[i][:, :HEAD_DIM]
                qw_ref[i, 0:c_len, :] = (q[i] * e_c[i] - iu[i][:, HEAD_DIM:]).astype(BF16)
                qw_ref[i, c_len:, :] = mb[i][:, HEAD_DIM:].astype(BF16)
                op_ref[i] = iu[i][:, :HEAD_DIM]
                cd_ref[i] = jnp.broadcast_to(jnp.exp(gc_last[i]), (SUBLANES, LANES))
        while todo:
            st = advance_one(st)
        return st

    st = run((0, bufs[0]), None, [jnp.zeros((HEAD_DIM, HEAD_DIM), F32) for _ in range(heads)])

    def body(i, st):
        st = run((2 * i + 1, bufs[1]), (2 * i, bufs[0]), st)
        return run((2 * i + 2, bufs[0]), (2 * i + 1, bufs[1]), st)

    st = lax.fori_loop(0, n_groups // 2 - 1, body, st)
    st = run((n_groups - 1, bufs[1]), (n_groups - 2, bufs[0]), st)
    run(None, (n_groups - 1, bufs[1]), st)


def _delta(q, k, v, z, gates, out_norm, batch, seq_len):
    t, width = q.shape
    c_len = DELTA_CHUNK
    nh = DELTA_HEADS
    n_chunks = seq_len // c_len
    assert n_chunks % (2 * DELTA_GROUP) == 0 and HEADS % nh == 0
    gates3 = gates.reshape(2 * HEADS, batch * n_chunks, c_len)
    seq_spec = pl.BlockSpec((seq_len, nh * HEAD_DIM), lambda b, h: (b, h))
    slots = DELTA_GROUP * nh
    buf_set = [pltpu.VMEM((slots, c_len + HEAD_DIM, HEAD_DIM), BF16),
               pltpu.VMEM((slots, HEAD_DIM, HEAD_DIM), F32),
               pltpu.VMEM((slots, c_len, HEAD_DIM), F32),
               pltpu.VMEM((slots, SUBLANES, LANES), F32)]
    return pl.pallas_call(
        functools.partial(_delta_kernel, chunk=c_len, group=DELTA_GROUP, heads=nh),
        grid=(batch, HEADS // nh),
        in_specs=[seq_spec, seq_spec, seq_spec, seq_spec,
                  pl.BlockSpec((nh, n_chunks, c_len), lambda b, h: (h, b, 0)),
                  pl.BlockSpec((nh, n_chunks, c_len), lambda b, h: (HEADS // nh + h, b, 0)),
                  pl.BlockSpec((1, HEAD_DIM), lambda b, h: (0, 0))],
        out_specs=seq_spec,
        out_shape=jax.ShapeDtypeStruct((t, width), BF16),
        scratch_shapes=[pltpu.VMEM((nh, n_chunks, c_len), F32)] + buf_set + buf_set,
        compiler_params=pltpu.CompilerParams(dimension_semantics=("parallel", "parallel"),
                                             vmem_limit_bytes=VMEM_LIMIT),
        name="delta",
    )(q, k, v, z, gates3, gates3, out_norm)


def _proj_mlp_kernel(o_ref, x_ref, wo_ref, nw_ref, w1_ref, w2_ref, out_ref):
    x1 = x_ref[...] + _dot(o_ref[...].astype(BF16), wo_ref[...])
    xn = (_rms_hat(x1) * nw_ref[...]).astype(BF16)
    acc = x1
    for j in range(D_FF // FF_CHUNK):
        h = jnp.maximum(_dot(xn, w1_ref[:, j * FF_CHUNK:(j + 1) * FF_CHUNK]), 0.0)
        acc = acc + _dot((h * h).astype(BF16), w2_ref[j * FF_CHUNK:(j + 1) * FF_CHUNK, :])
    out_ref[...] = acc


def _layer_spec(stacked, layer):
    return pl.BlockSpec((None,) + stacked.shape[1:], lambda *_: (layer, 0, 0),
                        pipeline_mode=pl.Buffered(1))


def _proj_mlp(o, x, w_out, norm_w, w1, w2, layer):
    t, d = x.shape
    tm = ROW_TILE
    return pl.pallas_call(
        _proj_mlp_kernel,
        grid=(t // tm,),
        in_specs=[pl.BlockSpec((tm, o.shape[1]), lambda i: (i, 0)),
                  pl.BlockSpec((tm, d), lambda i: (i, 0)),
                  _const_spec(w_out.shape), _const_spec((1, d)),
                  _layer_spec(w1, layer), _layer_spec(w2, layer)],
        out_specs=pl.BlockSpec((tm, d), lambda i: (i, 0)),
        out_shape=jax.ShapeDtypeStruct((t, d), F32),
        compiler_params=pltpu.CompilerParams(dimension_semantics=("parallel",),
                                             vmem_limit_bytes=VMEM_LIMIT),
        name="proj_mlp",
    )(o, x, w_out, norm_w, w1, w2)


def _rope_tab_kernel(pos_ref, freq_ref, cos_ref, s1_ref, s2_ref):
    ang = pos_ref[...].astype(F32) * freq_ref[...]
    sin = jnp.sin(ang)
    lane = lax.broadcasted_iota(jnp.int32, ang.shape, 1)
    first_half = (lane % MAP_DIM) < ROPE_HALF
    cos_ref[...] = jnp.cos(ang)
    s1_ref[...] = jnp.where(first_half, -sin, 0.0)
    s2_ref[...] = jnp.where(first_half, 0.0, sin)


def _rope_tab(pos_col, freq_row):
    t = pos_col.shape[0]
    tm = 2048
    tab = jax.ShapeDtypeStruct((t, LANES), F32)
    spec = pl.BlockSpec((tm, LANES), lambda i: (i, 0))
    return pl.pallas_call(
        _rope_tab_kernel,
        grid=(t // tm,),
        in_specs=[pl.BlockSpec((tm, 1), lambda i: (i, 0)), _const_spec((1, LANES))],
        out_specs=[spec, spec, spec],
        out_shape=[tab, tab, tab],
        compiler_params=pltpu.CompilerParams(dimension_semantics=("parallel",)),
        name="rope_tab",
    )(pos_col, freq_row)


def _attn_in_kernel(x_ref, kvn_ref, qnw_ref, wkv_ref, wq_ref, kg_ref, qg_ref,
                    cos_ref, s1_ref, s2_ref, k_ref, v_ref, q_ref, kvx_ref, qx_ref):
    width = HEADS * HEAD_DIM
    cb = 2 * MXU_DIM
    xhat = _rms_hat(x_ref[...])
    kvx_ref[...] = (xhat * kvn_ref[...]).astype(BF16)
    qx_ref[...] = (xhat * qnw_ref[...]).astype(BF16)
    cos = cos_ref[...]
    s1 = s1_ref[...]
    s2 = s2_ref[...]
    r = lax.broadcasted_iota(jnp.int32, (MXU_DIM, MXU_DIM), 0) // MAP_DIM
    c = lax.broadcasted_iota(jnp.int32, (MXU_DIM, MXU_DIM), 1) // MAP_DIM
    group_ones = (r == c).astype(BF16)

    def norm_rope(raw, gain, scale, o_ref, col):
        for s in range(raw.shape[1] // MXU_DIM):
            blk = raw[:, s * MXU_DIM:(s + 1) * MXU_DIM]
            ss = _dot((blk * blk).astype(BF16), group_ones)
            lo = col + s * MXU_DIM
            nb = blk * lax.rsqrt(ss * (1.0 / MAP_DIM) + EPS) * gain[:, lo:lo + MXU_DIM]
            for hh in range(MXU_DIM // LANES):
                xb = nb[:, hh * LANES:(hh + 1) * LANES]
                rot = xb * cos + pltpu.roll(xb, LANES - ROPE_HALF, 1) * s1 + pltpu.roll(xb, ROPE_HALF, 1) * s2
                o_ref[:, lo + hh * LANES:lo + (hh + 1) * LANES] = (rot * scale).astype(o_ref.dtype)

    plan = []
    for col in range(0, width, cb):
        plan += [("k", col), ("q", col)]
    plan += [("v", col) for col in range(0, width, cb)]

    def project(item):
        kind, col = item
        if kind == "q":
            return _dot(qx_ref[...], wq_ref[:, col:col + cb])
        base = 0 if kind == "k" else width
        return _dot(kvx_ref[...], wkv_ref[:, base + col:base + col + cb])

    raw = project(plan[0])
    for i, (kind, col) in enumerate(plan):
        raw_next = project(plan[i + 1]) if i + 1 < len(plan) else None
        if kind == "v":
            v_ref[:, col:col + cb] = raw.astype(v_ref.dtype)
        elif kind == "k":
            norm_rope(raw, kg_ref[...], 1.0, k_ref, col)
        else:
            norm_rope(raw, qg_ref[...], MAP_DIM ** -0.5 * LOG2E, q_ref, col)
        raw = raw_next


def _attn_in(x, kv_norm, q_norm_w, w_kv, w_q, k_gain, q_gain, cos, s1, s2):
    t, d = x.shape
    tm = ROW_TILE
    width = HEADS * HEAD_DIM
    act = jax.ShapeDtypeStruct((t, width), BF16)
    row_spec = pl.BlockSpec((tm, width), lambda i: (i, 0))
    tab_spec = pl.BlockSpec((tm, LANES), lambda i: (i, 0))
    return pl.pallas_call(
        _attn_in_kernel,
        grid=(t // tm,),
        in_specs=[pl.BlockSpec((tm, d), lambda i: (i, 0)),
                  _const_spec((1, d)), _const_spec((1, d)),
                  _const_spec(w_kv.shape), _const_spec(w_q.shape),
                  _const_spec((1, width)), _const_spec((1, width)),
                  tab_spec, tab_spec, tab_spec],
        out_specs=[row_spec, row_spec, row_spec],
        out_shape=[act, act, act],
        scratch_shapes=[pltpu.VMEM((tm, d), BF16), pltpu.VMEM((tm, d), BF16)],
        compiler_params=pltpu.CompilerParams(dimension_semantics=("parallel",),
                                             vmem_limit_bytes=VMEM_LIMIT),
        name="attn_in",
    )(x, kv_norm, q_norm_w, w_kv, w_q, k_gain, q_gain, cos, s1, s2)


def _diff_attn_kernel(q_ref, k_ref, v_ref, qg_ref, kg_ref, lam_ref, snw_ref, o_ref,
                      m_ref, l_ref, acc_ref, *, tq, lam_init):
    n_q = q_ref.shape[0] // tq
    piece = ATTN_DIAG_PIECE
    lane = lax.broadcasted_iota(jnp.int32, (tq, HEAD_DIM), 1)
    lp = lam_ref[...]
    lam = (jnp.exp(jnp.sum(lp[0:1] * lp[1:2], axis=-1, keepdims=True))
           - jnp.exp(jnp.sum(lp[2:3] * lp[3:4], axis=-1, keepdims=True)) + lam_init)
    snw = snw_ref[...]

    def q_maps(rows):
        q = q_ref[rows, :]
        zero = jnp.zeros_like(q)
        return (jnp.where(lane < MAP_DIM, q, zero), jnp.where(lane < MAP_DIM, zero, q))

    def scores(qm, rows):
        ks = k_ref[rows, :]
        return [lax.dot_general(x, ks, NT_DIMS, preferred_element_type=F32) for x in qm]

    def finish(rows, acc0, acc1, l0, l1):
        o = acc0 / l0 - lam * (acc1 / l1)
        o_ref[rows, :] = (_rms_hat(o) * snw * (1.0 - lam_init)).astype(o_ref.dtype)

    bound = SCORE_BOUND_COEF * jnp.max(jnp.abs(qg_ref[...])) * jnp.max(jnp.abs(kg_ref[...]))
    bounded = bound <= SCORE_BOUND_LIMIT

    @pl.when(bounded)
    def _():
        def lane_sums(p):
            out = p[:, 0:LANES]
            for b in range(1, p.shape[1] // LANES):
                out = out + p[:, b * LANES:(b + 1) * LANES]
            return out

        def plus(a, b):
            return b if a is None else a + b

        for qi in range(n_q):
            q0 = qi * tq
            qm = q_maps(slice(q0, q0 + tq))
            acc = [None, None]
            lsum = [None, None]
            s_next = scores(qm, slice(0, tq)) if qi > 0 else None
            for j in range(qi):
                s = s_next
                s_next = scores(qm, slice((j + 1) * tq, (j + 2) * tq)) if j + 1 < qi else None
                vs = v_ref[j * tq:(j + 1) * tq, :]
                for mi in range(2):
                    p = jnp.exp2(s[mi])
                    lsum[mi] = plus(lsum[mi], lane_sums(p))
                    acc[mi] = plus(acc[mi], _dot(p.astype(BF16), vs))
            for bi in range(tq // piece):
                rows = slice(bi * piece, (bi + 1) * piece)
                ncols = (bi + 1) * piece
                sd = scores([x[rows, :] for x in qm], slice(q0, q0 + ncols))
                r = lax.broadcasted_iota(jnp.int32, (piece, ncols), 0) + bi * piece
                c = lax.broadcasted_iota(jnp.int32, (piece, ncols), 1)
                vs = v_ref[q0:q0 + ncols, :]
                fin = []
                for mi in range(2):
                    p = jnp.where(r >= c, jnp.exp2(sd[mi]), 0.0)
                    below_l = None if lsum[mi] is None else lsum[mi][rows, :]
                    below_a = None if acc[mi] is None else acc[mi][rows, :]
                    fin.append((plus(below_a, _dot(p.astype(BF16), vs)),
                                jnp.sum(plus(below_l, lane_sums(p)), axis=-1, keepdims=True)))
                finish(slice(q0 + bi * piece, q0 + (bi + 1) * piece),
                       fin[0][0], fin[1][0], fin[0][1], fin[1][1])

    @pl.when(jnp.logical_not(bounded))
    def _():
        r = lax.broadcasted_iota(jnp.int32, (tq, tq), 0)
        c = lax.broadcasted_iota(jnp.int32, (tq, tq), 1)

        def q_tile(qi, carry):
            q_rows = pl.ds(pl.multiple_of(qi * tq, tq), tq)
            qm = q_maps(q_rows)
            m_ref[...] = jnp.full(m_ref.shape, NEG_INF, F32)
            l_ref[...] = jnp.zeros(l_ref.shape, F32)
            acc_ref[...] = jnp.zeros(acc_ref.shape, F32)

            def step(j, masked):
                kv_rows = pl.ds(pl.multiple_of(j * tq, tq), tq)
                s = scores(qm, kv_rows)
                vs = v_ref[kv_rows, :]
                for mi in range(2):
                    sm = jnp.where(r >= c, s[mi], NEG_INF) if masked else s[mi]
                    m_old = m_ref[mi]
                    m_new = jnp.maximum(m_old, jnp.max(sm, axis=-1, keepdims=True))
                    alpha = jnp.exp2(m_old - m_new)
                    p = jnp.exp2(sm - m_new[:, 0:1])
                    l_ref[mi] = alpha * l_ref[mi] + jnp.sum(p, axis=-1, keepdims=True)
                    acc_ref[mi] = alpha * acc_ref[mi] + _dot(p.astype(BF16), vs)
                    m_ref[mi] = m_new

            def full_body(j, carry):
                step(j, False)
                return carry

            lax.fori_loop(0, qi, full_body, 0)
            step(qi, True)
            finish(q_rows, acc_ref[0], acc_ref[1], l_ref[0], l_ref[1])
            return carry

        lax.fori_loop(0, n_q, q_tile, 0)


def _diff_attn(q, k, v, q_gain, k_gain, lam_params, sub_norm, batch, seq_len, lam_init):
    t, width = q.shape
    tq = ATTN_TILE
    seq_spec = pl.BlockSpec((seq_len, HEAD_DIM), lambda b, h: (b, h))
    return pl.pallas_call(
        functools.partial(_diff_attn_kernel, tq=tq, lam_init=lam_init),
        grid=(batch, HEADS),
        in_specs=[seq_spec, seq_spec, seq_spec,
                  _const_spec(q_gain.shape), _const_spec(k_gain.shape),
                  _const_spec(lam_params.shape), _const_spec((1, HEAD_DIM))],
        out_specs=seq_spec,
        out_shape=jax.ShapeDtypeStruct((t, width), BF16),
        scratch_shapes=[pltpu.VMEM((2, tq, LANES), F32), pltpu.VMEM((2, tq, LANES), F32),
                        pltpu.VMEM((2, tq, HEAD_DIM), F32)],
        compiler_params=pltpu.CompilerParams(dimension_semantics=("parallel", "parallel"),
                                             vmem_limit_bytes=VMEM_LIMIT),
        name="diff_attn",
    )(q, k, v, q_gain, k_gain, lam_params, sub_norm)


def kernel(x, positions, a_norm, a_w_in, a_conv_w, a_a_log, a_dt_bias, a_out_norm, a_w_out,
           kv_norm, w_kv, k_norm, b_norm, b_w_q, b_q_norm, b_lambda, b_sub_norm, b_w_out,
           mlp_norm, mlp_w1, mlp_w2):
    batch, seq_len, d = x.shape
    assert d == D_MODEL and a_norm.shape[0] == 1 and b_norm.shape[0] == 1
    assert seq_len % ROW_TILE == 0 and seq_len % ATTN_TILE == 0 and seq_len % DELTA_CHUNK == 0
    t = batch * seq_len
    width = HEADS * HEAD_DIM
    xf = x.reshape(t, d)

    w_in = a_w_in[0]
    pad = jnp.zeros((HEADS, 1), F32)
    alog16 = jnp.concatenate([pad, a_a_log[0].reshape(HEADS, 1)], axis=0)
    dtb16 = jnp.concatenate([pad, a_dt_bias[0].reshape(HEADS, 1)], axis=0)
    q, k, v, z, gates = _gdn_in(xf, a_norm[0].reshape(1, d), w_in.astype(BF16), a_conv_w[0],
                                alog16, dtb16, seq_len)
    o = _delta(q, k, v, z, gates, a_out_norm[0].reshape(1, HEAD_DIM), batch, seq_len)
    w1_all = mlp_w1.astype(BF16)
    w2_all = mlp_w2.astype(BF16)
    xf = _proj_mlp(o, xf, a_w_out[0].astype(BF16), mlp_norm[0].reshape(1, d), w1_all, w2_all, 0)

    half = ROPE_HALF
    freqs = ROPE_THETA ** (-jnp.arange(half, dtype=F32) / half)
    freq_row = jnp.tile(freqs, LANES // half).reshape(1, LANES)
    cos, s1, s2 = _rope_tab(positions.reshape(t, 1), freq_row)
    k_gain = jnp.tile(k_norm, width // MAP_DIM).reshape(1, width)
    q_gain = jnp.tile(b_q_norm[0], width // MAP_DIM).reshape(1, width)
    kr, vv, qr = _attn_in(xf, kv_norm.reshape(1, d), b_norm[0].reshape(1, d),
                          w_kv.astype(BF16), b_w_q[0].astype(BF16), k_gain, q_gain, cos, s1, s2)
    lam_init = 0.8 - 0.6 * math.exp(-0.3 * 1)
    oa = _diff_attn(qr, kr, vv, b_q_norm[0].reshape(1, MAP_DIM), k_norm.reshape(1, MAP_DIM),
                    b_lambda[0], b_sub_norm[0].reshape(1, HEAD_DIM), batch, seq_len, lam_init)
    xf = _proj_mlp(oa, xf, b_w_out[0].astype(BF16), mlp_norm[1].reshape(1, d), w1_all, w2_all, 1)
    return xf.reshape(batch, seq_len, d)
```

```python
import functools
import math

import jax
import jax.numpy as jnp
from jax import lax
from jax.experimental import pallas as pl
from jax.experimental.pallas import tpu as pltpu

F32 = jnp.float32
BF16 = jnp.bfloat16

D_MODEL = 1024
HEADS = 8
HEAD_DIM = 128
MAP_DIM = 64
ROPE_HALF = MAP_DIM // 2
CONV_WIDTH = 4
D_FF = 4 * D_MODEL
ROPE_THETA = 10000.0
EPS = 1e-6
NEG_INF = -1e30
LOG2E = math.log2(math.e)
SCORE_BOUND_COEF = MAP_DIM * MAP_DIM ** -0.5 * LOG2E * 1.02
SCORE_BOUND_LIMIT = 100.0

LANES = 128
SUBLANES = 8
MXU_DIM = 256

ROW_TILE = 512
GDN_ROW_TILE = 512
GDN_COL_BLOCK = 2 * MXU_DIM
FF_CHUNK = 1024
DELTA_CHUNK = 64
DELTA_GROUP = 8
DELTA_HEADS = 2
ATTN_TILE = 512
ATTN_DIAG_PIECE = 256
VMEM_LIMIT = 56 * 1024 * 1024

NT_DIMS = (((1,), (1,)), ((), ()))
TN_DIMS = (((0,), (0,)), ((), ()))


def _rms_hat(x):
    return x * lax.rsqrt(jnp.mean(x * x, axis=-1, keepdims=True) + EPS)


def _sigmoid(x):
    return 1.0 / (1.0 + jnp.exp(-x))


def _silu(x):
    h = 0.5 * x
    return h * jnp.tanh(h) + h


def _softplus(x):
    return jnp.maximum(x, 0.0) + jnp.log(1.0 + jnp.exp(-jnp.abs(x)))


def _dot(a, b):
    return jnp.dot(a, b, preferred_element_type=F32)


def _const_spec(shape):
    zeros = (0,) * len(shape)
    return pl.BlockSpec(shape, lambda *_: zeros, pipeline_mode=pl.Buffered(1))


def _gdn_in_kernel(x_ref, nw_ref, w_ref, cw_ref, alog_ref, dtb_ref,
                   q_ref, k_ref, v_ref, z_ref, gate_ref, xn_ref, tail_ref, *, tiles_per_seq):
    tm = x_ref.shape[0]
    width = HEADS * HEAD_DIM
    step = pl.program_id(0)

    @pl.when(step == 0)
    def _():
        tail_ref[...] = jnp.zeros(tail_ref.shape, F32)

    xn_ref[...] = (_rms_hat(x_ref[...]) * nw_ref[...]).astype(BF16)
    seq_start = (step % tiles_per_seq) == 0

    cb = GDN_COL_BLOCK
    blocks = 3 * width // cb

    def project(blk):
        return _dot(xn_ref[...], w_ref[:, blk * cb:(blk + 1) * cb])

    p = project(0)
    for blk in range(blocks):
        p_next = project(blk + 1) if blk + 1 < blocks else None
        cols = slice(blk * cb, (blk + 1) * cb)
        prev = jnp.where(seq_start, 0.0, tail_ref[:, cols])
        tail_ref[:, cols] = p[tm - SUBLANES:, :]
        xp = jnp.concatenate([prev, p], axis=0)
        cw = cw_ref[:, cols]
        c = cw[CONV_WIDTH - 1:CONV_WIDTH, :] * p
        for j in range(CONV_WIDTH - 1):
            c = c + cw[j:j + 1, :] * pltpu.roll(xp, CONV_WIDTH - 1 - j, 0)[SUBLANES:, :]
        kind, col = divmod(blk * cb, width)
        (q_ref, k_ref, v_ref)[kind][:, col:col + cb] = c
        p = p_next

    z_ref[...] = _dot(xn_ref[...], w_ref[:, 3 * width:4 * width])

    g_rows = _dot(xn_ref[...], w_ref[:, 4 * width:])
    eye_g = (lax.broadcasted_iota(jnp.int32, (2 * HEADS, 2 * HEADS), 0)
             == lax.broadcasted_iota(jnp.int32, (2 * HEADS, 2 * HEADS), 1)).astype(F32)
    gt = lax.dot_general(eye_g, g_rows, NT_DIMS, precision=lax.Precision.HIGHEST,
                         preferred_element_type=F32)
    beta = _sigmoid(gt)
    decay = -jnp.exp(alog_ref[...]) * _softplus(gt + dtb_ref[...])
    row = lax.broadcasted_iota(jnp.int32, gt.shape, 0)
    gate_ref[...] = jnp.where(row < HEADS, beta, decay)


def _gdn_in(x, norm_w, w_in, conv_w, alog16, dtb16, seq_len):
    t, d = x.shape
    tm = GDN_ROW_TILE
    width = HEADS * HEAD_DIM
    act = jax.ShapeDtypeStruct((t, width), F32)
    row_spec = pl.BlockSpec((tm, width), lambda i: (i, 0))
    return pl.pallas_call(
        functools.partial(_gdn_in_kernel, tiles_per_seq=seq_len // tm),
        grid=(t // tm,),
        in_specs=[
            pl.BlockSpec((tm, d), lambda i: (i, 0)),
            _const_spec((1, d)),
            _const_spec(w_in.shape),
            _const_spec(conv_w.shape),
            _const_spec(alog16.shape),
            _const_spec(dtb16.shape),
        ],
        out_specs=[row_spec, row_spec, row_spec, row_spec,
                   pl.BlockSpec((2 * HEADS, tm), lambda i: (0, i))],
        out_shape=[act, act, act, act, jax.ShapeDtypeStruct((2 * HEADS, t), F32)],
        scratch_shapes=[pltpu.VMEM((tm, d), BF16), pltpu.VMEM((SUBLANES, 3 * width), F32)],
        compiler_params=pltpu.CompilerParams(dimension_semantics=("arbitrary",),
                                             vmem_limit_bytes=VMEM_LIMIT),
        name="gdn_in",
    )(x, norm_w, w_in, conv_w, alog16, dtb16)


def _delta_kernel(q_ref, k_ref, v_ref, z_ref, beta_ref, g_ref, onw_ref, o_ref, gc_ref,
                  qw_a, b_a, op_a, cd_a, qw_b, b_b, op_b, cd_b, *, chunk, group, heads):
    c_len = chunk
    n_groups = q_ref.shape[0] // (c_len * group)
    bufs = ((qw_a, b_a, op_a, cd_a), (qw_b, b_b, op_b, cd_b))
    row = lax.broadcasted_iota(jnp.int32, (c_len, c_len), 0)
    col = lax.broadcasted_iota(jnp.int32, (c_len, c_len), 1)
    causal = row >= col
    strict = row > col
    eye = row == col
    upper = (row <= col).astype(F32)
    for hh in range(heads):
        gc_ref[hh] = jnp.dot(g_ref[hh], upper, precision=lax.Precision.HIGHEST,
                             preferred_element_type=F32)
    onw = onw_ref[...]
    n_steps = int(math.log2(c_len))
    chains = [(g, hh) for g in range(group) for hh in range(heads)]

    def to_col(r):
        return jnp.sum(jnp.where(eye, r, 0.0), axis=1, keepdims=True)

    def chunk_rows(c):
        return pl.ds(pl.multiple_of(c * c_len, c_len), c_len)

    def head_cols(hh):
        return slice(hh * HEAD_DIM, (hh + 1) * HEAD_DIM)

    def unit_rows(a, scale):
        return a * (lax.rsqrt(jnp.sum(a * a, axis=-1, keepdims=True) + EPS) * scale)

    def run(prep, adv, st):
        todo = list(range(group)) if adv is not None else []

        def advance_one(st):
            if not todo:
                return st
            g = todo.pop(0)
            grp, (qw_ref, b_ref, op_ref, cd_ref) = adv
            rows = chunk_rows(grp * group + g)
            slots = [g * heads + hh for hh in range(heads)]
            res = [_dot(qw_ref[slots[hh]], st[hh].astype(BF16)) for hh in range(heads)]
            out = [res[hh][:c_len] + op_ref[slots[hh]] for hh in range(heads)]
            st = [cd_ref[slots[hh], 0:1, :] * st[hh] - res[hh][c_len:] + b_ref[slots[hh]]
                  for hh in range(heads)]
            for hh in range(heads):
                zc = z_ref[rows, head_cols(hh)]
                o_ref[rows, head_cols(hh)] = (_rms_hat(out[hh]) * onw * _silu(zc)).astype(o_ref.dtype)
            return st

        if prep is not None:
            grp, (qw_ref, b_ref, op_ref, cd_ref) = prep
            n = len(chains)
            cidx = [grp * group + g for g, _ in chains]
            q = [unit_rows(_silu(q_ref[chunk_rows(cidx[i]), head_cols(chains[i][1])]),
                           HEAD_DIM ** -0.5) for i in range(n)]
            k = [unit_rows(_silu(k_ref[chunk_rows(cidx[i]), head_cols(chains[i][1])]), 1.0)
                 for i in range(n)]
            gc_r = [gc_ref[chains[i][1], pl.ds(cidx[i], 1), :] for i in range(n)]
            beta_c = [to_col(beta_ref[chains[i][1], pl.ds(cidx[i], 1), :]) for i in range(n)]
            gc_c = [to_col(r) for r in gc_r]
            gc_last = [r[:, c_len - 1:c_len] for r in gc_r]
            e_c = [jnp.exp(x) for x in gc_c]
            kb = [k[i] * beta_c[i] for i in range(n)]
            s = [lax.dot_general(jnp.concatenate([kb[i], q[i]], axis=0).astype(BF16),
                                 k[i].astype(BF16), NT_DIMS, preferred_element_type=F32)
                 for i in range(n)]
            st = advance_one(st)
            decay = [jnp.where(causal, jnp.exp(jnp.where(causal, gc_c[i] - gc_r[i], 0.0)), 0.0)
                     for i in range(n)]
            p = [jnp.where(strict, -(s[i][:c_len] * decay[i]), 0.0) for i in range(n)]
            intra = [s[i][c_len:] * decay[i] for i in range(n)]
            r = p
            p = [_dot(x.astype(BF16), x.astype(BF16)) for x in p]
            st = advance_one(st)
            for step in range(1, n_steps):
                last = step + 1 == n_steps
                p16 = [x.astype(BF16) for x in p]
                r16 = [x.astype(BF16) for x in r]
                t = [_dot(r16[i] if last else jnp.concatenate([p16[i], r16[i]], axis=0), p16[i])
                     for i in range(n)]
                r = [r[i] + p[i] + (t[i] if last else t[i][c_len:]) for i in range(n)]
                if not last:
                    p = [x[:c_len] for x in t]
                    st = advance_one(st)
            rhs = [jnp.concatenate([_silu(v_ref[chunk_rows(cidx[i]), head_cols(chains[i][1])]) * beta_c[i],
                                    kb[i] * e_c[i]], axis=1) for i in range(n)]
            y = [rhs[i] + _dot(r[i].astype(BF16), rhs[i].astype(BF16)) for i in range(n)]
            st = advance_one(st)
            y16 = [x.astype(BF16) for x in y]
            kd16 = [(k[i] * jnp.exp(gc_last[i] - gc_c[i])).astype(BF16) for i in range(n)]
            mb = [lax.dot_general(kd16[i], y16[i], TN_DIMS, preferred_element_type=F32)
                  for i in range(n)]
            iu = [_dot(intra[i].astype(BF16), y16[i]) for i in range(n)]
            st = advance_one(st)
            for i in range(n):
                b_ref[i] = mb[i][:, :HEAD_DIM]
                qw_ref[i, 0:c_len, :] = (q[i] * e_c[i] - iu[i][:, HEAD_DIM:]).astype(BF16)
                qw_ref[i, c_len:, :] = mb[i][:, HEAD_DIM:].astype(BF16)
                op_ref[i] = iu[i][:, :HEAD_DIM]
                cd_ref[i] = jnp.broadcast_to(jnp.exp(gc_last[i]), (SUBLANES, LANES))
        while todo:
            st = advance_one(st)
        return st

    st = run((0, bufs[0]), None, [jnp.zeros((HEAD_DIM, HEAD_DIM), F32) for _ in range(heads)])

    def body(i, st):
        st = run((2 * i + 1, bufs[1]), (2 * i, bufs[0]), st)
        return run((2 * i + 2, bufs[0]), (2 * i + 1, bufs[1]), st)

    st = lax.fori_loop(0, n_groups // 2 - 1, body, st)
    st = run((n_groups - 1, bufs[1]), (n_groups - 2, bufs[0]), st)
    run(None, (n_groups - 1, bufs[1]), st)


def _delta(q, k, v, z, gates, out_norm, batch, seq_len):
    t, width = q.shape
    c_len = DELTA_CHUNK
    nh = DELTA_HEADS
    n_chunks = seq_len // c_len
    assert n_chunks % (2 * DELTA_GROUP) == 0 and HEADS % nh == 0
    gates3 = gates.reshape(2 * HEADS, batch * n_chunks, c_len)
    seq_spec = pl.BlockSpec((seq_len, nh * HEAD_DIM), lambda b, h: (b, h))
    slots = DELTA_GROUP * nh
    buf_set = [pltpu.VMEM((slots, c_len + HEAD_DIM, HEAD_DIM), BF16),
               pltpu.VMEM((slots, HEAD_DIM, HEAD_DIM), F32),
               pltpu.VMEM((slots, c_len, HEAD_DIM), F32),
               pltpu.VMEM((slots, SUBLANES, LANES), F32)]
    return pl.pallas_call(
        functools.partial(_delta_kernel, chunk=c_len, group=DELTA_GROUP, heads=nh),
        grid=(batch, HEADS // nh),
        in_specs=[seq_spec, seq_spec, seq_spec, seq_spec,
                  pl.BlockSpec((nh, n_chunks, c_len), lambda b, h: (h, b, 0)),
                  pl.BlockSpec((nh, n_chunks, c_len), lambda b, h: (HEADS // nh + h, b, 0)),
                  pl.BlockSpec((1, HEAD_DIM), lambda b, h: (0, 0))],
        out_specs=seq_spec,
        out_shape=jax.ShapeDtypeStruct((t, width), BF16),
        scratch_shapes=[pltpu.VMEM((nh, n_chunks, c_len), F32)] + buf_set + buf_set,
        compiler_params=pltpu.CompilerParams(dimension_semantics=("parallel", "parallel"),
                                             vmem_limit_bytes=VMEM_LIMIT),
        name="delta",
    )(q, k, v, z, gates3, gates3, out_norm)


def _proj_mlp_kernel(o_ref, x_ref, wo_ref, nw_ref, w1_ref, w2_ref, out_ref):
    x1 = x_ref[...] + _dot(o_ref[...].astype(BF16), wo_ref[...])
    xn = (_rms_hat(x1) * nw_ref[...]).astype(BF16)
    acc = x1
    for j in range(D_FF // FF_CHUNK):
        h = jnp.maximum(_dot(xn, w1_ref[:, j * FF_CHUNK:(j + 1) * FF_CHUNK]), 0.0)
        acc = acc + _dot((h * h).astype(BF16), w2_ref[j * FF_CHUNK:(j + 1) * FF_CHUNK, :])
    out_ref[...] = acc


def _layer_spec(stacked, layer):
    return pl.BlockSpec((None,) + stacked.shape[1:], lambda *_: (layer, 0, 0),
                        pipeline_mode=pl.Buffered(1))


def _proj_mlp(o, x, w_out, norm_w, w1, w2, layer):
    t, d = x.shape
    tm = ROW_TILE
    return pl.pallas_call(
        _proj_mlp_kernel,
        grid=(t // tm,),
        in_specs=[pl.BlockSpec((tm, o.shape[1]), lambda i: (i, 0)),
                  pl.BlockSpec((tm, d), lambda i: (i, 0)),
                  _const_spec(w_out.shape), _const_spec((1, d)),
                  _layer_spec(w1, layer), _layer_spec(w2, layer)],
        out_specs=pl.BlockSpec((tm, d), lambda i: (i, 0)),
        out_shape=jax.ShapeDtypeStruct((t, d), F32),
        compiler_params=pltpu.CompilerParams(dimension_semantics=("parallel",),
                                             vmem_limit_bytes=VMEM_LIMIT),
        name="proj_mlp",
    )(o, x, w_out, norm_w, w1, w2)


def _rope_tab_kernel(pos_ref, freq_ref, cos_ref, s1_ref, s2_ref):
    ang = pos_ref[...].astype(F32) * freq_ref[...]
    sin = jnp.sin(ang)
    lane = lax.broadcasted_iota(jnp.int32, ang.shape, 1)
    first_half = (lane % MAP_DIM) < ROPE_HALF
    cos_ref[...] = jnp.cos(ang)
    s1_ref[...] = jnp.where(first_half, -sin, 0.0)
    s2_ref[...] = jnp.where(first_half, 0.0, sin)


def _rope_tab(pos_col, freq_row):
    t = pos_col.shape[0]
    tm = 2048
    tab = jax.ShapeDtypeStruct((t, LANES), F32)
    spec = pl.BlockSpec((tm, LANES), lambda i: (i, 0))
    return pl.pallas_call(
        _rope_tab_kernel,
        grid=(t // tm,),
        in_specs=[pl.BlockSpec((tm, 1), lambda i: (i, 0)), _const_spec((1, LANES))],
        out_specs=[spec, spec, spec],
        out_shape=[tab, tab, tab],
        compiler_params=pltpu.CompilerParams(dimension_semantics=("parallel",)),
        name="rope_tab",
    )(pos_col, freq_row)


def _attn_in_kernel(x_ref, kvn_ref, qnw_ref, wkv_ref, wq_ref, kg_ref, qg_ref,
                    cos_ref, s1_ref, s2_ref, k_ref, v_ref, q_ref, kvx_ref, qx_ref):
    width = HEADS * HEAD_DIM
    cb = 2 * MXU_DIM
    xhat = _rms_hat(x_ref[...])
    kvx_ref[...] = (xhat * kvn_ref[...]).astype(BF16)
    qx_ref[...] = (xhat * qnw_ref[...]).astype(BF16)
    cos = cos_ref[...]
    s1 = s1_ref[...]
    s2 = s2_ref[...]
    r = lax.broadcasted_iota(jnp.int32, (MXU_DIM, MXU_DIM), 0) // MAP_DIM
    c = lax.broadcasted_iota(jnp.int32, (MXU_DIM, MXU_DIM), 1) // MAP_DIM
    group_ones = (r == c).astype(BF16)

    def norm_rope(raw, gain, scale, o_ref, col):
        for s in range(raw.shape[1] // MXU_DIM):
            blk = raw[:, s * MXU_DIM:(s + 1) * MXU_DIM]
            ss = _dot((blk * blk).astype(BF16), group_ones)
            lo = col + s * MXU_DIM
            nb = blk * lax.rsqrt(ss * (1.0 / MAP_DIM) + EPS) * gain[:, lo:lo + MXU_DIM]
            for hh in range(MXU_DIM // LANES):
                xb = nb[:, hh * LANES:(hh + 1) * LANES]
                rot = xb * cos + pltpu.roll(xb, LANES - ROPE_HALF, 1) * s1 + pltpu.roll(xb, ROPE_HALF, 1) * s2
                o_ref[:, lo + hh * LANES:lo + (hh + 1) * LANES] = (rot * scale).astype(o_ref.dtype)

    plan = []
    for col in range(0, width, cb):
        plan += [("k", col), ("q", col)]
    plan += [("v", col) for col in range(0, width, cb)]

    def project(item):
        kind, col = item
        if kind == "q":
            return _dot(qx_ref[...], wq_ref[:, col:col + cb])
        base = 0 if kind == "k" else width
        return _dot(kvx_ref[...], wkv_ref[:, base + col:base + col + cb])

    raw = project(plan[0])
    for i, (kind, col) in enumerate(plan):
        raw_next = project(plan[i + 1]) if i + 1 < len(plan) else None
        if kind == "v":
            v_ref[:, col:col + cb] = raw.astype(v_ref.dtype)
        elif kind == "k":
            norm_rope(raw, kg_ref[...], 1.0, k_ref, col)
        else:
            norm_rope(raw, qg_ref[...], MAP_DIM ** -0.5 * LOG2E, q_ref, col)
        raw = raw_next


def _attn_in(x, kv_norm, q_norm_w, w_kv, w_q, k_gain, q_gain, cos, s1, s2):
    t, d = x.shape
    tm = ROW_TILE
    width = HEADS * HEAD_DIM
    act = jax.ShapeDtypeStruct((t, width), BF16)
    row_spec = pl.BlockSpec((tm, width), lambda i: (i, 0))
    tab_spec = pl.BlockSpec((tm, LANES), lambda i: (i, 0))
    return pl.pallas_call(
        _attn_in_kernel,
        grid=(t // tm,),
        in_specs=[pl.BlockSpec((tm, d), lambda i: (i, 0)),
                  _const_spec((1, d)), _const_spec((1, d)),
                  _const_spec(w_kv.shape), _const_spec(w_q.shape),
                  _const_spec((1, width)), _const_spec((1, width)),
                  tab_spec, tab_spec, tab_spec],
        out_specs=[row_spec, row_spec, row_spec],
        out_shape=[act, act, act],
        scratch_shapes=[pltpu.VMEM((tm, d), BF16), pltpu.VMEM((tm, d), BF16)],
        compiler_params=pltpu.CompilerParams(dimension_semantics=("parallel",),
                                             vmem_limit_bytes=VMEM_LIMIT),
        name="attn_in",
    )(x, kv_norm, q_norm_w, w_kv, w_q, k_gain, q_gain, cos, s1, s2)


def _diff_attn_kernel(q_ref, k_ref, v_ref, qg_ref, kg_ref, lam_ref, snw_ref, o_ref,
                      m_ref, l_ref, acc_ref, *, tq, lam_init):
    n_q = q_ref.shape[0] // tq
    piece = ATTN_DIAG_PIECE
    lane = lax.broadcasted_iota(jnp.int32, (tq, HEAD_DIM), 1)
    lp = lam_ref[...]
    lam = (jnp.exp(jnp.sum(lp[0:1] * lp[1:2], axis=-1, keepdims=True))
           - jnp.exp(jnp.sum(lp[2:3] * lp[3:4], axis=-1, keepdims=True)) + lam_init)
    snw = snw_ref[...]

    def q_maps(rows):
        q = q_ref[rows, :]
        zero = jnp.zeros_like(q)
        return (jnp.where(lane < MAP_DIM, q, zero), jnp.where(lane < MAP_DIM, zero, q))

    def scores(qm, rows):
        ks = k_ref[rows, :]
        return [lax.dot_general(x, ks, NT_DIMS, preferred_element_type=F32) for x in qm]

    def finish(rows, acc0, acc1, l0, l1):
        o = acc0 / l0 - lam * (acc1 / l1)
        o_ref[rows, :] = (_rms_hat(o) * snw * (1.0 - lam_init)).astype(o_ref.dtype)

    bound = SCORE_BOUND_COEF * jnp.max(jnp.abs(qg_ref[...])) * jnp.max(jnp.abs(kg_ref[...]))
    bounded = bound <= SCORE_BOUND_LIMIT

    @pl.when(bounded)
    def _():
        def lane_sums(p):
            out = p[:, 0:LANES]
            for b in range(1, p.shape[1] // LANES):
                out = out + p[:, b * LANES:(b + 1) * LANES]
            return out

        def plus(a, b):
            return b if a is None else a + b

        for qi in range(n_q):
            q0 = qi * tq
            qm = q_maps(slice(q0, q0 + tq))
            acc = [None, None]
            lsum = [None, None]
            s_next = scores(qm, slice(0, tq)) if qi > 0 else None
            for j in range(qi):
                s = s_next
                s_next = scores(qm, slice((j + 1) * tq, (j + 2) * tq)) if j + 1 < qi else None
                vs = v_ref[j * tq:(j + 1) * tq, :]
                for mi in range(2):
                    p = jnp.exp2(s[mi])
                    lsum[mi] = plus(lsum[mi], lane_sums(p))
                    acc[mi] = plus(acc[mi], _dot(p.astype(BF16), vs))
            for bi in range(tq // piece):
                rows = slice(bi * piece, (bi + 1) * piece)
                ncols = (bi + 1) * piece
                sd = scores([x[rows, :] for x in qm], slice(q0, q0 + ncols))
                r = lax.broadcasted_iota(jnp.int32, (piece, ncols), 0) + bi * piece
                c = lax.broadcasted_iota(jnp.int32, (piece, ncols), 1)
                vs = v_ref[q0:q0 + ncols, :]
                fin = []
                for mi in range(2):
                    p = jnp.where(r >= c, jnp.exp2(sd[mi]), 0.0)
                    below_l = None if lsum[mi] is None else lsum[mi][rows, :]
                    below_a = None if acc[mi] is None else acc[mi][rows, :]
                    fin.append((plus(below_a, _dot(p.astype(BF16), vs)),
                                jnp.sum(plus(below_l, lane_sums(p)), axis=-1, keepdims=True)))
                finish(slice(q0 + bi * piece, q0 + (bi + 1) * piece),
                       fin[0][0], fin[1][0], fin[0][1], fin[1][1])

    @pl.when(jnp.logical_not(bounded))
    def _():
        r = lax.broadcasted_iota(jnp.int32, (tq, tq), 0)
        c = lax.broadcasted_iota(jnp.int32, (tq, tq), 1)

        def q_tile(qi, carry):
            q_rows = pl.ds(pl.multiple_of(qi * tq, tq), tq)
            qm = q_maps(q_rows)
            m_ref[...] = jnp.full(m_ref.shape, NEG_INF, F32)
            l_ref[...] = jnp.zeros(l_ref.shape, F32)
            acc_ref[...] = jnp.zeros(acc_ref.shape, F32)

            def step(j, masked):
                kv_rows = pl.ds(pl.multiple_of(j * tq, tq), tq)
                s = scores(qm, kv_rows)
                vs = v_ref[kv_rows, :]
                for mi in range(2):
                    sm = jnp.where(r >= c, s[mi], NEG_INF) if masked else s[mi]
                    m_old = m_ref[mi]
                    m_new = jnp.maximum(m_old, jnp.max(sm, axis=-1, keepdims=True))
                    alpha = jnp.exp2(m_old - m_new)
                    p = jnp.exp2(sm - m_new[:, 0:1])
                    l_ref[mi] = alpha * l_ref[mi] + jnp.sum(p, axis=-1, keepdims=True)
                    acc_ref[mi] = alpha * acc_ref[mi] + _dot(p.astype(BF16), vs)
                    m_ref[mi] = m_new

            def full_body(j, carry):
                step(j, False)
                return carry

            lax.fori_loop(0, qi, full_body, 0)
            step(qi, True)
            finish(q_rows, acc_ref[0], acc_ref[1], l_ref[0], l_ref[1])
            return carry

        lax.fori_loop(0, n_q, q_tile, 0)


def _diff_attn(q, k, v, q_gain, k_gain, lam_params, sub_norm, batch, seq_len, lam_init):
    t, width = q.shape
    tq = ATTN_TILE
    seq_spec = pl.BlockSpec((seq_len, HEAD_DIM), lambda b, h: (b, h))
    return pl.pallas_call(
        functools.partial(_diff_attn_kernel, tq=tq, lam_init=lam_init),
        grid=(batch, HEADS),
        in_specs=[seq_spec, seq_spec, seq_spec,
                  _const_spec(q_gain.shape), _const_spec(k_gain.shape),
                  _const_spec(lam_params.shape), _const_spec((1, HEAD_DIM))],
        out_specs=seq_spec,
        out_shape=jax.ShapeDtypeStruct((t, width), BF16),
        scratch_shapes=[pltpu.VMEM((2, tq, LANES), F32), pltpu.VMEM((2, tq, LANES), F32),
                        pltpu.VMEM((2, tq, HEAD_DIM), F32)],
        compiler_params=pltpu.CompilerParams(dimension_semantics=("parallel", "parallel"),
                                             vmem_limit_bytes=VMEM_LIMIT),
        name="diff_attn",
    )(q, k, v, q_gain, k_gain, lam_params, sub_norm)


def kernel(x, positions, a_norm, a_w_in, a_conv_w, a_a_log, a_dt_bias, a_out_norm, a_w_out,
           kv_norm, w_kv, k_norm, b_norm, b_w_q, b_q_norm, b_lambda, b_sub_norm, b_w_out,
           mlp_norm, mlp_w1, mlp_w2):
    batch, seq_len, d = x.shape
    assert d == D_MODEL and a_norm.shape[0] == 1 and b_norm.shape[0] == 1
    assert seq_len % ROW_TILE == 0 and seq_len % ATTN_TILE == 0 and seq_len % DELTA_CHUNK == 0
    t = batch * seq_len
    width = HEADS * HEAD_DIM
    xf = x.reshape(t, d)

    w_in = a_w_in[0]
    pad = jnp.zeros((HEADS, 1), F32)
    alog16 = jnp.concatenate([pad, a_a_log[0].reshape(HEADS, 1)], axis=0)
    dtb16 = jnp.concatenate([pad, a_dt_bias[0].reshape(HEADS, 1)], axis=0)
    q, k, v, z, gates = _gdn_in(xf, a_norm[0].reshape(1, d), w_in.astype(BF16), a_conv_w[0],
                                alog16, dtb16, seq_len)
    o = _delta(q, k, v, z, gates, a_out_norm[0].reshape(1, HEAD_DIM), batch, seq_len)
    w1_all = mlp_w1.astype(BF16)
    w2_all = mlp_w2.astype(BF16)
    xf = _proj_mlp(o, xf, a_w_out[0].astype(BF16), mlp_norm[0].reshape(1, d), w1_all, w2_all, 0)

    half = ROPE_HALF
    freqs = ROPE_THETA ** (-jnp.arange(half, dtype=F32) / half)
    freq_row = jnp.tile(freqs, LANES // half).reshape(1, LANES)
    cos, s1, s2 = _rope_tab(positions.reshape(t, 1), freq_row)
    k_gain = jnp.tile(k_norm, width // MAP_DIM).reshape(1, width)
    q_gain = jnp.tile(b_q_norm[0], width // MAP_DIM).reshape(1, width)
    kr, vv, qr = _attn_in(xf, kv_norm.reshape(1, d), b_norm[0].reshape(1, d),
                          w_kv.astype(BF16), b_w_q[0].astype(BF16), k_gain, q_gain, cos, s1, s2)
    lam_init = 0.8 - 0.6 * math.exp(-0.3 * 1)
    oa = _diff_attn(qr, kr, vv, b_q_norm[0].reshape(1, MAP_DIM), k_norm.reshape(1, MAP_DIM),
                    b_lambda[0], b_sub_norm[0].reshape(1, HEAD_DIM), batch, seq_len, lam_init)
    xf = _proj_mlp(oa, xf, b_w_out[0].astype(BF16), mlp_norm[1].reshape(1, d), w1_all, w2_all, 1)
    return xf.reshape(batch, seq_len, d)
```

```python
import functools
import math

import jax
import jax.numpy as jnp
from jax import lax
from jax.experimental import pallas as pl
from jax.experimental.pallas import tpu as pltpu

F32 = jnp.float32
BF16 = jnp.bfloat16

D_MODEL = 1024
HEADS = 8
HEAD_DIM = 128
MAP_DIM = 64
ROPE_HALF = MAP_DIM // 2
CONV_WIDTH = 4
D_FF = 4 * D_MODEL
ROPE_THETA = 10000.0
EPS = 1e-6
NEG_INF = -1e30
LOG2E = math.log2(math.e)
SCORE_BOUND_COEF = MAP_DIM * MAP_DIM ** -0.5 * LOG2E * 1.02
SCORE_BOUND_LIMIT = 100.0

LANES = 128
SUBLANES = 8
MXU_DIM = 256

ROW_TILE = 512
GDN_ROW_TILE = 512
GDN_COL_BLOCK = 2 * MXU_DIM
FF_CHUNK = 1024
DELTA_CHUNK = 64
DELTA_GROUPS = (4, 8, 8, 8, 4)
DELTA_HEADS = 2
ATTN_TILE = 512
ATTN_DIAG_PIECE = 256
VMEM_LIMIT = 56 * 1024 * 1024

NT_DIMS = (((1,), (1,)), ((), ()))
TN_DIMS = (((0,), (0,)), ((), ()))


def _rms_hat(x):
    return x * lax.rsqrt(jnp.mean(x * x, axis=-1, keepdims=True) + EPS)


def _sigmoid(x):
    return 1.0 / (1.0 + jnp.exp(-x))


def _silu(x):
    h = 0.5 * x
    return h * jnp.tanh(h) + h


def _softplus(x):
    return jnp.maximum(x, 0.0) + jnp.log(1.0 + jnp.exp(-jnp.abs(x)))


def _dot(a, b):
    return jnp.dot(a, b, preferred_element_type=F32)


def _const_spec(shape):
    zeros = (0,) * len(shape)
    return pl.BlockSpec(shape, lambda *_: zeros, pipeline_mode=pl.Buffered(1))


def _gdn_in_kernel(x_ref, nw_ref, w_ref, cw_ref, alog_ref, dtb_ref,
                   q_ref, k_ref, v_ref, z_ref, gate_ref, xn_ref, tail_ref, *, tiles_per_seq):
    tm = x_ref.shape[0]
    width = HEADS * HEAD_DIM
    step = pl.program_id(0)

    @pl.when(step == 0)
    def _():
        tail_ref[...] = jnp.zeros(tail_ref.shape, F32)

    xn_ref[...] = (_rms_hat(x_ref[...]) * nw_ref[...]).astype(BF16)
    seq_start = (step % tiles_per_seq) == 0

    cb = GDN_COL_BLOCK
    blocks = 3 * width // cb

    def project(blk):
        return _dot(xn_ref[...], w_ref[:, blk * cb:(blk + 1) * cb])

    p = project(0)
    for blk in range(blocks):
        p_next = project(blk + 1) if blk + 1 < blocks else None
        cols = slice(blk * cb, (blk + 1) * cb)
        prev = jnp.where(seq_start, 0.0, tail_ref[:, cols])
        tail_ref[:, cols] = p[tm - SUBLANES:, :]
        xp = jnp.concatenate([prev, p], axis=0)
        cw = cw_ref[:, cols]
        c = cw[CONV_WIDTH - 1:CONV_WIDTH, :] * p
        for j in range(CONV_WIDTH - 1):
            c = c + cw[j:j + 1, :] * pltpu.roll(xp, CONV_WIDTH - 1 - j, 0)[SUBLANES:, :]
        kind, col = divmod(blk * cb, width)
        (q_ref, k_ref, v_ref)[kind][:, col:col + cb] = c
        p = p_next

    z_ref[...] = _dot(xn_ref[...], w_ref[:, 3 * width:4 * width])

    g_rows = _dot(xn_ref[...], w_ref[:, 4 * width:])
    eye_g = (lax.broadcasted_iota(jnp.int32, (2 * HEADS, 2 * HEADS), 0)
             == lax.broadcasted_iota(jnp.int32, (2 * HEADS, 2 * HEADS), 1)).astype(F32)
    gt = lax.dot_general(eye_g, g_rows, NT_DIMS, precision=lax.Precision.HIGHEST,
                         preferred_element_type=F32)
    beta = _sigmoid(gt)
    decay = -jnp.exp(alog_ref[...]) * _softplus(gt + dtb_ref[...])
    row = lax.broadcasted_iota(jnp.int32, gt.shape, 0)
    gate_ref[...] = jnp.where(row < HEADS, beta, decay)


def _gdn_in(x, norm_w, w_in, conv_w, alog16, dtb16, seq_len):
    t, d = x.shape
    tm = GDN_ROW_TILE
    width = HEADS * HEAD_DIM
    act = jax.ShapeDtypeStruct((t, width), F32)
    row_spec = pl.BlockSpec((tm, width), lambda i: (i, 0))
    return pl.pallas_call(
        functools.partial(_gdn_in_kernel, tiles_per_seq=seq_len // tm),
        grid=(t // tm,),
        in_specs=[
            pl.BlockSpec((tm, d), lambda i: (i, 0)),
            _const_spec((1, d)),
            _const_spec(w_in.shape),
            _const_spec(conv_w.shape),
            _const_spec(alog16.shape),
            _const_spec(dtb16.shape),
        ],
        out_specs=[row_spec, row_spec, row_spec, row_spec,
                   pl.BlockSpec((2 * HEADS, tm), lambda i: (0, i))],
        out_shape=[act, act, act, act, jax.ShapeDtypeStruct((2 * HEADS, t), F32)],
        scratch_shapes=[pltpu.VMEM((tm, d), BF16), pltpu.VMEM((SUBLANES, 3 * width), F32)],
        compiler_params=pltpu.CompilerParams(dimension_semantics=("arbitrary",),
                                             vmem_limit_bytes=VMEM_LIMIT),
        name="gdn_in",
    )(x, norm_w, w_in, conv_w, alog16, dtb16)


def _delta_kernel(q_ref, k_ref, v_ref, z_ref, beta_ref, g_ref, onw_ref, o_ref, gc_ref,
                  qw_a, b_a, op_a, cd_a, qw_b, b_b, op_b, cd_b, *, chunk, groups, heads):
    c_len = chunk
    bufs = ((qw_a, b_a, op_a, cd_a), (qw_b, b_b, op_b, cd_b))
    row = lax.broadcasted_iota(jnp.int32, (c_len, c_len), 0)
    col = lax.broadcasted_iota(jnp.int32, (c_len, c_len), 1)
    causal = row >= col
    strict = row > col
    eye = row == col
    upper = (row <= col).astype(F32)
    for hh in range(heads):
        gc_ref[hh] = jnp.dot(g_ref[hh], upper, precision=lax.Precision.HIGHEST,
                             preferred_element_type=F32)
    onw = onw_ref[...]
    n_steps = int(math.log2(c_len))

    def to_col(r):
        return jnp.sum(jnp.where(eye, r, 0.0), axis=1, keepdims=True)

    def chunk_rows(c):
        return slice(c * c_len, (c + 1) * c_len)

    def head_cols(hh):
        return slice(hh * HEAD_DIM, (hh + 1) * HEAD_DIM)

    def unit_rows(a, scale):
        return a * (lax.rsqrt(jnp.sum(a * a, axis=-1, keepdims=True) + EPS) * scale)

    def run(prep, adv, st):
        todo = list(range(adv[1])) if adv is not None else []

        def advance_one(st):
            if not todo:
                return st
            g = todo.pop(0)
            first, _, (qw_ref, b_ref, op_ref, cd_ref) = adv
            rows = chunk_rows(first + g)
            slots = [g * heads + hh for hh in range(heads)]
            res = [_dot(qw_ref[slots[hh]], st[hh].astype(BF16)) for hh in range(heads)]
            out = [res[hh][:c_len] + op_ref[slots[hh]] for hh in range(heads)]
            st = [cd_ref[slots[hh], 0:1, :] * st[hh] - res[hh][c_len:] + b_ref[slots[hh]]
                  for hh in range(heads)]
            for hh in range(heads):
                zc = z_ref[rows, head_cols(hh)]
                o_ref[rows, head_cols(hh)] = (_rms_hat(out[hh]) * onw * _silu(zc)).astype(o_ref.dtype)
            return st

        if prep is not None:
            first, size, (qw_ref, b_ref, op_ref, cd_ref) = prep
            chains = [(g, hh) for g in range(size) for hh in range(heads)]
            n = len(chains)
            cidx = [first + g for g, _ in chains]
            q = [unit_rows(_silu(q_ref[chunk_rows(cidx[i]), head_cols(chains[i][1])]),
                           HEAD_DIM ** -0.5) for i in range(n)]
            k = [unit_rows(_silu(k_ref[chunk_rows(cidx[i]), head_cols(chains[i][1])]), 1.0)
                 for i in range(n)]
            gc_r = [gc_ref[chains[i][1], pl.ds(cidx[i], 1), :] for i in range(n)]
            beta_c = [to_col(beta_ref[chains[i][1], pl.ds(cidx[i], 1), :]) for i in range(n)]
            gc_c = [to_col(r) for r in gc_r]
            gc_last = [r[:, c_len - 1:c_len] for r in gc_r]
            e_c = [jnp.exp(x) for x in gc_c]
            kb = [k[i] * beta_c[i] for i in range(n)]
            s = [lax.dot_general(jnp.concatenate([kb[i], q[i]], axis=0).astype(BF16),
                                 k[i].astype(BF16), NT_DIMS, preferred_element_type=F32)
                 for i in range(n)]
            st = advance_one(st)
            decay = [jnp.where(causal, jnp.exp(jnp.where(causal, gc_c[i] - gc_r[i], 0.0)), 0.0)
                     for i in range(n)]
            p = [jnp.where(strict, -(s[i][:c_len] * decay[i]), 0.0) for i in range(n)]
            intra = [s[i][c_len:] * decay[i] for i in range(n)]
            r = p
            p = [_dot(x.astype(BF16), x.astype(BF16)) for x in p]
            st = advance_one(st)
            for step in range(1, n_steps):
                last = step + 1 == n_steps
                p16 = [x.astype(BF16) for x in p]
                r16 = [x.astype(BF16) for x in r]
                t = [_dot(r16[i] if last else jnp.concatenate([p16[i], r16[i]], axis=0), p16[i])
                     for i in range(n)]
                r = [r[i] + p[i] + (t[i] if last else t[i][c_len:]) for i in range(n)]
                if not last:
                    p = [x[:c_len] for x in t]
                    st = advance_one(st)
            rhs = [jnp.concatenate([_silu(v_ref[chunk_rows(cidx[i]), head_cols(chains[i][1])]) * beta_c[i],
                                    kb[i] * e_c[i]], axis=1) for i in range(n)]
            y = [rhs[i] + _dot(r[i].astype(BF16), rhs[i].astype(BF16)) for i in range(n)]
            st = advance_one(st)
            y16 = [x.astype(BF16) for x in y]
            kd16 = [(k[i] * jnp.exp(gc_last[i] - gc_c[i])).astype(BF16) for i in range(n)]
            mb = [lax.dot_general(kd16[i], y16[i], TN_DIMS, preferred_element_type=F32)
                  for i in range(n)]
            iu = [_dot(intra[i].astype(BF16), y16[i]) for i in range(n)]
            st = advance_one(st)
            for i in range(n):
                b_ref[i] = mb[i][:, :HEAD_DIM]
                qw_ref[i, 0:c_len, :] = (q[i] * e_c[i] - iu[i][:, HEAD_DIM:]).astype(BF16)
                qw_ref[i, c_len:, :] = mb[i][:, HEAD_DIM:].astype(BF16)
                op_ref[i] = iu[i][:, :HEAD_DIM]
                cd_ref[i] = jnp.broadcast_to(jnp.exp(gc_last[i]), (SUBLANES, LANES))
        while todo:
            st = advance_one(st)
        return st

    firsts = [sum(groups[:j]) for j in range(len(groups))]
    plan = [(firsts[j], groups[j], bufs[j % 2]) for j in range(len(groups))]
    st = [jnp.zeros((HEAD_DIM, HEAD_DIM), F32) for _ in range(heads)]
    for j in range(len(plan) + 1):
        st = run(plan[j] if j < len(plan) else None, plan[j - 1] if j > 0 else None, st)


def _delta(q, k, v, z, gates, out_norm, batch, seq_len):
    t, width = q.shape
    c_len = DELTA_CHUNK
    nh = DELTA_HEADS
    n_chunks = seq_len // c_len
    assert sum(DELTA_GROUPS) == n_chunks and HEADS % nh == 0
    gates3 = gates.reshape(2 * HEADS, batch * n_chunks, c_len)
    seq_spec = pl.BlockSpec((seq_len, nh * HEAD_DIM), lambda b, h: (b, h))
    slots = max(DELTA_GROUPS) * nh
    buf_set = [pltpu.VMEM((slots, c_len + HEAD_DIM, HEAD_DIM), BF16),
               pltpu.VMEM((slots, HEAD_DIM, HEAD_DIM), F32),
               pltpu.VMEM((slots, c_len, HEAD_DIM), F32),
               pltpu.VMEM((slots, SUBLANES, LANES), F32)]
    return pl.pallas_call(
        functools.partial(_delta_kernel, chunk=c_len, groups=DELTA_GROUPS, heads=nh),
        grid=(batch, HEADS // nh),
        in_specs=[seq_spec, seq_spec, seq_spec, seq_spec,
                  pl.BlockSpec((nh, n_chunks, c_len), lambda b, h: (h, b, 0)),
                  pl.BlockSpec((nh, n_chunks, c_len), lambda b, h: (HEADS // nh + h, b, 0)),
                  pl.BlockSpec((1, HEAD_DIM), lambda b, h: (0, 0))],
        out_specs=seq_spec,
        out_shape=jax.ShapeDtypeStruct((t, width), BF16),
        scratch_shapes=[pltpu.VMEM((nh, n_chunks, c_len), F32)] + buf_set + buf_set,
        compiler_params=pltpu.CompilerParams(dimension_semantics=("parallel", "parallel"),
                                             vmem_limit_bytes=VMEM_LIMIT),
        name="delta",
    )(q, k, v, z, gates3, gates3, out_norm)


def _proj_mlp_kernel(o_ref, x_ref, wo_ref, nw_ref, w1_ref, w2_ref, out_ref):
    x1 = x_ref[...] + _dot(o_ref[...].astype(BF16), wo_ref[...])
    xn = (_rms_hat(x1) * nw_ref[...]).astype(BF16)
    acc = x1
    for j in range(D_FF // FF_CHUNK):
        h = jnp.maximum(_dot(xn, w1_ref[:, j * FF_CHUNK:(j + 1) * FF_CHUNK]), 0.0)
        acc = acc + _dot((h * h).astype(BF16), w2_ref[j * FF_CHUNK:(j + 1) * FF_CHUNK, :])
    out_ref[...] = acc


def _layer_spec(stacked, layer):
    return pl.BlockSpec((None,) + stacked.shape[1:], lambda *_: (layer, 0, 0),
                        pipeline_mode=pl.Buffered(1))


def _proj_mlp(o, x, w_out, norm_w, w1, w2, layer):
    t, d = x.shape
    tm = ROW_TILE
    return pl.pallas_call(
        _proj_mlp_kernel,
        grid=(t // tm,),
        in_specs=[pl.BlockSpec((tm, o.shape[1]), lambda i: (i, 0)),
                  pl.BlockSpec((tm, d), lambda i: (i, 0)),
                  _const_spec(w_out.shape), _const_spec((1, d)),
                  _layer_spec(w1, layer), _layer_spec(w2, layer)],
        out_specs=pl.BlockSpec((tm, d), lambda i: (i, 0)),
        out_shape=jax.ShapeDtypeStruct((t, d), F32),
        compiler_params=pltpu.CompilerParams(dimension_semantics=("parallel",),
                                             vmem_limit_bytes=VMEM_LIMIT),
        name="proj_mlp",
    )(o, x, w_out, norm_w, w1, w2)


def _rope_tab_kernel(pos_ref, freq_ref, cos_ref, s1_ref, s2_ref):
    ang = pos_ref[...].astype(F32) * freq_ref[...]
    sin = jnp.sin(ang)
    lane = lax.broadcasted_iota(jnp.int32, ang.shape, 1)
    first_half = (lane % MAP_DIM) < ROPE_HALF
    cos_ref[...] = jnp.cos(ang)
    s1_ref[...] = jnp.where(first_half, -sin, 0.0)
    s2_ref[...] = jnp.where(first_half, 0.0, sin)


def _rope_tab(pos_col, freq_row):
    t = pos_col.shape[0]
    tm = 2048
    tab = jax.ShapeDtypeStruct((t, LANES), F32)
    spec = pl.BlockSpec((tm, LANES), lambda i: (i, 0))
    return pl.pallas_call(
        _rope_tab_kernel,
        grid=(t // tm,),
        in_specs=[pl.BlockSpec((tm, 1), lambda i: (i, 0)), _const_spec((1, LANES))],
        out_specs=[spec, spec, spec],
        out_shape=[tab, tab, tab],
        compiler_params=pltpu.CompilerParams(dimension_semantics=("parallel",)),
        name="rope_tab",
    )(pos_col, freq_row)


def _attn_in_kernel(x_ref, kvn_ref, qnw_ref, wkv_ref, wq_ref, kg_ref, qg_ref,
                    cos_ref, s1_ref, s2_ref, k_ref, v_ref, q_ref, kvx_ref, qx_ref):
    width = HEADS * HEAD_DIM
    cb = 2 * MXU_DIM
    xhat = _rms_hat(x_ref[...])
    kvx_ref[...] = (xhat * kvn_ref[...]).astype(BF16)
    qx_ref[...] = (xhat * qnw_ref[...]).astype(BF16)
    cos = cos_ref[...]
    s1 = s1_ref[...]
    s2 = s2_ref[...]
    r = lax.broadcasted_iota(jnp.int32, (MXU_DIM, MXU_DIM), 0) // MAP_DIM
    c = lax.broadcasted_iota(jnp.int32, (MXU_DIM, MXU_DIM), 1) // MAP_DIM
    group_ones = (r == c).astype(BF16)

    def norm_rope(raw, gain, scale, o_ref, col):
        for s in range(raw.shape[1] // MXU_DIM):
            blk = raw[:, s * MXU_DIM:(s + 1) * MXU_DIM]
            ss = _dot((blk * blk).astype(BF16), group_ones)
            lo = col + s * MXU_DIM
            nb = blk * lax.rsqrt(ss * (1.0 / MAP_DIM) + EPS) * gain[:, lo:lo + MXU_DIM]
            for hh in range(MXU_DIM // LANES):
                xb = nb[:, hh * LANES:(hh + 1) * LANES]
                rot = xb * cos + pltpu.roll(xb, LANES - ROPE_HALF, 1) * s1 + pltpu.roll(xb, ROPE_HALF, 1) * s2
                o_ref[:, lo + hh * LANES:lo + (hh + 1) * LANES] = (rot * scale).astype(o_ref.dtype)

    plan = []
    for col in range(0, width, cb):
        plan += [("k", col), ("q", col)]
    plan += [("v", col) for col in range(0, width, cb)]

    def project(item):
        kind, col = item
        if kind == "q":
            return _dot(qx_ref[...], wq_ref[:, col:col + cb])
        base = 0 if kind == "k" else width
        return _dot(kvx_ref[...], wkv_ref[:, base + col:base + col + cb])

    raw = project(plan[0])
    for i, (kind, col) in enumerate(plan):
        raw_next = project(plan[i + 1]) if i + 1 < len(plan) else None
        if kind == "v":
            v_ref[:, col:col + cb] = raw.astype(v_ref.dtype)
        elif kind == "k":
            norm_rope(raw, kg_ref[...], 1.0, k_ref, col)
        else:
            norm_rope(raw, qg_ref[...], MAP_DIM ** -0.5 * LOG2E, q_ref, col)
        raw = raw_next


def _attn_in(x, kv_norm, q_norm_w, w_kv, w_q, k_gain, q_gain, cos, s1, s2):
    t, d = x.shape
    tm = ROW_TILE
    width = HEADS * HEAD_DIM
    act = jax.ShapeDtypeStruct((t, width), BF16)
    row_spec = pl.BlockSpec((tm, width), lambda i: (i, 0))
    tab_spec = pl.BlockSpec((tm, LANES), lambda i: (i, 0))
    return pl.pallas_call(
        _attn_in_kernel,
        grid=(t // tm,),
        in_specs=[pl.BlockSpec((tm, d), lambda i: (i, 0)),
                  _const_spec((1, d)), _const_spec((1, d)),
                  _const_spec(w_kv.shape), _const_spec(w_q.shape),
                  _const_spec((1, width)), _const_spec((1, width)),
                  tab_spec, tab_spec, tab_spec],
        out_specs=[row_spec, row_spec, row_spec],
        out_shape=[act, act, act],
        scratch_shapes=[pltpu.VMEM((tm, d), BF16), pltpu.VMEM((tm, d), BF16)],
        compiler_params=pltpu.CompilerParams(dimension_semantics=("parallel",),
                                             vmem_limit_bytes=VMEM_LIMIT),
        name="attn_in",
    )(x, kv_norm, q_norm_w, w_kv, w_q, k_gain, q_gain, cos, s1, s2)


def _diff_attn_kernel(q_ref, k_ref, v_ref, qg_ref, kg_ref, lam_ref, snw_ref, o_ref,
                      m_ref, l_ref, acc_ref, *, tq, lam_init):
    n_q = q_ref.shape[0] // tq
    piece = ATTN_DIAG_PIECE
    lane = lax.broadcasted_iota(jnp.int32, (tq, HEAD_DIM), 1)
    lp = lam_ref[...]
    lam = (jnp.exp(jnp.sum(lp[0:1] * lp[1:2], axis=-1, keepdims=True))
           - jnp.exp(jnp.sum(lp[2:3] * lp[3:4], axis=-1, keepdims=True)) + lam_init)
    snw = snw_ref[...]

    def q_maps(rows):
        q = q_ref[rows, :]
        zero = jnp.zeros_like(q)
        return (jnp.where(lane < MAP_DIM, q, zero), jnp.where(lane < MAP_DIM, zero, q))

    def scores(qm, rows):
        ks = k_ref[rows, :]
        return [lax.dot_general(x, ks, NT_DIMS, preferred_element_type=F32) for x in qm]

    def finish(rows, acc0, acc1, l0, l1):
        o = acc0 / l0 - lam * (acc1 / l1)
        o_ref[rows, :] = (_rms_hat(o) * snw * (1.0 - lam_init)).astype(o_ref.dtype)

    bound = SCORE_BOUND_COEF * jnp.max(jnp.abs(qg_ref[...])) * jnp.max(jnp.abs(kg_ref[...]))
    bounded = bound <= SCORE_BOUND_LIMIT

    @pl.when(bounded)
    def _():
        def lane_sums(p):
            out = p[:, 0:LANES]
            for b in range(1, p.shape[1] // LANES):
                out = out + p[:, b * LANES:(b + 1) * LANES]
            return out

        def plus(a, b):
            return b if a is None else a + b

        for qi in range(n_q):
            q0 = qi * tq
            qm = q_maps(slice(q0, q0 + tq))
            acc = [None, None]
            lsum = [None, None]
            s_next = scores(qm, slice(0, tq)) if qi > 0 else None
            for j in range(qi):
                s = s_next
                s_next = scores(qm, slice((j + 1) * tq, (j + 2) * tq)) if j + 1 < qi else None
                vs = v_ref[j * tq:(j + 1) * tq, :]
                for mi in range(2):
                    p = jnp.exp2(s[mi])
                    lsum[mi] = plus(lsum[mi], lane_sums(p))
                    acc[mi] = plus(acc[mi], _dot(p.astype(BF16), vs))
            for bi in range(tq // piece):
                rows = slice(bi * piece, (bi + 1) * piece)
                ncols = (bi + 1) * piece
                sd = scores([x[rows, :] for x in qm], slice(q0, q0 + ncols))
                r = lax.broadcasted_iota(jnp.int32, (piece, ncols), 0) + bi * piece
                c = lax.broadcasted_iota(jnp.int32, (piece, ncols), 1)
                vs = v_ref[q0:q0 + ncols, :]
                fin = []
                for mi in range(2):
                    p = jnp.where(r >= c, jnp.exp2(sd[mi]), 0.0)
                    below_l = None if lsum[mi] is None else lsum[mi][rows, :]
                    below_a = None if acc[mi] is None else acc[mi][rows, :]
                    fin.append((plus(below_a, _dot(p.astype(BF16), vs)),
                                jnp.sum(plus(below_l, lane_sums(p)), axis=-1, keepdims=True)))
                finish(slice(q0 + bi * piece, q0 + (bi + 1) * piece),
                       fin[0][0], fin[1][0], fin[0][1], fin[1][1])

    @pl.when(jnp.logical_not(bounded))
    def _():
        r = lax.broadcasted_iota(jnp.int32, (tq, tq), 0)
        c = lax.broadcasted_iota(jnp.int32, (tq, tq), 1)

        def q_tile(qi, carry):
            q_rows = pl.ds(pl.multiple_of(qi * tq, tq), tq)
            qm = q_maps(q_rows)
            m_ref[...] = jnp.full(m_ref.shape, NEG_INF, F32)
            l_ref[...] = jnp.zeros(l_ref.shape, F32)
            acc_ref[...] = jnp.zeros(acc_ref.shape, F32)

            def step(j, masked):
                kv_rows = pl.ds(pl.multiple_of(j * tq, tq), tq)
                s = scores(qm, kv_rows)
                vs = v_ref[kv_rows, :]
                for mi in range(2):
                    sm = jnp.where(r >= c, s[mi], NEG_INF) if masked else s[mi]
                    m_old = m_ref[mi]
                    m_new = jnp.maximum(m_old, jnp.max(sm, axis=-1, keepdims=True))
                    alpha = jnp.exp2(m_old - m_new)
                    p = jnp.exp2(sm - m_new[:, 0:1])
                    l_ref[mi] = alpha * l_ref[mi] + jnp.sum(p, axis=-1, keepdims=True)
                    acc_ref[mi] = alpha * acc_ref[mi] + _dot(p.astype(BF16), vs)
                    m_ref[mi] = m_new

            def full_body(j, carry):
                step(j, False)
                return carry

            lax.fori_loop(0, qi, full_body, 0)
            step(qi, True)
            finish(q_rows, acc_ref[0], acc_ref[1], l_ref[0], l_ref[1])
            return carry

        lax.fori_loop(0, n_q, q_tile, 0)


def _diff_attn(q, k, v, q_gain, k_gain, lam_params, sub_norm, batch, seq_len, lam_init):
    t, width = q.shape
    tq = ATTN_TILE
    seq_spec = pl.BlockSpec((seq_len, HEAD_DIM), lambda b, h: (b, h))
    return pl.pallas_call(
        functools.partial(_diff_attn_kernel, tq=tq, lam_init=lam_init),
        grid=(batch, HEADS),
        in_specs=[seq_spec, seq_spec, seq_spec,
                  _const_spec(q_gain.shape), _const_spec(k_gain.shape),
                  _const_spec(lam_params.shape), _const_spec((1, HEAD_DIM))],
        out_specs=seq_spec,
        out_shape=jax.ShapeDtypeStruct((t, width), BF16),
        scratch_shapes=[pltpu.VMEM((2, tq, LANES), F32), pltpu.VMEM((2, tq, LANES), F32),
                        pltpu.VMEM((2, tq, HEAD_DIM), F32)],
        compiler_params=pltpu.CompilerParams(dimension_semantics=("parallel", "parallel"),
                                             vmem_limit_bytes=VMEM_LIMIT),
        name="diff_attn",
    )(q, k, v, q_gain, k_gain, lam_params, sub_norm)


def kernel(x, positions, a_norm, a_w_in, a_conv_w, a_a_log, a_dt_bias, a_out_norm, a_w_out,
           kv_norm, w_kv, k_norm, b_norm, b_w_q, b_q_norm, b_lambda, b_sub_norm, b_w_out,
           mlp_norm, mlp_w1, mlp_w2):
    batch, seq_len, d = x.shape
    assert d == D_MODEL and a_norm.shape[0] == 1 and b_norm.shape[0] == 1
    assert seq_len % ROW_TILE == 0 and seq_len % ATTN_TILE == 0 and seq_len % DELTA_CHUNK == 0
    t = batch * seq_len
    width = HEADS * HEAD_DIM
    xf = x.reshape(t, d)

    w_in = a_w_in[0]
    pad = jnp.zeros((HEADS, 1), F32)
    alog16 = jnp.concatenate([pad, a_a_log[0].reshape(HEADS, 1)], axis=0)
    dtb16 = jnp.concatenate([pad, a_dt_bias[0].reshape(HEADS, 1)], axis=0)
    q, k, v, z, gates = _gdn_in(xf, a_norm[0].reshape(1, d), w_in.astype(BF16), a_conv_w[0],
                                alog16, dtb16, seq_len)
    o = _delta(q, k, v, z, gates, a_out_norm[0].reshape(1, HEAD_DIM), batch, seq_len)
    w1_all = mlp_w1.astype(BF16)
    w2_all = mlp_w2.astype(BF16)
    xf = _proj_mlp(o, xf, a_w_out[0].astype(BF16), mlp_norm[0].reshape(1, d), w1_all, w2_all, 0)

    half = ROPE_HALF
    freqs = ROPE_THETA ** (-jnp.arange(half, dtype=F32) / half)
    freq_row = jnp.tile(freqs, LANES // half).reshape(1, LANES)
    cos, s1, s2 = _rope_tab(positions.reshape(t, 1), freq_row)
    k_gain = jnp.tile(k_norm, width // MAP_DIM).reshape(1, width)
    q_gain = jnp.tile(b_q_norm[0], width // MAP_DIM).reshape(1, width)
    kr, vv, qr = _attn_in(xf, kv_norm.reshape(1, d), b_norm[0].reshape(1, d),
                          w_kv.astype(BF16), b_w_q[0].astype(BF16), k_gain, q_gain, cos, s1, s2)
    lam_init = 0.8 - 0.6 * math.exp(-0.3 * 1)
    oa = _diff_attn(qr, kr, vv, b_q_norm[0].reshape(1, MAP_DIM), k_norm.reshape(1, MAP_DIM),
                    b_lambda[0], b_sub_norm[0].reshape(1, HEAD_DIM), batch, seq_len, lam_init)
    xf = _proj_mlp(oa, xf, b_w_out[0].astype(BF16), mlp_norm[1].reshape(1, d), w1_all, w2_all, 1)
    return xf.reshape(batch, seq_len, d)
```

```python
import functools
import math

import jax
import jax.numpy as jnp
from jax import lax
from jax.experimental import pallas as pl
from jax.experimental.pallas import tpu as pltpu

F32 = jnp.float32
BF16 = jnp.bfloat16

D_MODEL = 1024
HEADS = 8
HEAD_DIM = 128
MAP_DIM = 64
ROPE_HALF = MAP_DIM // 2
CONV_WIDTH = 4
D_FF = 4 * D_MODEL
ROPE_THETA = 10000.0
EPS = 1e-6
NEG_INF = -1e30
LOG2E = math.log2(math.e)
SCORE_BOUND_COEF = MAP_DIM * MAP_DIM ** -0.5 * LOG2E * 1.02
SCORE_BOUND_LIMIT = 100.0

LANES = 128
SUBLANES = 8
MXU_DIM = 256

ROW_TILE = 512
GDN_ROW_TILE = 512
GDN_COL_BLOCK = 2 * MXU_DIM
FF_CHUNK = 1024
DELTA_CHUNK = 64
DELTA_GROUPS = (8, 8, 8, 8)
DELTA_HEADS = 2
ROPE_TILE = 2048
ATTN_TILE = 512
ATTN_DIAG_PIECE = 256
VMEM_LIMIT = 56 * 1024 * 1024

NT_DIMS = (((1,), (1,)), ((), ()))
TN_DIMS = (((0,), (0,)), ((), ()))


def _rms_hat(x):
    return x * lax.rsqrt(jnp.mean(x * x, axis=-1, keepdims=True) + EPS)


def _sigmoid(x):
    return 1.0 / (1.0 + jnp.exp(-x))


def _silu(x):
    h = 0.5 * x
    return h * jnp.tanh(h) + h


def _softplus(x):
    return jnp.maximum(x, 0.0) + jnp.log(1.0 + jnp.exp(-jnp.abs(x)))


def _dot(a, b):
    return jnp.dot(a, b, preferred_element_type=F32)


def _const_spec(shape):
    zeros = (0,) * len(shape)
    return pl.BlockSpec(shape, lambda *_: zeros, pipeline_mode=pl.Buffered(1))


def _gdn_in_kernel(x_ref, nw_ref, w_ref, cw_ref, alog_ref, dtb_ref,
                   q_ref, k_ref, v_ref, z_ref, gate_ref, xn_ref, tail_ref, *, tiles_per_seq):
    tm = x_ref.shape[0]
    width = HEADS * HEAD_DIM
    step = pl.program_id(0)

    @pl.when(step == 0)
    def _():
        tail_ref[...] = jnp.zeros(tail_ref.shape, F32)

    xn_ref[...] = (_rms_hat(x_ref[...]) * nw_ref[...]).astype(BF16)
    seq_start = (step % tiles_per_seq) == 0

    cb = GDN_COL_BLOCK
    blocks = 3 * width // cb

    def project(blk):
        return _dot(xn_ref[...], w_ref[:, blk * cb:(blk + 1) * cb])

    p = project(0)
    for blk in range(blocks):
        p_next = project(blk + 1) if blk + 1 < blocks else None
        cols = slice(blk * cb, (blk + 1) * cb)
        prev = jnp.where(seq_start, 0.0, tail_ref[:, cols])
        tail_ref[:, cols] = p[tm - SUBLANES:, :]
        xp = jnp.concatenate([prev, p], axis=0)
        cw = cw_ref[:, cols]
        c = cw[CONV_WIDTH - 1:CONV_WIDTH, :] * p
        for j in range(CONV_WIDTH - 1):
            c = c + cw[j:j + 1, :] * pltpu.roll(xp, CONV_WIDTH - 1 - j, 0)[SUBLANES:, :]
        kind, col = divmod(blk * cb, width)
        (q_ref, k_ref, v_ref)[kind][:, col:col + cb] = c
        p = p_next

    z_ref[...] = _dot(xn_ref[...], w_ref[:, 3 * width:4 * width])

    g_rows = _dot(xn_ref[...], w_ref[:, 4 * width:])
    eye_g = (lax.broadcasted_iota(jnp.int32, (2 * HEADS, 2 * HEADS), 0)
             == lax.broadcasted_iota(jnp.int32, (2 * HEADS, 2 * HEADS), 1)).astype(F32)
    gt = lax.dot_general(eye_g, g_rows, NT_DIMS, precision=lax.Precision.HIGHEST,
                         preferred_element_type=F32)
    beta = _sigmoid(gt)
    decay = -jnp.exp(alog_ref[...]) * _softplus(gt + dtb_ref[...])
    row = lax.broadcasted_iota(jnp.int32, gt.shape, 0)
    gate_ref[...] = jnp.where(row < HEADS, beta, decay)


def _gdn_in(x, norm_w, w_in, conv_w, alog16, dtb16, seq_len):
    t, d = x.shape
    tm = GDN_ROW_TILE
    width = HEADS * HEAD_DIM
    act = jax.ShapeDtypeStruct((t, width), F32)
    row_spec = pl.BlockSpec((tm, width), lambda i: (i, 0))
    return pl.pallas_call(
        functools.partial(_gdn_in_kernel, tiles_per_seq=seq_len // tm),
        grid=(t // tm,),
        in_specs=[
            pl.BlockSpec((tm, d), lambda i: (i, 0)),
            _const_spec((1, d)),
            _const_spec(w_in.shape),
            _const_spec(conv_w.shape),
            _const_spec(alog16.shape),
            _const_spec(dtb16.shape),
        ],
        out_specs=[row_spec, row_spec, row_spec, row_spec,
                   pl.BlockSpec((2 * HEADS, tm), lambda i: (0, i))],
        out_shape=[act, act, act, act, jax.ShapeDtypeStruct((2 * HEADS, t), F32)],
        scratch_shapes=[pltpu.VMEM((tm, d), BF16), pltpu.VMEM((SUBLANES, 3 * width), F32)],
        compiler_params=pltpu.CompilerParams(dimension_semantics=("arbitrary",),
                                             vmem_limit_bytes=VMEM_LIMIT),
        name="gdn_in",
    )(x, norm_w, w_in, conv_w, alog16, dtb16)


def _delta_kernel(q_ref, k_ref, v_ref, z_ref, beta_ref, g_ref, onw_ref, o_ref, gc_ref,
                  qw_a, b_a, op_a, cd_a, qw_b, b_b, op_b, cd_b, *, chunk, groups, heads):
    c_len = chunk
    bufs = ((qw_a, b_a, op_a, cd_a), (qw_b, b_b, op_b, cd_b))
    row = lax.broadcasted_iota(jnp.int32, (c_len, c_len), 0)
    col = lax.broadcasted_iota(jnp.int32, (c_len, c_len), 1)
    causal = row >= col
    strict = row > col
    eye = row == col
    upper = (row <= col).astype(F32)
    for hh in range(heads):
        gc_ref[hh] = jnp.dot(g_ref[hh], upper, precision=lax.Precision.HIGHEST,
                             preferred_element_type=F32)
    onw = onw_ref[...]
    n_steps = int(math.log2(c_len))

    def to_col(r):
        return jnp.sum(jnp.where(eye, r, 0.0), axis=1, keepdims=True)

    def chunk_rows(c):
        return slice(c * c_len, (c + 1) * c_len)

    def head_cols(hh):
        return slice(hh * HEAD_DIM, (hh + 1) * HEAD_DIM)

    def unit_rows(a, scale):
        return a * (lax.rsqrt(jnp.sum(a * a, axis=-1, keepdims=True) + EPS) * scale)

    def run(prep, adv, st):
        todo = list(range(adv[1])) if adv is not None else []

        def advance_one(st):
            if not todo:
                return st
            g = todo.pop(0)
            first, _, (qw_ref, b_ref, op_ref, cd_ref) = adv
            rows = chunk_rows(first + g)
            slots = [g * heads + hh for hh in range(heads)]
            res = [_dot(qw_ref[slots[hh]], st[hh].astype(BF16)) for hh in range(heads)]
            out = [res[hh][:c_len] + op_ref[slots[hh]] for hh in range(heads)]
            st = [cd_ref[slots[hh], 0:1, :] * st[hh] - res[hh][c_len:] + b_ref[slots[hh]]
                  for hh in range(heads)]
            for hh in range(heads):
                zc = z_ref[rows, head_cols(hh)]
                o_ref[rows, head_cols(hh)] = (_rms_hat(out[hh]) * onw * _silu(zc)).astype(o_ref.dtype)
            return st

        if prep is not None:
            first, size, (qw_ref, b_ref, op_ref, cd_ref) = prep
            chains = [(g, hh) for g in range(size) for hh in range(heads)]
            n = len(chains)
            cidx = [first + g for g, _ in chains]
            q = [unit_rows(_silu(q_ref[chunk_rows(cidx[i]), head_cols(chains[i][1])]),
                           HEAD_DIM ** -0.5) for i in range(n)]
            k = [unit_rows(_silu(k_ref[chunk_rows(cidx[i]), head_cols(chains[i][1])]), 1.0)
                 for i in range(n)]
            gc_r = [gc_ref[chains[i][1], pl.ds(cidx[i], 1), :] for i in range(n)]
            beta_c = [to_col(beta_ref[chains[i][1], pl.ds(cidx[i], 1), :]) for i in range(n)]
            gc_c = [to_col(r) for r in gc_r]
            gc_last = [r[:, c_len - 1:c_len] for r in gc_r]
            e_c = [jnp.exp(x) for x in gc_c]
            kb = [k[i] * beta_c[i] for i in range(n)]
            s = [lax.dot_general(jnp.concatenate([kb[i], q[i]], axis=0).astype(BF16),
                                 k[i].astype(BF16), NT_DIMS, preferred_element_type=F32)
                 for i in range(n)]
            st = advance_one(st)
            decay = [jnp.where(causal, jnp.exp(jnp.where(causal, gc_c[i] - gc_r[i], 0.0)), 0.0)
                     for i in range(n)]
            p = [jnp.where(strict, -(s[i][:c_len] * decay[i]), 0.0) for i in range(n)]
            intra = [s[i][c_len:] * decay[i] for i in range(n)]
            r = p
            p = [_dot(x.astype(BF16), x.astype(BF16)) for x in p]
            st = advance_one(st)
            for step in range(1, n_steps):
                last = step + 1 == n_steps
                p16 = [x.astype(BF16) for x in p]
                r16 = [x.astype(BF16) for x in r]
                t = [_dot(r16[i] if last else jnp.concatenate([p16[i], r16[i]], axis=0), p16[i])
                     for i in range(n)]
                r = [r[i] + p[i] + (t[i] if last else t[i][c_len:]) for i in range(n)]
                if not last:
                    p = [x[:c_len] for x in t]
                    st = advance_one(st)
            rhs = [jnp.concatenate([_silu(v_ref[chunk_rows(cidx[i]), head_cols(chains[i][1])]) * beta_c[i],
                                    kb[i] * e_c[i]], axis=1) for i in range(n)]
            y = [rhs[i] + _dot(r[i].astype(BF16), rhs[i].astype(BF16)) for i in range(n)]
            st = advance_one(st)
            y16 = [x.astype(BF16) for x in y]
            kd16 = [(k[i] * jnp.exp(gc_last[i] - gc_c[i])).astype(BF16) for i in range(n)]
            mb = [lax.dot_general(kd16[i], y16[i], TN_DIMS, preferred_element_type=F32)
                  for i in range(n)]
            iu = [_dot(intra[i].astype(BF16), y16[i]) for i in range(n)]
            st = advance_one(st)
            for i in range(n):
                b_ref[i] = mb[i][:, :HEAD_DIM]
                qw_ref[i, 0:c_len, :] = (q[i] * e_c[i] - iu[i][:, HEAD_DIM:]).astype(BF16)
                qw_ref[i, c_len:, :] = mb[i][:, HEAD_DIM:].astype(BF16)
                op_ref[i] = iu[i][:, :HEAD_DIM]
                cd_ref[i] = jnp.broadcast_to(jnp.exp(gc_last[i]), (SUBLANES, LANES))
        while todo:
            st = advance_one(st)
        return st

    firsts = [sum(groups[:j]) for j in range(len(groups))]
    plan = [(firsts[j], groups[j], bufs[j % 2]) for j in range(len(groups))]
    st = [jnp.zeros((HEAD_DIM, HEAD_DIM), F32) for _ in range(heads)]
    for j in range(len(plan) + 1):
        st = run(plan[j] if j < len(plan) else None, plan[j - 1] if j > 0 else None, st)


def _delta(q, k, v, z, gates, out_norm, batch, seq_len):
    t, width = q.shape
    c_len = DELTA_CHUNK
    nh = DELTA_HEADS
    n_chunks = seq_len // c_len
    assert sum(DELTA_GROUPS) == n_chunks and HEADS % nh == 0
    gates3 = gates.reshape(2 * HEADS, batch * n_chunks, c_len)
    seq_spec = pl.BlockSpec((seq_len, nh * HEAD_DIM), lambda b, h: (b, h))
    slots = max(DELTA_GROUPS) * nh
    buf_set = [pltpu.VMEM((slots, c_len + HEAD_DIM, HEAD_DIM), BF16),
               pltpu.VMEM((slots, HEAD_DIM, HEAD_DIM), F32),
               pltpu.VMEM((slots, c_len, HEAD_DIM), F32),
               pltpu.VMEM((slots, SUBLANES, LANES), F32)]
    return pl.pallas_call(
        functools.partial(_delta_kernel, chunk=c_len, groups=DELTA_GROUPS, heads=nh),
        grid=(batch, HEADS // nh),
        in_specs=[seq_spec, seq_spec, seq_spec, seq_spec,
                  pl.BlockSpec((nh, n_chunks, c_len), lambda b, h: (h, b, 0)),
                  pl.BlockSpec((nh, n_chunks, c_len), lambda b, h: (HEADS // nh + h, b, 0)),
                  pl.BlockSpec((1, HEAD_DIM), lambda b, h: (0, 0))],
        out_specs=seq_spec,
        out_shape=jax.ShapeDtypeStruct((t, width), BF16),
        scratch_shapes=[pltpu.VMEM((nh, n_chunks, c_len), F32)] + buf_set + buf_set,
        compiler_params=pltpu.CompilerParams(dimension_semantics=("parallel", "parallel"),
                                             vmem_limit_bytes=VMEM_LIMIT),
        name="delta",
    )(q, k, v, z, gates3, gates3, out_norm)


def _proj_mlp_kernel(o_ref, x_ref, wo_ref, nw_ref, w1_ref, w2_ref, out_ref):
    x1 = x_ref[...] + _dot(o_ref[...].astype(BF16), wo_ref[...])
    xn = (_rms_hat(x1) * nw_ref[...]).astype(BF16)
    acc = x1
    for j in range(D_FF // FF_CHUNK):
        h = jnp.maximum(_dot(xn, w1_ref[:, j * FF_CHUNK:(j + 1) * FF_CHUNK]), 0.0)
        acc = acc + _dot((h * h).astype(BF16), w2_ref[j * FF_CHUNK:(j + 1) * FF_CHUNK, :])
    out_ref[...] = acc


def _layer_spec(stacked, layer):
    return pl.BlockSpec((None,) + stacked.shape[1:], lambda *_: (layer, 0, 0),
                        pipeline_mode=pl.Buffered(1))


def _proj_mlp(o, x, w_out, norm_w, w1, w2, layer):
    t, d = x.shape
    tm = ROW_TILE
    return pl.pallas_call(
        _proj_mlp_kernel,
        grid=(t // tm,),
        in_specs=[pl.BlockSpec((tm, o.shape[1]), lambda i: (i, 0)),
                  pl.BlockSpec((tm, d), lambda i: (i, 0)),
                  _const_spec(w_out.shape), _const_spec((1, d)),
                  _layer_spec(w1, layer), _layer_spec(w2, layer)],
        out_specs=pl.BlockSpec((tm, d), lambda i: (i, 0)),
        out_shape=jax.ShapeDtypeStruct((t, d), F32),
        compiler_params=pltpu.CompilerParams(dimension_semantics=("parallel",),
                                             vmem_limit_bytes=VMEM_LIMIT),
        name="proj_mlp",
    )(o, x, w_out, norm_w, w1, w2)


def _rope_tab_kernel(pos_ref, freq_ref, cos_ref, s1_ref, s2_ref):
    rows = pos_ref.shape[0]
    ang = pos_ref[...].astype(F32) * freq_ref[...]
    cos = jnp.cos(ang)
    sin = jnp.sin(ang)
    lane = lax.broadcasted_iota(jnp.int32, ang.shape, 1)
    first_half = (lane % MAP_DIM) < ROPE_HALF
    group = lane // ROPE_HALF

    def spread(x, i):
        y = jnp.where(group == i, x, 0.0)
        y = y + pltpu.roll(y, ROPE_HALF, 1)
        return y + pltpu.roll(y, 2 * ROPE_HALF, 1)

    for i in range(LANES // ROPE_HALF):
        out = slice(i * rows, (i + 1) * rows)
        sin_i = spread(sin, i)
        cos_ref[out, :] = spread(cos, i)
        s1_ref[out, :] = jnp.where(first_half, -sin_i, 0.0)
        s2_ref[out, :] = jnp.where(first_half, 0.0, sin_i)


def _rope_tab(pos_packed, freq_row):
    per_row = LANES // ROPE_HALF
    rows = ROPE_TILE // per_row
    t = pos_packed.shape[0] * per_row
    tab = jax.ShapeDtypeStruct((t, LANES), F32)
    spec = pl.BlockSpec((ROPE_TILE, LANES), lambda i: (i, 0))
    return pl.pallas_call(
        _rope_tab_kernel,
        grid=(t // ROPE_TILE,),
        in_specs=[pl.BlockSpec((rows, LANES), lambda i: (i, 0)), _const_spec((1, LANES))],
        out_specs=[spec, spec, spec],
        out_shape=[tab, tab, tab],
        compiler_params=pltpu.CompilerParams(dimension_semantics=("parallel",)),
        name="rope_tab",
    )(pos_packed, freq_row)


def _attn_in_kernel(x_ref, kvn_ref, qnw_ref, wkv_ref, wq_ref, kg_ref, qg_ref,
                    cos_ref, s1_ref, s2_ref, k_ref, v_ref, q_ref, kvx_ref, qx_ref):
    width = HEADS * HEAD_DIM
    cb = 2 * MXU_DIM
    xhat = _rms_hat(x_ref[...])
    kvx_ref[...] = (xhat * kvn_ref[...]).astype(BF16)
    qx_ref[...] = (xhat * qnw_ref[...]).astype(BF16)
    cos = cos_ref[...]
    s1 = s1_ref[...]
    s2 = s2_ref[...]
    r = lax.broadcasted_iota(jnp.int32, (MXU_DIM, MXU_DIM), 0) // MAP_DIM
    c = lax.broadcasted_iota(jnp.int32, (MXU_DIM, MXU_DIM), 1) // MAP_DIM
    group_ones = (r == c).astype(BF16)

    def norm_rope(raw, gain, scale, o_ref, col):
        for s in range(raw.shape[1] // MXU_DIM):
            blk = raw[:, s * MXU_DIM:(s + 1) * MXU_DIM]
            ss = _dot((blk * blk).astype(BF16), group_ones)
            lo = col + s * MXU_DIM
            nb = blk * lax.rsqrt(ss * (1.0 / MAP_DIM) + EPS) * gain[:, lo:lo + MXU_DIM]
            for hh in range(MXU_DIM // LANES):
                xb = nb[:, hh * LANES:(hh + 1) * LANES]
                rot = xb * cos + pltpu.roll(xb, LANES - ROPE_HALF, 1) * s1 + pltpu.roll(xb, ROPE_HALF, 1) * s2
                o_ref[:, lo + hh * LANES:lo + (hh + 1) * LANES] = (rot * scale).astype(o_ref.dtype)

    plan = []
    for col in range(0, width, cb):
        plan += [("k", col), ("q", col)]
    plan += [("v", col) for col in range(0, width, cb)]

    def project(item):
        kind, col = item
        if kind == "q":
            return _dot(qx_ref[...], wq_ref[:, col:col + cb])
        base = 0 if kind == "k" else width
        return _dot(kvx_ref[...], wkv_ref[:, base + col:base + col + cb])

    raw = project(plan[0])
    for i, (kind, col) in enumerate(plan):
        raw_next = project(plan[i + 1]) if i + 1 < len(plan) else None
        if kind == "v":
            v_ref[:, col:col + cb] = raw.astype(v_ref.dtype)
        elif kind == "k":
            norm_rope(raw, kg_ref[...], 1.0, k_ref, col)
        else:
            norm_rope(raw, qg_ref[...], MAP_DIM ** -0.5 * LOG2E, q_ref, col)
        raw = raw_next


def _attn_in(x, kv_norm, q_norm_w, w_kv, w_q, k_gain, q_gain, cos, s1, s2):
    t, d = x.shape
    tm = ROW_TILE
    width = HEADS * HEAD_DIM
    act = jax.ShapeDtypeStruct((t, width), BF16)
    row_spec = pl.BlockSpec((tm, width), lambda i: (i, 0))
    tab_spec = pl.BlockSpec((tm, LANES), lambda i: (i, 0))
    return pl.pallas_call(
        _attn_in_kernel,
        grid=(t // tm,),
        in_specs=[pl.BlockSpec((tm, d), lambda i: (i, 0)),
                  _const_spec((1, d)), _const_spec((1, d)),
                  _const_spec(w_kv.shape), _const_spec(w_q.shape),
                  _const_spec((1, width)), _const_spec((1, width)),
                  tab_spec, tab_spec, tab_spec],
        out_specs=[row_spec, row_spec, row_spec],
        out_shape=[act, act, act],
        scratch_shapes=[pltpu.VMEM((tm, d), BF16), pltpu.VMEM((tm, d), BF16)],
        compiler_params=pltpu.CompilerParams(dimension_semantics=("parallel",),
                                             vmem_limit_bytes=VMEM_LIMIT),
        name="attn_in",
    )(x, kv_norm, q_norm_w, w_kv, w_q, k_gain, q_gain, cos, s1, s2)


def _diff_attn_kernel(q_ref, k_ref, v_ref, qg_ref, kg_ref, lam_ref, snw_ref, o_ref,
                      m_ref, l_ref, acc_ref, *, tq, lam_init):
    n_q = q_ref.shape[0] // tq
    piece = ATTN_DIAG_PIECE
    lane = lax.broadcasted_iota(jnp.int32, (tq, HEAD_DIM), 1)
    lp = lam_ref[...]
    lam = (jnp.exp(jnp.sum(lp[0:1] * lp[1:2], axis=-1, keepdims=True))
           - jnp.exp(jnp.sum(lp[2:3] * lp[3:4], axis=-1, keepdims=True)) + lam_init)
    snw = snw_ref[...]

    def q_maps(rows):
        q = q_ref[rows, :]
        zero = jnp.zeros_like(q)
        return (jnp.where(lane < MAP_DIM, q, zero), jnp.where(lane < MAP_DIM, zero, q))

    def scores(qm, rows):
        ks = k_ref[rows, :]
        return [lax.dot_general(x, ks, NT_DIMS, preferred_element_type=F32) for x in qm]

    def finish(rows, acc0, acc1, l0, l1):
        o = acc0 / l0 - lam * (acc1 / l1)
        o_ref[rows, :] = (_rms_hat(o) * snw * (1.0 - lam_init)).astype(o_ref.dtype)

    bound = SCORE_BOUND_COEF * jnp.max(jnp.abs(qg_ref[...])) * jnp.max(jnp.abs(kg_ref[...]))
    bounded = bound <= SCORE_BOUND_LIMIT

    @pl.when(bounded)
    def _():
        def lane_sums(p):
            out = p[:, 0:LANES]
            for b in range(1, p.shape[1] // LANES):
                out = out + p[:, b * LANES:(b + 1) * LANES]
            return out

        def plus(a, b):
            return b if a is None else a + b

        for qi in range(n_q):
            q0 = qi * tq
            qm = q_maps(slice(q0, q0 + tq))
            acc = [None, None]
            lsum = [None, None]
            s_next = scores(qm, slice(0, tq)) if qi > 0 else None
            for j in range(qi):
                s = s_next
                s_next = scores(qm, slice((j + 1) * tq, (j + 2) * tq)) if j + 1 < qi else None
                vs = v_ref[j * tq:(j + 1) * tq, :]
                for mi in range(2):
                    p = jnp.exp2(s[mi])
                    lsum[mi] = plus(lsum[mi], lane_sums(p))
                    acc[mi] = plus(acc[mi], _dot(p.astype(BF16), vs))
            for bi in range(tq // piece):
                rows = slice(bi * piece, (bi + 1) * piece)
                ncols = (bi + 1) * piece
                sd = scores([x[rows, :] for x in qm], slice(q0, q0 + ncols))
                r = lax.broadcasted_iota(jnp.int32, (piece, ncols), 0) + bi * piece
                c = lax.broadcasted_iota(jnp.int32, (piece, ncols), 1)
                vs = v_ref[q0:q0 + ncols, :]
                fin = []
                for mi in range(2):
                    p = jnp.where(r >= c, jnp.exp2(sd[mi]), 0.0)
                    below_l = None if lsum[mi] is None else lsum[mi][rows, :]
                    below_a = None if acc[mi] is None else acc[mi][rows, :]
                    fin.append((plus(below_a, _dot(p.astype(BF16), vs)),
                                jnp.sum(plus(below_l, lane_sums(p)), axis=-1, keepdims=True)))
                finish(slice(q0 + bi * piece, q0 + (bi + 1) * piece),
                       fin[0][0], fin[1][0], fin[0][1], fin[1][1])

    @pl.when(jnp.logical_not(bounded))
    def _():
        r = lax.broadcasted_iota(jnp.int32, (tq, tq), 0)
        c = lax.broadcasted_iota(jnp.int32, (tq, tq), 1)

        def q_tile(qi, carry):
            q_rows = pl.ds(pl.multiple_of(qi * tq, tq), tq)
            qm = q_maps(q_rows)
            m_ref[...] = jnp.full(m_ref.shape, NEG_INF, F32)
            l_ref[...] = jnp.zeros(l_ref.shape, F32)
            acc_ref[...] = jnp.zeros(acc_ref.shape, F32)

            def step(j, masked):
                kv_rows = pl.ds(pl.multiple_of(j * tq, tq), tq)
                s = scores(qm, kv_rows)
                vs = v_ref[kv_rows, :]
                for mi in range(2):
                    sm = jnp.where(r >= c, s[mi], NEG_INF) if masked else s[mi]
                    m_old = m_ref[mi]
                    m_new = jnp.maximum(m_old, jnp.max(sm, axis=-1, keepdims=True))
                    alpha = jnp.exp2(m_old - m_new)
                    p = jnp.exp2(sm - m_new[:, 0:1])
                    l_ref[mi] = alpha * l_ref[mi] + jnp.sum(p, axis=-1, keepdims=True)
                    acc_ref[mi] = alpha * acc_ref[mi] + _dot(p.astype(BF16), vs)
                    m_ref[mi] = m_new

            def full_body(j, carry):
                step(j, False)
                return carry

            lax.fori_loop(0, qi, full_body, 0)
            step(qi, True)
            finish(q_rows, acc_ref[0], acc_ref[1], l_ref[0], l_ref[1])
            return carry

        lax.fori_loop(0, n_q, q_tile, 0)


def _diff_attn(q, k, v, q_gain, k_gain, lam_params, sub_norm, batch, seq_len, lam_init):
    t, width = q.shape
    tq = ATTN_TILE
    seq_spec = pl.BlockSpec((seq_len, HEAD_DIM), lambda b, h: (b, h))
    return pl.pallas_call(
        functools.partial(_diff_attn_kernel, tq=tq, lam_init=lam_init),
        grid=(batch, HEADS),
        in_specs=[seq_spec, seq_spec, seq_spec,
                  _const_spec(q_gain.shape), _const_spec(k_gain.shape),
                  _const_spec(lam_params.shape), _const_spec((1, HEAD_DIM))],
        out_specs=seq_spec,
        out_shape=jax.ShapeDtypeStruct((t, width), BF16),
        scratch_shapes=[pltpu.VMEM((2, tq, LANES), F32), pltpu.VMEM((2, tq, LANES), F32),
                        pltpu.VMEM((2, tq, HEAD_DIM), F32)],
        compiler_params=pltpu.CompilerParams(dimension_semantics=("parallel", "parallel"),
                                             vmem_limit_bytes=VMEM_LIMIT),
        name="diff_attn",
    )(q, k, v, q_gain, k_gain, lam_params, sub_norm)


def kernel(x, positions, a_norm, a_w_in, a_conv_w, a_a_log, a_dt_bias, a_out_norm, a_w_out,
           kv_norm, w_kv, k_norm, b_norm, b_w_q, b_q_norm, b_lambda, b_sub_norm, b_w_out,
           mlp_norm, mlp_w1, mlp_w2):
    batch, seq_len, d = x.shape
    assert d == D_MODEL and a_norm.shape[0] == 1 and b_norm.shape[0] == 1
    assert seq_len % ROW_TILE == 0 and seq_len % ATTN_TILE == 0 and seq_len % DELTA_CHUNK == 0
    assert (batch * seq_len) % ROPE_TILE == 0
    t = batch * seq_len
    width = HEADS * HEAD_DIM
    xf = x.reshape(t, d)

    w_in = a_w_in[0]
    pad = jnp.zeros((HEADS, 1), F32)
    alog16 = jnp.concatenate([pad, a_a_log[0].reshape(HEADS, 1)], axis=0)
    dtb16 = jnp.concatenate([pad, a_dt_bias[0].reshape(HEADS, 1)], axis=0)
    q, k, v, z, gates = _gdn_in(xf, a_norm[0].reshape(1, d), w_in.astype(BF16), a_conv_w[0],
                                alog16, dtb16, seq_len)
    o = _delta(q, k, v, z, gates, a_out_norm[0].reshape(1, HEAD_DIM), batch, seq_len)
    w1_all = mlp_w1.astype(BF16)
    w2_all = mlp_w2.astype(BF16)
    xf = _proj_mlp(o, xf, a_w_out[0].astype(BF16), mlp_norm[0].reshape(1, d), w1_all, w2_all, 0)

    half = ROPE_HALF
    freqs = ROPE_THETA ** (-jnp.arange(half, dtype=F32) / half)
    freq_row = jnp.tile(freqs, LANES // half).reshape(1, LANES)
    per_row = LANES // half
    pos_packed = jnp.repeat(
        positions.reshape(t // ROPE_TILE, per_row, ROPE_TILE // per_row).transpose(0, 2, 1)
        .reshape(t // per_row, per_row), half, axis=1)
    cos, s1, s2 = _rope_tab(pos_packed, freq_row)
    k_gain = jnp.tile(k_norm, width // MAP_DIM).reshape(1, width)
    q_gain = jnp.tile(b_q_norm[0], width // MAP_DIM).reshape(1, width)
    kr, vv, qr = _attn_in(xf, kv_norm.reshape(1, d), b_norm[0].reshape(1, d),
                          w_kv.astype(BF16), b_w_q[0].astype(BF16), k_gain, q_gain, cos, s1, s2)
    lam_init = 0.8 - 0.6 * math.exp(-0.3 * 1)
    oa = _diff_attn(qr, kr, vv, b_q_norm[0].reshape(1, MAP_DIM), k_norm.reshape(1, MAP_DIM),
                    b_lambda[0], b_sub_norm[0].reshape(1, HEAD_DIM), batch, seq_len, lam_init)
    xf = _proj_mlp(oa, xf, b_w_out[0].astype(BF16), mlp_norm[1].reshape(1, d), w1_all, w2_all, 1)
    return xf.reshape(batch, seq_len, d)
```

```python
import functools
import math

import jax
import jax.numpy as jnp
from jax import lax
from jax.experimental import pallas as pl
from jax.experimental.pallas import tpu as pltpu

F32 = jnp.float32
BF16 = jnp.bfloat16

D_MODEL = 1024
HEADS = 8
HEAD_DIM = 128
MAP_DIM = 64
ROPE_HALF = MAP_DIM // 2
CONV_WIDTH = 4
D_FF = 4 * D_MODEL
ROPE_THETA = 10000.0
EPS = 1e-6
NEG_INF = -1e30
LOG2E = math.log2(math.e)
SCORE_BOUND_COEF = MAP_DIM * MAP_DIM ** -0.5 * LOG2E * 1.02
SCORE_BOUND_LIMIT = 100.0

LANES = 128
SUBLANES = 8
MXU_DIM = 256

ROW_TILE = 512
GDN_ROW_TILE = 512
GDN_COL_BLOCK = 2 * MXU_DIM
FF_CHUNK = 1024
DELTA_CHUNK = 64
DELTA_GROUPS = (4,) * 8
DELTA_HEADS = 4
ROPE_TILE = 2048
ATTN_TILE = 512
ATTN_DIAG_PIECE = 256
VMEM_LIMIT = 56 * 1024 * 1024

NT_DIMS = (((1,), (1,)), ((), ()))
TN_DIMS = (((0,), (0,)), ((), ()))


def _rms_hat(x):
    return x * lax.rsqrt(jnp.mean(x * x, axis=-1, keepdims=True) + EPS)


def _sigmoid(x):
    return 1.0 / (1.0 + jnp.exp(-x))


def _silu(x):
    h = 0.5 * x
    return h * jnp.tanh(h) + h


def _softplus(x):
    return jnp.maximum(x, 0.0) + jnp.log(1.0 + jnp.exp(-jnp.abs(x)))


def _dot(a, b):
    return jnp.dot(a, b, preferred_element_type=F32)


def _const_spec(shape):
    zeros = (0,) * len(shape)
    return pl.BlockSpec(shape, lambda *_: zeros, pipeline_mode=pl.Buffered(1))


def _gdn_in_kernel(x_ref, nw_ref, w_ref, cw_ref, alog_ref, dtb_ref,
                   q_ref, k_ref, v_ref, z_ref, gate_ref, xn_ref, tail_ref, *, tiles_per_seq):
    tm = x_ref.shape[0]
    width = HEADS * HEAD_DIM
    step = pl.program_id(0)

    @pl.when(step == 0)
    def _():
        tail_ref[...] = jnp.zeros(tail_ref.shape, F32)

    xn_ref[...] = (_rms_hat(x_ref[...]) * nw_ref[...]).astype(BF16)
    seq_start = (step % tiles_per_seq) == 0

    cb = GDN_COL_BLOCK
    blocks = 3 * width // cb

    def project(blk):
        return _dot(xn_ref[...], w_ref[:, blk * cb:(blk + 1) * cb])

    p = project(0)
    for blk in range(blocks):
        p_next = project(blk + 1) if blk + 1 < blocks else None
        cols = slice(blk * cb, (blk + 1) * cb)
        prev = jnp.where(seq_start, 0.0, tail_ref[:, cols])
        tail_ref[:, cols] = p[tm - SUBLANES:, :]
        xp = jnp.concatenate([prev, p], axis=0)
        cw = cw_ref[:, cols]
        c = cw[CONV_WIDTH - 1:CONV_WIDTH, :] * p
        for j in range(CONV_WIDTH - 1):
            c = c + cw[j:j + 1, :] * pltpu.roll(xp, CONV_WIDTH - 1 - j, 0)[SUBLANES:, :]
        kind, col = divmod(blk * cb, width)
        (q_ref, k_ref, v_ref)[kind][:, col:col + cb] = c
        p = p_next

    z_ref[...] = _dot(xn_ref[...], w_ref[:, 3 * width:4 * width])

    g_rows = _dot(xn_ref[...], w_ref[:, 4 * width:])
    eye_g = (lax.broadcasted_iota(jnp.int32, (2 * HEADS, 2 * HEADS), 0)
             == lax.broadcasted_iota(jnp.int32, (2 * HEADS, 2 * HEADS), 1)).astype(F32)
    gt = lax.dot_general(eye_g, g_rows, NT_DIMS, precision=lax.Precision.HIGHEST,
                         preferred_element_type=F32)
    beta = _sigmoid(gt)
    decay = -jnp.exp(alog_ref[...]) * _softplus(gt + dtb_ref[...])
    row = lax.broadcasted_iota(jnp.int32, gt.shape, 0)
    gate_ref[...] = jnp.where(row < HEADS, beta, decay)


def _gdn_in(x, norm_w, w_in, conv_w, alog16, dtb16, seq_len):
    t, d = x.shape
    tm = GDN_ROW_TILE
    width = HEADS * HEAD_DIM
    act = jax.ShapeDtypeStruct((t, width), F32)
    row_spec = pl.BlockSpec((tm, width), lambda i: (i, 0))
    return pl.pallas_call(
        functools.partial(_gdn_in_kernel, tiles_per_seq=seq_len // tm),
        grid=(t // tm,),
        in_specs=[
            pl.BlockSpec((tm, d), lambda i: (i, 0)),
            _const_spec((1, d)),
            _const_spec(w_in.shape),
            _const_spec(conv_w.shape),
            _const_spec(alog16.shape),
            _const_spec(dtb16.shape),
        ],
        out_specs=[row_spec, row_spec, row_spec, row_spec,
                   pl.BlockSpec((2 * HEADS, tm), lambda i: (0, i))],
        out_shape=[act, act, act, act, jax.ShapeDtypeStruct((2 * HEADS, t), F32)],
        scratch_shapes=[pltpu.VMEM((tm, d), BF16), pltpu.VMEM((SUBLANES, 3 * width), F32)],
        compiler_params=pltpu.CompilerParams(dimension_semantics=("arbitrary",),
                                             vmem_limit_bytes=VMEM_LIMIT),
        name="gdn_in",
    )(x, norm_w, w_in, conv_w, alog16, dtb16)


def _delta_kernel(q_ref, k_ref, v_ref, z_ref, beta_ref, g_ref, onw_ref, o_ref, gc_ref,
                  qw_a, b_a, op_a, cd_a, qw_b, b_b, op_b, cd_b, *, chunk, groups, heads):
    c_len = chunk
    bufs = ((qw_a, b_a, op_a, cd_a), (qw_b, b_b, op_b, cd_b))
    row = lax.broadcasted_iota(jnp.int32, (c_len, c_len), 0)
    col = lax.broadcasted_iota(jnp.int32, (c_len, c_len), 1)
    causal = row >= col
    strict = row > col
    eye = row == col
    upper = (row <= col).astype(F32)
    for hh in range(heads):
        gc_ref[hh] = jnp.dot(g_ref[hh], upper, precision=lax.Precision.HIGHEST,
                             preferred_element_type=F32)
    onw = onw_ref[...]
    n_steps = int(math.log2(c_len))

    def to_col(r):
        return jnp.sum(jnp.where(eye, r, 0.0), axis=1, keepdims=True)

    def chunk_rows(c):
        return slice(c * c_len, (c + 1) * c_len)

    def head_cols(hh):
        return slice(hh * HEAD_DIM, (hh + 1) * HEAD_DIM)

    def unit_rows(a, scale):
        return a * (lax.rsqrt(jnp.sum(a * a, axis=-1, keepdims=True) + EPS) * scale)

    def run(prep, adv, st):
        todo = list(range(adv[1])) if adv is not None else []

        def advance_one(st):
            if not todo:
                return st
            g = todo.pop(0)
            first, _, (qw_ref, b_ref, op_ref, cd_ref) = adv
            rows = chunk_rows(first + g)
            slots = [g * heads + hh for hh in range(heads)]
            res = [_dot(qw_ref[slots[hh]], st[hh].astype(BF16)) for hh in range(heads)]
            out = [res[hh][:c_len] + op_ref[slots[hh]] for hh in range(heads)]
            st = [cd_ref[slots[hh], 0:1, :] * st[hh] - res[hh][c_len:] + b_ref[slots[hh]]
                  for hh in range(heads)]
            for hh in range(heads):
                zc = z_ref[rows, head_cols(hh)]
                o_ref[rows, head_cols(hh)] = (_rms_hat(out[hh]) * onw * _silu(zc)).astype(o_ref.dtype)
            return st

        if prep is not None:
            first, size, (qw_ref, b_ref, op_ref, cd_ref) = prep
            chains = [(g, hh) for g in range(size) for hh in range(heads)]
            n = len(chains)
            cidx = [first + g for g, _ in chains]
            q = [unit_rows(_silu(q_ref[chunk_rows(cidx[i]), head_cols(chains[i][1])]),
                           HEAD_DIM ** -0.5) for i in range(n)]
            k = [unit_rows(_silu(k_ref[chunk_rows(cidx[i]), head_cols(chains[i][1])]), 1.0)
                 for i in range(n)]
            gc_r = [gc_ref[chains[i][1], pl.ds(cidx[i], 1), :] for i in range(n)]
            beta_c = [to_col(beta_ref[chains[i][1], pl.ds(cidx[i], 1), :]) for i in range(n)]
            gc_c = [to_col(r) for r in gc_r]
            gc_last = [r[:, c_len - 1:c_len] for r in gc_r]
            e_c = [jnp.exp(x) for x in gc_c]
            kb = [k[i] * beta_c[i] for i in range(n)]
            s = [lax.dot_general(jnp.concatenate([kb[i], q[i]], axis=0).astype(BF16),
                                 k[i].astype(BF16), NT_DIMS, preferred_element_type=F32)
                 for i in range(n)]
            st = advance_one(st)
            decay = [jnp.where(causal, jnp.exp(jnp.where(causal, gc_c[i] - gc_r[i], 0.0)), 0.0)
                     for i in range(n)]
            p = [jnp.where(strict, -(s[i][:c_len] * decay[i]), 0.0) for i in range(n)]
            intra = [s[i][c_len:] * decay[i] for i in range(n)]
            r = p
            p = [_dot(x.astype(BF16), x.astype(BF16)) for x in p]
            st = advance_one(st)
            for step in range(1, n_steps):
                last = step + 1 == n_steps
                p16 = [x.astype(BF16) for x in p]
                r16 = [x.astype(BF16) for x in r]
                t = [_dot(r16[i] if last else jnp.concatenate([p16[i], r16[i]], axis=0), p16[i])
                     for i in range(n)]
                r = [r[i] + p[i] + (t[i] if last else t[i][c_len:]) for i in range(n)]
                if not last:
                    p = [x[:c_len] for x in t]
                    st = advance_one(st)
            rhs = [jnp.concatenate([_silu(v_ref[chunk_rows(cidx[i]), head_cols(chains[i][1])]) * beta_c[i],
                                    kb[i] * e_c[i]], axis=1) for i in range(n)]
            y = [rhs[i] + _dot(r[i].astype(BF16), rhs[i].astype(BF16)) for i in range(n)]
            st = advance_one(st)
            y16 = [x.astype(BF16) for x in y]
            kd16 = [(k[i] * jnp.exp(gc_last[i] - gc_c[i])).astype(BF16) for i in range(n)]
            mb = [lax.dot_general(kd16[i], y16[i], TN_DIMS, preferred_element_type=F32)
                  for i in range(n)]
            iu = [_dot(intra[i].astype(BF16), y16[i]) for i in range(n)]
            st = advance_one(st)
            for i in range(n):
                b_ref[i] = mb[i][:, :HEAD_DIM]
                qw_ref[i, 0:c_len, :] = (q[i] * e_c[i] - iu[i][:, HEAD_DIM:]).astype(BF16)
                qw_ref[i, c_len:, :] = mb[i][:, HEAD_DIM:].astype(BF16)
                op_ref[i] = iu[i][:, :HEAD_DIM]
                cd_ref[i] = jnp.broadcast_to(jnp.exp(gc_last[i]), (SUBLANES, LANES))
        while todo:
            st = advance_one(st)
        return st

    firsts = [sum(groups[:j]) for j in range(len(groups))]
    plan = [(firsts[j], groups[j], bufs[j % 2]) for j in range(len(groups))]
    st = [jnp.zeros((HEAD_DIM, HEAD_DIM), F32) for _ in range(heads)]
    for j in range(len(plan) + 1):
        st = run(plan[j] if j < len(plan) else None, plan[j - 1] if j > 0 else None, st)


def _delta(q, k, v, z, gates, out_norm, batch, seq_len):
    t, width = q.shape
    c_len = DELTA_CHUNK
    nh = DELTA_HEADS
    n_chunks = seq_len // c_len
    assert sum(DELTA_GROUPS) == n_chunks and HEADS % nh == 0
    gates3 = gates.reshape(2 * HEADS, batch * n_chunks, c_len)
    seq_spec = pl.BlockSpec((seq_len, nh * HEAD_DIM), lambda b, h: (b, h))
    slots = max(DELTA_GROUPS) * nh
    buf_set = [pltpu.VMEM((slots, c_len + HEAD_DIM, HEAD_DIM), BF16),
               pltpu.VMEM((slots, HEAD_DIM, HEAD_DIM), F32),
               pltpu.VMEM((slots, c_len, HEAD_DIM), F32),
               pltpu.VMEM((slots, SUBLANES, LANES), F32)]
    return pl.pallas_call(
        functools.partial(_delta_kernel, chunk=c_len, groups=DELTA_GROUPS, heads=nh),
        grid=(batch, HEADS // nh),
        in_specs=[seq_spec, seq_spec, seq_spec, seq_spec,
                  pl.BlockSpec((nh, n_chunks, c_len), lambda b, h: (h, b, 0)),
                  pl.BlockSpec((nh, n_chunks, c_len), lambda b, h: (HEADS // nh + h, b, 0)),
                  pl.BlockSpec((1, HEAD_DIM), lambda b, h: (0, 0))],
        out_specs=seq_spec,
        out_shape=jax.ShapeDtypeStruct((t, width), BF16),
        scratch_shapes=[pltpu.VMEM((nh, n_chunks, c_len), F32)] + buf_set + buf_set,
        compiler_params=pltpu.CompilerParams(dimension_semantics=("parallel", "parallel"),
                                             vmem_limit_bytes=VMEM_LIMIT),
        name="delta",
    )(q, k, v, z, gates3, gates3, out_norm)


def _proj_mlp_kernel(o_ref, x_ref, wo_ref, nw_ref, w1_ref, w2_ref, out_ref):
    x1 = x_ref[...] + _dot(o_ref[...].astype(BF16), wo_ref[...])
    xn = (_rms_hat(x1) * nw_ref[...]).astype(BF16)
    acc = x1
    for j in range(D_FF // FF_CHUNK):
        h = jnp.maximum(_dot(xn, w1_ref[:, j * FF_CHUNK:(j + 1) * FF_CHUNK]), 0.0)
        acc = acc + _dot((h * h).astype(BF16), w2_ref[j * FF_CHUNK:(j + 1) * FF_CHUNK, :])
    out_ref[...] = acc


def _layer_spec(stacked, layer):
    return pl.BlockSpec((None,) + stacked.shape[1:], lambda *_: (layer, 0, 0),
                        pipeline_mode=pl.Buffered(1))


def _proj_mlp(o, x, w_out, norm_w, w1, w2, layer):
    t, d = x.shape
    tm = ROW_TILE
    return pl.pallas_call(
        _proj_mlp_kernel,
        grid=(t // tm,),
        in_specs=[pl.BlockSpec((tm, o.shape[1]), lambda i: (i, 0)),
                  pl.BlockSpec((tm, d), lambda i: (i, 0)),
                  _const_spec(w_out.shape), _const_spec((1, d)),
                  _layer_spec(w1, layer), _layer_spec(w2, layer)],
        out_specs=pl.BlockSpec((tm, d), lambda i: (i, 0)),
        out_shape=jax.ShapeDtypeStruct((t, d), F32),
        compiler_params=pltpu.CompilerParams(dimension_semantics=("parallel",),
                                             vmem_limit_bytes=VMEM_LIMIT),
        name="proj_mlp",
    )(o, x, w_out, norm_w, w1, w2)


def _rope_tab_kernel(pos_ref, freq_ref, cos_ref, s1_ref, s2_ref):
    rows = pos_ref.shape[0]
    ang = pos_ref[...].astype(F32) * freq_ref[...]
    cos = jnp.cos(ang)
    sin = jnp.sin(ang)
    lane = lax.broadcasted_iota(jnp.int32, ang.shape, 1)
    first_half = (lane % MAP_DIM) < ROPE_HALF
    group = lane // ROPE_HALF

    def spread(x, i):
        y = jnp.where(group == i, x, 0.0)
        y = y + pltpu.roll(y, ROPE_HALF, 1)
        return y + pltpu.roll(y, 2 * ROPE_HALF, 1)

    for i in range(LANES // ROPE_HALF):
        out = slice(i * rows, (i + 1) * rows)
        sin_i = spread(sin, i)
        cos_ref[out, :] = spread(cos, i)
        s1_ref[out, :] = jnp.where(first_half, -sin_i, 0.0)
        s2_ref[out, :] = jnp.where(first_half, 0.0, sin_i)


def _rope_tab(pos_packed, freq_row):
    per_row = LANES // ROPE_HALF
    rows = ROPE_TILE // per_row
    t = pos_packed.shape[0] * per_row
    tab = jax.ShapeDtypeStruct((t, LANES), F32)
    spec = pl.BlockSpec((ROPE_TILE, LANES), lambda i: (i, 0))
    return pl.pallas_call(
        _rope_tab_kernel,
        grid=(t // ROPE_TILE,),
        in_specs=[pl.BlockSpec((rows, LANES), lambda i: (i, 0)), _const_spec((1, LANES))],
        out_specs=[spec, spec, spec],
        out_shape=[tab, tab, tab],
        compiler_params=pltpu.CompilerParams(dimension_semantics=("parallel",)),
        name="rope_tab",
    )(pos_packed, freq_row)


def _attn_in_kernel(x_ref, kvn_ref, qnw_ref, wkv_ref, wq_ref, kg_ref, qg_ref,
                    cos_ref, s1_ref, s2_ref, k_ref, v_ref, q_ref, kvx_ref, qx_ref):
    width = HEADS * HEAD_DIM
    cb = 2 * MXU_DIM
    xhat = _rms_hat(x_ref[...])
    kvx_ref[...] = (xhat * kvn_ref[...]).astype(BF16)
    qx_ref[...] = (xhat * qnw_ref[...]).astype(BF16)
    cos = cos_ref[...]
    s1 = s1_ref[...]
    s2 = s2_ref[...]
    r = lax.broadcasted_iota(jnp.int32, (MXU_DIM, MXU_DIM), 0) // MAP_DIM
    c = lax.broadcasted_iota(jnp.int32, (MXU_DIM, MXU_DIM), 1) // MAP_DIM
    group_ones = (r == c).astype(BF16)

    def norm_rope(raw, gain, scale, o_ref, col):
        for s in range(raw.shape[1] // MXU_DIM):
            blk = raw[:, s * MXU_DIM:(s + 1) * MXU_DIM]
            ss = _dot((blk * blk).astype(BF16), group_ones)
            lo = col + s * MXU_DIM
            nb = blk * lax.rsqrt(ss * (1.0 / MAP_DIM) + EPS) * gain[:, lo:lo + MXU_DIM]
            for hh in range(MXU_DIM // LANES):
                xb = nb[:, hh * LANES:(hh + 1) * LANES]
                rot = xb * cos + pltpu.roll(xb, LANES - ROPE_HALF, 1) * s1 + pltpu.roll(xb, ROPE_HALF, 1) * s2
                o_ref[:, lo + hh * LANES:lo + (hh + 1) * LANES] = (rot * scale).astype(o_ref.dtype)

    plan = []
    for col in range(0, width, cb):
        plan += [("k", col), ("q", col)]
    plan += [("v", col) for col in range(0, width, cb)]

    def project(item):
        kind, col = item
        if kind == "q":
            return _dot(qx_ref[...], wq_ref[:, col:col + cb])
        base = 0 if kind == "k" else width
        return _dot(kvx_ref[...], wkv_ref[:, base + col:base + col + cb])

    raw = project(plan[0])
    for i, (kind, col) in enumerate(plan):
        raw_next = project(plan[i + 1]) if i + 1 < len(plan) else None
        if kind == "v":
            v_ref[:, col:col + cb] = raw.astype(v_ref.dtype)
        elif kind == "k":
            norm_rope(raw, kg_ref[...], 1.0, k_ref, col)
        else:
            norm_rope(raw, qg_ref[...], MAP_DIM ** -0.5 * LOG2E, q_ref, col)
        raw = raw_next


def _attn_in(x, kv_norm, q_norm_w, w_kv, w_q, k_gain, q_gain, cos, s1, s2):
    t, d = x.shape
    tm = ROW_TILE
    width = HEADS * HEAD_DIM
    act = jax.ShapeDtypeStruct((t, width), BF16)
    row_spec = pl.BlockSpec((tm, width), lambda i: (i, 0))
    tab_spec = pl.BlockSpec((tm, LANES), lambda i: (i, 0))
    return pl.pallas_call(
        _attn_in_kernel,
        grid=(t // tm,),
        in_specs=[pl.BlockSpec((tm, d), lambda i: (i, 0)),
                  _const_spec((1, d)), _const_spec((1, d)),
                  _const_spec(w_kv.shape), _const_spec(w_q.shape),
                  _const_spec((1, width)), _const_spec((1, width)),
                  tab_spec, tab_spec, tab_spec],
        out_specs=[row_spec, row_spec, row_spec],
        out_shape=[act, act, act],
        scratch_shapes=[pltpu.VMEM((tm, d), BF16), pltpu.VMEM((tm, d), BF16)],
        compiler_params=pltpu.CompilerParams(dimension_semantics=("parallel",),
                                             vmem_limit_bytes=VMEM_LIMIT),
        name="attn_in",
    )(x, kv_norm, q_norm_w, w_kv, w_q, k_gain, q_gain, cos, s1, s2)


def _diff_attn_kernel(q_ref, k_ref, v_ref, qg_ref, kg_ref, lam_ref, snw_ref, o_ref,
                      m_ref, l_ref, acc_ref, *, tq, lam_init):
    n_q = q_ref.shape[0] // tq
    piece = ATTN_DIAG_PIECE
    lane = lax.broadcasted_iota(jnp.int32, (tq, HEAD_DIM), 1)
    lp = lam_ref[...]
    lam = (jnp.exp(jnp.sum(lp[0:1] * lp[1:2], axis=-1, keepdims=True))
           - jnp.exp(jnp.sum(lp[2:3] * lp[3:4], axis=-1, keepdims=True)) + lam_init)
    snw = snw_ref[...]

    def q_maps(rows):
        q = q_ref[rows, :]
        zero = jnp.zeros_like(q)
        return (jnp.where(lane < MAP_DIM, q, zero), jnp.where(lane < MAP_DIM, zero, q))

    def scores(qm, rows):
        ks = k_ref[rows, :]
        return [lax.dot_general(x, ks, NT_DIMS, preferred_element_type=F32) for x in qm]

    def finish(rows, acc0, acc1, l0, l1):
        o = acc0 / l0 - lam * (acc1 / l1)
        o_ref[rows, :] = (_rms_hat(o) * snw * (1.0 - lam_init)).astype(o_ref.dtype)

    bound = SCORE_BOUND_COEF * jnp.max(jnp.abs(qg_ref[...])) * jnp.max(jnp.abs(kg_ref[...]))
    bounded = bound <= SCORE_BOUND_LIMIT

    @pl.when(bounded)
    def _():
        def lane_sums(p):
            out = p[:, 0:LANES]
            for b in range(1, p.shape[1] // LANES):
                out = out + p[:, b * LANES:(b + 1) * LANES]
            return out

        def plus(a, b):
            return b if a is None else a + b

        for qi in range(n_q):
            q0 = qi * tq
            qm = q_maps(slice(q0, q0 + tq))
            acc = [None, None]
            lsum = [None, None]
            s_next = scores(qm, slice(0, tq)) if qi > 0 else None
            for j in range(qi):
                s = s_next
                s_next = scores(qm, slice((j + 1) * tq, (j + 2) * tq)) if j + 1 < qi else None
                vs = v_ref[j * tq:(j + 1) * tq, :]
                for mi in range(2):
                    p = jnp.exp2(s[mi])
                    lsum[mi] = plus(lsum[mi], lane_sums(p))
                    acc[mi] = plus(acc[mi], _dot(p.astype(BF16), vs))
            for bi in range(tq // piece):
                rows = slice(bi * piece, (bi + 1) * piece)
                ncols = (bi + 1) * piece
                sd = scores([x[rows, :] for x in qm], slice(q0, q0 + ncols))
                r = lax.broadcasted_iota(jnp.int32, (piece, ncols), 0) + bi * piece
                c = lax.broadcasted_iota(jnp.int32, (piece, ncols), 1)
                vs = v_ref[q0:q0 + ncols, :]
                fin = []
                for mi in range(2):
                    p = jnp.where(r >= c, jnp.exp2(sd[mi]), 0.0)
                    below_l = None if lsum[mi] is None else lsum[mi][rows, :]
                    below_a = None if acc[mi] is None else acc[mi][rows, :]
                    fin.append((plus(below_a, _dot(p.astype(BF16), vs)),
                                jnp.sum(plus(below_l, lane_sums(p)), axis=-1, keepdims=True)))
                finish(slice(q0 + bi * piece, q0 + (bi + 1) * piece),
                       fin[0][0], fin[1][0], fin[0][1], fin[1][1])

    @pl.when(jnp.logical_not(bounded))
    def _():
        r = lax.broadcasted_iota(jnp.int32, (tq, tq), 0)
        c = lax.broadcasted_iota(jnp.int32, (tq, tq), 1)

        def q_tile(qi, carry):
            q_rows = pl.ds(pl.multiple_of(qi * tq, tq), tq)
            qm = q_maps(q_rows)
            m_ref[...] = jnp.full(m_ref.shape, NEG_INF, F32)
            l_ref[...] = jnp.zeros(l_ref.shape, F32)
            acc_ref[...] = jnp.zeros(acc_ref.shape, F32)

            def step(j, masked):
                kv_rows = pl.ds(pl.multiple_of(j * tq, tq), tq)
                s = scores(qm, kv_rows)
                vs = v_ref[kv_rows, :]
                for mi in range(2):
                    sm = jnp.where(r >= c, s[mi], NEG_INF) if masked else s[mi]
                    m_old = m_ref[mi]
                    m_new = jnp.maximum(m_old, jnp.max(sm, axis=-1, keepdims=True))
                    alpha = jnp.exp2(m_old - m_new)
                    p = jnp.exp2(sm - m_new[:, 0:1])
                    l_ref[mi] = alpha * l_ref[mi] + jnp.sum(p, axis=-1, keepdims=True)
                    acc_ref[mi] = alpha * acc_ref[mi] + _dot(p.astype(BF16), vs)
                    m_ref[mi] = m_new

            def full_body(j, carry):
                step(j, False)
                return carry

            lax.fori_loop(0, qi, full_body, 0)
            step(qi, True)
            finish(q_rows, acc_ref[0], acc_ref[1], l_ref[0], l_ref[1])
            return carry

        lax.fori_loop(0, n_q, q_tile, 0)


def _diff_attn(q, k, v, q_gain, k_gain, lam_params, sub_norm, batch, seq_len, lam_init):
    t, width = q.shape
    tq = ATTN_TILE
    seq_spec = pl.BlockSpec((seq_len, HEAD_DIM), lambda b, h: (b, h))
    return pl.pallas_call(
        functools.partial(_diff_attn_kernel, tq=tq, lam_init=lam_init),
        grid=(batch, HEADS),
        in_specs=[seq_spec, seq_spec, seq_spec,
                  _const_spec(q_gain.shape), _const_spec(k_gain.shape),
                  _const_spec(lam_params.shape), _const_spec((1, HEAD_DIM))],
        out_specs=seq_spec,
        out_shape=jax.ShapeDtypeStruct((t, width), BF16),
        scratch_shapes=[pltpu.VMEM((2, tq, LANES), F32), pltpu.VMEM((2, tq, LANES), F32),
                        pltpu.VMEM((2, tq, HEAD_DIM), F32)],
        compiler_params=pltpu.CompilerParams(dimension_semantics=("parallel", "parallel"),
                                             vmem_limit_bytes=VMEM_LIMIT),
        name="diff_attn",
    )(q, k, v, q_gain, k_gain, lam_params, sub_norm)


def kernel(x, positions, a_norm, a_w_in, a_conv_w, a_a_log, a_dt_bias, a_out_norm, a_w_out,
           kv_norm, w_kv, k_norm, b_norm, b_w_q, b_q_norm, b_lambda, b_sub_norm, b_w_out,
           mlp_norm, mlp_w1, mlp_w2):
    batch, seq_len, d = x.shape
    assert d == D_MODEL and a_norm.shape[0] == 1 and b_norm.shape[0] == 1
    assert seq_len % ROW_TILE == 0 and seq_len % ATTN_TILE == 0 and seq_len % DELTA_CHUNK == 0
    assert (batch * seq_len) % ROPE_TILE == 0
    t = batch * seq_len
    width = HEADS * HEAD_DIM
    xf = x.reshape(t, d)

    w_in = a_w_in[0]
    pad = jnp.zeros((HEADS, 1), F32)
    alog16 = jnp.concatenate([pad, a_a_log[0].reshape(HEADS, 1)], axis=0)
    dtb16 = jnp.concatenate([pad, a_dt_bias[0].reshape(HEADS, 1)], axis=0)
    q, k, v, z, gates = _gdn_in(xf, a_norm[0].reshape(1, d), w_in.astype(BF16), a_conv_w[0],
                                alog16, dtb16, seq_len)
    o = _delta(q, k, v, z, gates, a_out_norm[0].reshape(1, HEAD_DIM), batch, seq_len)
    w1_all = mlp_w1.astype(BF16)
    w2_all = mlp_w2.astype(BF16)
    xf = _proj_mlp(o, xf, a_w_out[0].astype(BF16), mlp_norm[0].reshape(1, d), w1_all, w2_all, 0)

    half = ROPE_HALF
    freqs = ROPE_THETA ** (-jnp.arange(half, dtype=F32) / half)
    freq_row = jnp.tile(freqs, LANES // half).reshape(1, LANES)
    per_row = LANES // half
    pos_packed = jnp.repeat(
        positions.reshape(t // ROPE_TILE, per_row, ROPE_TILE // per_row).transpose(0, 2, 1)
        .reshape(t // per_row, per_row), half, axis=1)
    cos, s1, s2 = _rope_tab(pos_packed, freq_row)
    k_gain = jnp.tile(k_norm, width // MAP_DIM).reshape(1, width)
    q_gain = jnp.tile(b_q_norm[0], width // MAP_DIM).reshape(1, width)
    kr, vv, qr = _attn_in(xf, kv_norm.reshape(1, d), b_norm[0].reshape(1, d),
                          w_kv.astype(BF16), b_w_q[0].astype(BF16), k_gain, q_gain, cos, s1, s2)
    lam_init = 0.8 - 0.6 * math.exp(-0.3 * 1)
    oa = _diff_attn(qr, kr, vv, b_q_norm[0].reshape(1, MAP_DIM), k_norm.reshape(1, MAP_DIM),
                    b_lambda[0], b_sub_norm[0].reshape(1, HEAD_DIM), batch, seq_len, lam_init)
    xf = _proj_mlp(oa, xf, b_w_out[0].astype(BF16), mlp_norm[1].reshape(1, d), w1_all, w2_all, 1)
    return xf.reshape(batch, seq_len, d)
```

```python
import functools
import math

import jax
import jax.numpy as jnp
from jax import lax
from jax.experimental import pallas as pl
from jax.experimental.pallas import tpu as pltpu

F32 = jnp.float32
BF16 = jnp.bfloat16

D_MODEL = 1024
HEADS = 8
HEAD_DIM = 128
MAP_DIM = 64
ROPE_HALF = MAP_DIM // 2
CONV_WIDTH = 4
D_FF = 4 * D_MODEL
ROPE_THETA = 10000.0
EPS = 1e-6
NEG_INF = -1e30
LOG2E = math.log2(math.e)
SCORE_BOUND_COEF = MAP_DIM * MAP_DIM ** -0.5 * LOG2E * 1.02
SCORE_BOUND_LIMIT = 100.0

LANES = 128
SUBLANES = 8
MXU_DIM = 256

ROW_TILE = 512
GDN_ROW_TILE = 512
GDN_COL_BLOCK = 2 * MXU_DIM
FF_CHUNK = 1024
DELTA_CHUNK = 64
DELTA_GROUPS = (4,) * 8
DELTA_HEADS = 4
ROPE_TILE = 2048
ATTN_TILE = 512
ATTN_DIAG_PIECE = 256
VMEM_LIMIT = 56 * 1024 * 1024

NT_DIMS = (((1,), (1,)), ((), ()))
TN_DIMS = (((0,), (0,)), ((), ()))


def _rms_hat(x):
    return x * lax.rsqrt(jnp.mean(x * x, axis=-1, keepdims=True) + EPS)


def _sigmoid(x):
    return 1.0 / (1.0 + jnp.exp(-x))


def _silu(x):
    h = 0.5 * x
    return h * jnp.tanh(h) + h


def _softplus(x):
    return jnp.maximum(x, 0.0) + jnp.log(1.0 + jnp.exp(-jnp.abs(x)))


def _dot(a, b):
    return jnp.dot(a, b, preferred_element_type=F32)


def _const_spec(shape):
    zeros = (0,) * len(shape)
    return pl.BlockSpec(shape, lambda *_: zeros, pipeline_mode=pl.Buffered(1))


def _gdn_in_kernel(x_ref, nw_ref, w_ref, cw_ref, alog_ref, dtb_ref,
                   q_ref, k_ref, v_ref, z_ref, gate_ref, xn_ref, tail_ref, *, tiles_per_seq):
    tm = x_ref.shape[0]
    width = HEADS * HEAD_DIM
    step = pl.program_id(0)

    @pl.when(step == 0)
    def _():
        tail_ref[...] = jnp.zeros(tail_ref.shape, F32)

    xn_ref[...] = (_rms_hat(x_ref[...]) * nw_ref[...]).astype(BF16)
    seq_start = (step % tiles_per_seq) == 0

    cb = GDN_COL_BLOCK
    blocks = 3 * width // cb

    def project(blk):
        return _dot(xn_ref[...], w_ref[:, blk * cb:(blk + 1) * cb])

    p = project(0)
    for blk in range(blocks):
        p_next = project(blk + 1) if blk + 1 < blocks else None
        cols = slice(blk * cb, (blk + 1) * cb)
        prev = jnp.where(seq_start, 0.0, tail_ref[:, cols])
        tail_ref[:, cols] = p[tm - SUBLANES:, :]
        xp = jnp.concatenate([prev, p], axis=0)
        cw = cw_ref[:, cols]
        c = cw[CONV_WIDTH - 1:CONV_WIDTH, :] * p
        for j in range(CONV_WIDTH - 1):
            c = c + cw[j:j + 1, :] * pltpu.roll(xp, CONV_WIDTH - 1 - j, 0)[SUBLANES:, :]
        kind, col = divmod(blk * cb, width)
        (q_ref, k_ref, v_ref)[kind][:, col:col + cb] = c
        p = p_next

    z_ref[...] = _dot(xn_ref[...], w_ref[:, 3 * width:4 * width])

    g_rows = _dot(xn_ref[...], w_ref[:, 4 * width:])
    eye_g = (lax.broadcasted_iota(jnp.int32, (2 * HEADS, 2 * HEADS), 0)
             == lax.broadcasted_iota(jnp.int32, (2 * HEADS, 2 * HEADS), 1)).astype(F32)
    gt = lax.dot_general(eye_g, g_rows, NT_DIMS, precision=lax.Precision.HIGHEST,
                         preferred_element_type=F32)
    beta = _sigmoid(gt)
    decay = -jnp.exp(alog_ref[...]) * _softplus(gt + dtb_ref[...])
    row = lax.broadcasted_iota(jnp.int32, gt.shape, 0)
    gate_ref[...] = jnp.where(row < HEADS, beta, decay)


def _gdn_in(x, norm_w, w_in, conv_w, alog16, dtb16, seq_len):
    t, d = x.shape
    tm = GDN_ROW_TILE
    width = HEADS * HEAD_DIM
    act = jax.ShapeDtypeStruct((t, width), F32)
    row_spec = pl.BlockSpec((tm, width), lambda i: (i, 0))
    return pl.pallas_call(
        functools.partial(_gdn_in_kernel, tiles_per_seq=seq_len // tm),
        grid=(t // tm,),
        in_specs=[
            pl.BlockSpec((tm, d), lambda i: (i, 0)),
            _const_spec((1, d)),
            _const_spec(w_in.shape),
            _const_spec(conv_w.shape),
            _const_spec(alog16.shape),
            _const_spec(dtb16.shape),
        ],
        out_specs=[row_spec, row_spec, row_spec, row_spec,
                   pl.BlockSpec((2 * HEADS, tm), lambda i: (0, i))],
        out_shape=[act, act, act, act, jax.ShapeDtypeStruct((2 * HEADS, t), F32)],
        scratch_shapes=[pltpu.VMEM((tm, d), BF16), pltpu.VMEM((SUBLANES, 3 * width), F32)],
        compiler_params=pltpu.CompilerParams(dimension_semantics=("arbitrary",),
                                             vmem_limit_bytes=VMEM_LIMIT),
        name="gdn_in",
    )(x, norm_w, w_in, conv_w, alog16, dtb16)


def _delta_kernel(q_ref, k_ref, v_ref, z_ref, beta_ref, g_ref, onw_ref, o_ref, gc_ref,
                  qw_a, b_a, op_a, cd_a, qw_b, b_b, op_b, cd_b, *, chunk, groups, heads):
    c_len = chunk
    bufs = ((qw_a, b_a, op_a, cd_a), (qw_b, b_b, op_b, cd_b))
    row = lax.broadcasted_iota(jnp.int32, (c_len, c_len), 0)
    col = lax.broadcasted_iota(jnp.int32, (c_len, c_len), 1)
    causal = row >= col
    strict = row > col
    eye = row == col
    upper = (row <= col).astype(F32)
    for hh in range(heads):
        gc_ref[hh] = jnp.dot(g_ref[hh], upper, precision=lax.Precision.HIGHEST,
                             preferred_element_type=F32)
    onw = onw_ref[...]
    n_steps = int(math.log2(c_len))

    def to_col(r):
        return jnp.sum(jnp.where(eye, r, 0.0), axis=1, keepdims=True)

    def chunk_rows(c):
        return slice(c * c_len, (c + 1) * c_len)

    def head_cols(hh):
        return slice(hh * HEAD_DIM, (hh + 1) * HEAD_DIM)

    def unit_rows(a, scale):
        return a * (lax.rsqrt(jnp.sum(a * a, axis=-1, keepdims=True) + EPS) * scale)

    def run(prep, adv, st):
        todo = list(range(adv[1])) if adv is not None else []

        def advance_one(st):
            if not todo:
                return st
            g = todo.pop(0)
            first, _, (qw_ref, b_ref, op_ref, cd_ref) = adv
            rows = chunk_rows(first + g)
            slots = [g * heads + hh for hh in range(heads)]
            res = [_dot(qw_ref[slots[hh]], st[hh].astype(BF16)) for hh in range(heads)]
            out = [res[hh][:c_len] + op_ref[slots[hh]] for hh in range(heads)]
            st = [cd_ref[slots[hh], 0:1, :] * st[hh] - res[hh][c_len:] + b_ref[slots[hh]]
                  for hh in range(heads)]
            for hh in range(heads):
                zc = z_ref[rows, head_cols(hh)]
                o_ref[rows, head_cols(hh)] = (_rms_hat(out[hh]) * onw * _silu(zc)).astype(o_ref.dtype)
            return st

        if prep is not None:
            first, size, (qw_ref, b_ref, op_ref, cd_ref) = prep
            chains = [(g, hh) for g in range(size) for hh in range(heads)]
            n = len(chains)
            cidx = [first + g for g, _ in chains]
            q = [unit_rows(_silu(q_ref[chunk_rows(cidx[i]), head_cols(chains[i][1])]),
                           HEAD_DIM ** -0.5) for i in range(n)]
            k = [unit_rows(_silu(k_ref[chunk_rows(cidx[i]), head_cols(chains[i][1])]), 1.0)
                 for i in range(n)]
            gc_r = [gc_ref[chains[i][1], pl.ds(cidx[i], 1), :] for i in range(n)]
            beta_c = [to_col(beta_ref[chains[i][1], pl.ds(cidx[i], 1), :]) for i in range(n)]
            gc_c = [to_col(r) for r in gc_r]
            gc_last = [r[:, c_len - 1:c_len] for r in gc_r]
            e_c = [jnp.exp(x) for x in gc_c]
            kb = [k[i] * beta_c[i] for i in range(n)]
            s = [lax.dot_general(jnp.concatenate([kb[i], q[i]], axis=0).astype(BF16),
                                 k[i].astype(BF16), NT_DIMS, preferred_element_type=F32)
                 for i in range(n)]
            st = advance_one(st)
            decay = [jnp.where(causal, jnp.exp(jnp.where(causal, gc_c[i] - gc_r[i], 0.0)), 0.0)
                     for i in range(n)]
            p = [jnp.where(strict, -(s[i][:c_len] * decay[i]), 0.0) for i in range(n)]
            intra = [s[i][c_len:] * decay[i] for i in range(n)]
            r = p
            p = [_dot(x.astype(BF16), x.astype(BF16)) for x in p]
            st = advance_one(st)
            for step in range(1, n_steps):
                last = step + 1 == n_steps
                p16 = [x.astype(BF16) for x in p]
                r16 = [x.astype(BF16) for x in r]
                t = [_dot(r16[i] if last else jnp.concatenate([p16[i], r16[i]], axis=0), p16[i])
                     for i in range(n)]
                r = [r[i] + p[i] + (t[i] if last else t[i][c_len:]) for i in range(n)]
                if not last:
                    p = [x[:c_len] for x in t]
                    st = advance_one(st)
            rhs = [jnp.concatenate([_silu(v_ref[chunk_rows(cidx[i]), head_cols(chains[i][1])]) * beta_c[i],
                                    kb[i] * e_c[i]], axis=1) for i in range(n)]
            y = [rhs[i] + _dot(r[i].astype(BF16), rhs[i].astype(BF16)) for i in range(n)]
            st = advance_one(st)
            y16 = [x.astype(BF16) for x in y]
            kd16 = [(k[i] * jnp.exp(gc_last[i] - gc_c[i])).astype(BF16) for i in range(n)]
            mb = [lax.dot_general(kd16[i], y16[i], TN_DIMS, preferred_element_type=F32)
                  for i in range(n)]
            iu = [_dot(intra[i].astype(BF16), y16[i]) for i in range(n)]
            st = advance_one(st)
            for i in range(n):
                b_ref[i] = mb[i][:, :HEAD_DIM]
                qw_ref[i, 0:c_len, :] = (q[i] * e_c[i] - iu[i][:, HEAD_DIM:]).astype(BF16)
                qw_ref[i, c_len:, :] = mb[i][:, HEAD_DIM:].astype(BF16)
                op_ref[i] = iu[i][:, :HEAD_DIM]
                cd_ref[i] = jnp.broadcast_to(jnp.exp(gc_last[i]), (SUBLANES, LANES))
        while todo:
            st = advance_one(st)
        return st

    firsts = [sum(groups[:j]) for j in range(len(groups))]
    plan = [(firsts[j], groups[j], bufs[j % 2]) for j in range(len(groups))]
    st = [jnp.zeros((HEAD_DIM, HEAD_DIM), F32) for _ in range(heads)]
    for j in range(len(plan) + 1):
        st = run(plan[j] if j < len(plan) else None, plan[j - 1] if j > 0 else None, st)


def _delta(q, k, v, z, gates, out_norm, batch, seq_len):
    t, width = q.shape
    c_len = DELTA_CHUNK
    nh = DELTA_HEADS
    n_chunks = seq_len // c_len
    assert sum(DELTA_GROUPS) == n_chunks and HEADS % nh == 0
    gates3 = gates.reshape(2 * HEADS, batch * n_chunks, c_len)
    seq_spec = pl.BlockSpec((seq_len, nh * HEAD_DIM), lambda b, h: (b, h))
    slots = max(DELTA_GROUPS) * nh
    buf_set = [pltpu.VMEM((slots, c_len + HEAD_DIM, HEAD_DIM), BF16),
               pltpu.VMEM((slots, HEAD_DIM, HEAD_DIM), F32),
               pltpu.VMEM((slots, c_len, HEAD_DIM), F32),
               pltpu.VMEM((slots, SUBLANES, LANES), F32)]
    return pl.pallas_call(
        functools.partial(_delta_kernel, chunk=c_len, groups=DELTA_GROUPS, heads=nh),
        grid=(batch, HEADS // nh),
        in_specs=[seq_spec, seq_spec, seq_spec, seq_spec,
                  pl.BlockSpec((nh, n_chunks, c_len), lambda b, h: (h, b, 0)),
                  pl.BlockSpec((nh, n_chunks, c_len), lambda b, h: (HEADS // nh + h, b, 0)),
                  pl.BlockSpec((1, HEAD_DIM), lambda b, h: (0, 0))],
        out_specs=seq_spec,
        out_shape=jax.ShapeDtypeStruct((t, width), BF16),
        scratch_shapes=[pltpu.VMEM((nh, n_chunks, c_len), F32)] + buf_set + buf_set,
        compiler_params=pltpu.CompilerParams(dimension_semantics=("parallel", "parallel"),
                                             vmem_limit_bytes=VMEM_LIMIT),
        name="delta",
    )(q, k, v, z, gates3, gates3, out_norm)


def _proj_mlp_kernel(o_ref, x_ref, wo_ref, nw_ref, w1_hbm, w2_hbm, out_ref, w1_ref, w2_ref, sem,
                     *, layer):
    chunks = D_FF // FF_CHUNK

    def copies(j):
        span = pl.ds(j * FF_CHUNK, FF_CHUNK)
        return (pltpu.make_async_copy(w1_hbm.at[layer, :, span], w1_ref.at[:, span], sem.at[0, j]),
                pltpu.make_async_copy(w2_hbm.at[layer, span, :], w2_ref.at[span, :], sem.at[1, j]))

    def body(fetch):
        if fetch:
            for j in range(chunks):
                for c in copies(j):
                    c.start()
        x1 = x_ref[...] + _dot(o_ref[...].astype(BF16), wo_ref[...])
        xn = (_rms_hat(x1) * nw_ref[...]).astype(BF16)
        acc = x1
        for j in range(chunks):
            if fetch:
                for c in copies(j):
                    c.wait()
            h = jnp.maximum(_dot(xn, w1_ref[:, j * FF_CHUNK:(j + 1) * FF_CHUNK]), 0.0)
            acc = acc + _dot((h * h).astype(BF16), w2_ref[j * FF_CHUNK:(j + 1) * FF_CHUNK, :])
        out_ref[...] = acc

    first = pl.program_id(0) == 0
    pl.when(first)(lambda: body(True))
    pl.when(jnp.logical_not(first))(lambda: body(False))


def _proj_mlp(o, x, w_out, norm_w, w1, w2, layer):
    t, d = x.shape
    tm = ROW_TILE
    return pl.pallas_call(
        functools.partial(_proj_mlp_kernel, layer=layer),
        grid=(t // tm,),
        in_specs=[pl.BlockSpec((tm, o.shape[1]), lambda i: (i, 0)),
                  pl.BlockSpec((tm, d), lambda i: (i, 0)),
                  _const_spec(w_out.shape), _const_spec((1, d)),
                  pl.BlockSpec(memory_space=pl.ANY), pl.BlockSpec(memory_space=pl.ANY)],
        out_specs=pl.BlockSpec((tm, d), lambda i: (i, 0)),
        out_shape=jax.ShapeDtypeStruct((t, d), F32),
        scratch_shapes=[pltpu.VMEM(w1.shape[1:], BF16), pltpu.VMEM(w2.shape[1:], BF16),
                        pltpu.SemaphoreType.DMA((2, D_FF // FF_CHUNK))],
        compiler_params=pltpu.CompilerParams(dimension_semantics=("arbitrary",),
                                             vmem_limit_bytes=VMEM_LIMIT),
        name="proj_mlp",
    )(o, x, w_out, norm_w, w1, w2)


def _rope_tab_kernel(pos_ref, freq_ref, cos_ref, s1_ref, s2_ref):
    rows = pos_ref.shape[0]
    ang = pos_ref[...].astype(F32) * freq_ref[...]
    cos = jnp.cos(ang)
    sin = jnp.sin(ang)
    lane = lax.broadcasted_iota(jnp.int32, ang.shape, 1)
    first_half = (lane % MAP_DIM) < ROPE_HALF
    group = lane // ROPE_HALF

    def spread(x, i):
        y = jnp.where(group == i, x, 0.0)
        y = y + pltpu.roll(y, ROPE_HALF, 1)
        return y + pltpu.roll(y, 2 * ROPE_HALF, 1)

    for i in range(LANES // ROPE_HALF):
        out = slice(i * rows, (i + 1) * rows)
        sin_i = spread(sin, i)
        cos_ref[out, :] = spread(cos, i)
        s1_ref[out, :] = jnp.where(first_half, -sin_i, 0.0)
        s2_ref[out, :] = jnp.where(first_half, 0.0, sin_i)


def _rope_tab(pos_packed, freq_row):
    per_row = LANES // ROPE_HALF
    rows = ROPE_TILE // per_row
    t = pos_packed.shape[0] * per_row
    tab = jax.ShapeDtypeStruct((t, LANES), F32)
    spec = pl.BlockSpec((ROPE_TILE, LANES), lambda i: (i, 0))
    return pl.pallas_call(
        _rope_tab_kernel,
        grid=(t // ROPE_TILE,),
        in_specs=[pl.BlockSpec((rows, LANES), lambda i: (i, 0)), _const_spec((1, LANES))],
        out_specs=[spec, spec, spec],
        out_shape=[tab, tab, tab],
        compiler_params=pltpu.CompilerParams(dimension_semantics=("parallel",)),
        name="rope_tab",
    )(pos_packed, freq_row)


def _attn_in_kernel(x_ref, kvn_ref, qnw_ref, wkv_ref, wq_ref, kg_ref, qg_ref,
                    cos_ref, s1_ref, s2_ref, k_ref, v_ref, q_ref, kvx_ref, qx_ref):
    width = HEADS * HEAD_DIM
    cb = 2 * MXU_DIM
    xhat = _rms_hat(x_ref[...])
    kvx_ref[...] = (xhat * kvn_ref[...]).astype(BF16)
    qx_ref[...] = (xhat * qnw_ref[...]).astype(BF16)
    cos = cos_ref[...]
    s1 = s1_ref[...]
    s2 = s2_ref[...]
    r = lax.broadcasted_iota(jnp.int32, (MXU_DIM, MXU_DIM), 0) // MAP_DIM
    c = lax.broadcasted_iota(jnp.int32, (MXU_DIM, MXU_DIM), 1) // MAP_DIM
    group_ones = (r == c).astype(BF16)

    def norm_rope(raw, gain, scale, o_ref, col):
        for s in range(raw.shape[1] // MXU_DIM):
            blk = raw[:, s * MXU_DIM:(s + 1) * MXU_DIM]
            ss = _dot((blk * blk).astype(BF16), group_ones)
            lo = col + s * MXU_DIM
            nb = blk * lax.rsqrt(ss * (1.0 / MAP_DIM) + EPS) * gain[:, lo:lo + MXU_DIM]
            for hh in range(MXU_DIM // LANES):
                xb = nb[:, hh * LANES:(hh + 1) * LANES]
                rot = xb * cos + pltpu.roll(xb, LANES - ROPE_HALF, 1) * s1 + pltpu.roll(xb, ROPE_HALF, 1) * s2
                o_ref[:, lo + hh * LANES:lo + (hh + 1) * LANES] = (rot * scale).astype(o_ref.dtype)

    plan = []
    for col in range(0, width, cb):
        plan += [("k", col), ("q", col)]
    plan += [("v", col) for col in range(0, width, cb)]

    def project(item):
        kind, col = item
        if kind == "q":
            return _dot(qx_ref[...], wq_ref[:, col:col + cb])
        base = 0 if kind == "k" else width
        return _dot(kvx_ref[...], wkv_ref[:, base + col:base + col + cb])

    raw = project(plan[0])
    for i, (kind, col) in enumerate(plan):
        raw_next = project(plan[i + 1]) if i + 1 < len(plan) else None
        if kind == "v":
            v_ref[:, col:col + cb] = raw.astype(v_ref.dtype)
        elif kind == "k":
            norm_rope(raw, kg_ref[...], 1.0, k_ref, col)
        else:
            norm_rope(raw, qg_ref[...], MAP_DIM ** -0.5 * LOG2E, q_ref, col)
        raw = raw_next


def _attn_in(x, kv_norm, q_norm_w, w_kv, w_q, k_gain, q_gain, cos, s1, s2):
    t, d = x.shape
    tm = ROW_TILE
    width = HEADS * HEAD_DIM
    act = jax.ShapeDtypeStruct((t, width), BF16)
    row_spec = pl.BlockSpec((tm, width), lambda i: (i, 0))
    tab_spec = pl.BlockSpec((tm, LANES), lambda i: (i, 0))
    return pl.pallas_call(
        _attn_in_kernel,
        grid=(t // tm,),
        in_specs=[pl.BlockSpec((tm, d), lambda i: (i, 0)),
                  _const_spec((1, d)), _const_spec((1, d)),
                  _const_spec(w_kv.shape), _const_spec(w_q.shape),
                  _const_spec((1, width)), _const_spec((1, width)),
                  tab_spec, tab_spec, tab_spec],
        out_specs=[row_spec, row_spec, row_spec],
        out_shape=[act, act, act],
        scratch_shapes=[pltpu.VMEM((tm, d), BF16), pltpu.VMEM((tm, d), BF16)],
        compiler_params=pltpu.CompilerParams(dimension_semantics=("parallel",),
                                             vmem_limit_bytes=VMEM_LIMIT),
        name="attn_in",
    )(x, kv_norm, q_norm_w, w_kv, w_q, k_gain, q_gain, cos, s1, s2)


def _diff_attn_kernel(q_ref, k_ref, v_ref, qg_ref, kg_ref, lam_ref, snw_ref, o_ref,
                      m_ref, l_ref, acc_ref, *, tq, lam_init):
    n_q = q_ref.shape[0] // tq
    piece = ATTN_DIAG_PIECE
    lane = lax.broadcasted_iota(jnp.int32, (tq, HEAD_DIM), 1)
    lp = lam_ref[...]
    lam = (jnp.exp(jnp.sum(lp[0:1] * lp[1:2], axis=-1, keepdims=True))
           - jnp.exp(jnp.sum(lp[2:3] * lp[3:4], axis=-1, keepdims=True)) + lam_init)
    snw = snw_ref[...]

    def q_maps(rows):
        q = q_ref[rows, :]
        zero = jnp.zeros_like(q)
        return (jnp.where(lane < MAP_DIM, q, zero), jnp.where(lane < MAP_DIM, zero, q))

    def scores(qm, rows):
        ks = k_ref[rows, :]
        return [lax.dot_general(x, ks, NT_DIMS, preferred_element_type=F32) for x in qm]

    def finish(rows, acc0, acc1, l0, l1):
        o = acc0 / l0 - lam * (acc1 / l1)
        o_ref[rows, :] = (_rms_hat(o) * snw * (1.0 - lam_init)).astype(o_ref.dtype)

    bound = SCORE_BOUND_COEF * jnp.max(jnp.abs(qg_ref[...])) * jnp.max(jnp.abs(kg_ref[...]))
    bounded = bound <= SCORE_BOUND_LIMIT

    @pl.when(bounded)
    def _():
        def lane_sums(p):
            out = p[:, 0:LANES]
            for b in range(1, p.shape[1] // LANES):
                out = out + p[:, b * LANES:(b + 1) * LANES]
            return out

        def plus(a, b):
            return b if a is None else a + b

        for qi in range(n_q):
            q0 = qi * tq
            qm = q_maps(slice(q0, q0 + tq))
            acc = [None, None]
            lsum = [None, None]
            s_next = scores(qm, slice(0, tq)) if qi > 0 else None
            for j in range(qi):
                s = s_next
                s_next = scores(qm, slice((j + 1) * tq, (j + 2) * tq)) if j + 1 < qi else None
                vs = v_ref[j * tq:(j + 1) * tq, :]
                for mi in range(2):
                    p = jnp.exp2(s[mi])
                    lsum[mi] = plus(lsum[mi], lane_sums(p))
                    acc[mi] = plus(acc[mi], _dot(p.astype(BF16), vs))
            for bi in range(tq // piece):
                rows = slice(bi * piece, (bi + 1) * piece)
                ncols = (bi + 1) * piece
                sd = scores([x[rows, :] for x in qm], slice(q0, q0 + ncols))
                r = lax.broadcasted_iota(jnp.int32, (piece, ncols), 0) + bi * piece
                c = lax.broadcasted_iota(jnp.int32, (piece, ncols), 1)
                vs = v_ref[q0:q0 + ncols, :]
                fin = []
                for mi in range(2):
                    p = jnp.where(r >= c, jnp.exp2(sd[mi]), 0.0)
                    below_l = None if lsum[mi] is None else lsum[mi][rows, :]
                    below_a = None if acc[mi] is None else acc[mi][rows, :]
                    fin.append((plus(below_a, _dot(p.astype(BF16), vs)),
                                jnp.sum(plus(below_l, lane_sums(p)), axis=-1, keepdims=True)))
                finish(slice(q0 + bi * piece, q0 + (bi + 1) * piece),
                       fin[0][0], fin[1][0], fin[0][1], fin[1][1])

    @pl.when(jnp.logical_not(bounded))
    def _():
        r = lax.broadcasted_iota(jnp.int32, (tq, tq), 0)
        c = lax.broadcasted_iota(jnp.int32, (tq, tq), 1)

        def q_tile(qi, carry):
            q_rows = pl.ds(pl.multiple_of(qi * tq, tq), tq)
            qm = q_maps(q_rows)
            m_ref[...] = jnp.full(m_ref.shape, NEG_INF, F32)
            l_ref[...] = jnp.zeros(l_ref.shape, F32)
            acc_ref[...] = jnp.zeros(acc_ref.shape, F32)

            def step(j, masked):
                kv_rows = pl.ds(pl.multiple_of(j * tq, tq), tq)
                s = scores(qm, kv_rows)
                vs = v_ref[kv_rows, :]
                for mi in range(2):
                    sm = jnp.where(r >= c, s[mi], NEG_INF) if masked else s[mi]
                    m_old = m_ref[mi]
                    m_new = jnp.maximum(m_old, jnp.max(sm, axis=-1, keepdims=True))
                    alpha = jnp.exp2(m_old - m_new)
                    p = jnp.exp2(sm - m_new[:, 0:1])
                    l_ref[mi] = alpha * l_ref[mi] + jnp.sum(p, axis=-1, keepdims=True)
                    acc_ref[mi] = alpha * acc_ref[mi] + _dot(p.astype(BF16), vs)
                    m_ref[mi] = m_new

            def full_body(j, carry):
                step(j, False)
                return carry

            lax.fori_loop(0, qi, full_body, 0)
            step(qi, True)
            finish(q_rows, acc_ref[0], acc_ref[1], l_ref[0], l_ref[1])
            return carry

        lax.fori_loop(0, n_q, q_tile, 0)


def _diff_attn(q, k, v, q_gain, k_gain, lam_params, sub_norm, batch, seq_len, lam_init):
    t, width = q.shape
    tq = ATTN_TILE
    seq_spec = pl.BlockSpec((seq_len, HEAD_DIM), lambda b, h: (b, h))
    return pl.pallas_call(
        functools.partial(_diff_attn_kernel, tq=tq, lam_init=lam_init),
        grid=(batch, HEADS),
        in_specs=[seq_spec, seq_spec, seq_spec,
                  _const_spec(q_gain.shape), _const_spec(k_gain.shape),
                  _const_spec(lam_params.shape), _const_spec((1, HEAD_DIM))],
        out_specs=seq_spec,
        out_shape=jax.ShapeDtypeStruct((t, width), BF16),
        scratch_shapes=[pltpu.VMEM((2, tq, LANES), F32), pltpu.VMEM((2, tq, LANES), F32),
                        pltpu.VMEM((2, tq, HEAD_DIM), F32)],
        compiler_params=pltpu.CompilerParams(dimension_semantics=("parallel", "parallel"),
                                             vmem_limit_bytes=VMEM_LIMIT),
        name="diff_attn",
    )(q, k, v, q_gain, k_gain, lam_params, sub_norm)


def kernel(x, positions, a_norm, a_w_in, a_conv_w, a_a_log, a_dt_bias, a_out_norm, a_w_out,
           kv_norm, w_kv, k_norm, b_norm, b_w_q, b_q_norm, b_lambda, b_sub_norm, b_w_out,
           mlp_norm, mlp_w1, mlp_w2):
    batch, seq_len, d = x.shape
    assert d == D_MODEL and a_norm.shape[0] == 1 and b_norm.shape[0] == 1
    assert seq_len % ROW_TILE == 0 and seq_len % ATTN_TILE == 0 and seq_len % DELTA_CHUNK == 0
    assert (batch * seq_len) % ROPE_TILE == 0
    t = batch * seq_len
    width = HEADS * HEAD_DIM
    xf = x.reshape(t, d)

    w_in = a_w_in[0]
    pad = jnp.zeros((HEADS, 1), F32)
    alog16 = jnp.concatenate([pad, a_a_log[0].reshape(HEADS, 1)], axis=0)
    dtb16 = jnp.concatenate([pad, a_dt_bias[0].reshape(HEADS, 1)], axis=0)
    q, k, v, z, gates = _gdn_in(xf, a_norm[0].reshape(1, d), w_in.astype(BF16), a_conv_w[0],
                                alog16, dtb16, seq_len)
    o = _delta(q, k, v, z, gates, a_out_norm[0].reshape(1, HEAD_DIM), batch, seq_len)
    w1_all = mlp_w1.astype(BF16)
    w2_all = mlp_w2.astype(BF16)
    xf = _proj_mlp(o, xf, a_w_out[0].astype(BF16), mlp_norm[0].reshape(1, d), w1_all, w2_all, 0)

    half = ROPE_HALF
    freqs = ROPE_THETA ** (-jnp.arange(half, dtype=F32) / half)
    freq_row = jnp.tile(freqs, LANES // half).reshape(1, LANES)
    per_row = LANES // half
    pos_packed = jnp.repeat(
        positions.reshape(t // ROPE_TILE, per_row, ROPE_TILE // per_row).transpose(0, 2, 1)
        .reshape(t // per_row, per_row), half, axis=1)
    cos, s1, s2 = _rope_tab(pos_packed, freq_row)
    k_gain = jnp.tile(k_norm, width // MAP_DIM).reshape(1, width)
    q_gain = jnp.tile(b_q_norm[0], width // MAP_DIM).reshape(1, width)
    kr, vv, qr = _attn_in(xf, kv_norm.reshape(1, d), b_norm[0].reshape(1, d),
                          w_kv.astype(BF16), b_w_q[0].astype(BF16), k_gain, q_gain, cos, s1, s2)
    lam_init = 0.8 - 0.6 * math.exp(-0.3 * 1)
    oa = _diff_attn(qr, kr, vv, b_q_norm[0].reshape(1, MAP_DIM), k_norm.reshape(1, MAP_DIM),
                    b_lambda[0], b_sub_norm[0].reshape(1, HEAD_DIM), batch, seq_len, lam_init)
    xf = _proj_mlp(oa, xf, b_w_out[0].astype(BF16), mlp_norm[1].reshape(1, d), w1_all, w2_all, 1)
    return xf.reshape(batch, seq_len, d)
```

```python
import functools
import math

import jax
import jax.numpy as jnp
from jax import lax
from jax.experimental import pallas as pl
from jax.experimental.pallas import tpu as pltpu

F32 = jnp.float32
BF16 = jnp.bfloat16

D_MODEL = 1024
HEADS = 8
HEAD_DIM = 128
MAP_DIM = 64
ROPE_HALF = MAP_DIM // 2
CONV_WIDTH = 4
D_FF = 4 * D_MODEL
ROPE_THETA = 10000.0
EPS = 1e-6
NEG_INF = -1e30
LOG2E = math.log2(math.e)
SCORE_BOUND_COEF = MAP_DIM * MAP_DIM ** -0.5 * LOG2E * 1.02
SCORE_BOUND_LIMIT = 100.0

LANES = 128
SUBLANES = 8
MXU_DIM = 256

ROW_TILE = 512
GDN_ROW_TILE = 512
GDN_COL_BLOCK = 2 * MXU_DIM
FF_CHUNK = 1024
DELTA_CHUNK = 64
DELTA_GROUPS = (4,) * 8
DELTA_HEADS = 4
ROPE_TILE = 2048
ATTN_TILE = 512
ATTN_DIAG_PIECE = 256
VMEM_LIMIT = 56 * 1024 * 1024

NT_DIMS = (((1,), (1,)), ((), ()))
TN_DIMS = (((0,), (0,)), ((), ()))


def _rms_hat(x):
    return x * lax.rsqrt(jnp.mean(x * x, axis=-1, keepdims=True) + EPS)


def _sigmoid(x):
    return 1.0 / (1.0 + jnp.exp(-x))


def _silu(x):
    h = 0.5 * x
    return h * jnp.tanh(h) + h


def _softplus(x):
    return jnp.maximum(x, 0.0) + jnp.log(1.0 + jnp.exp(-jnp.abs(x)))


def _dot(a, b):
    return jnp.dot(a, b, preferred_element_type=F32)


def _const_spec(shape):
    zeros = (0,) * len(shape)
    return pl.BlockSpec(shape, lambda *_: zeros, pipeline_mode=pl.Buffered(1))


def _gdn_in_kernel(x_ref, nw_ref, w_ref, cw_ref, alog_ref, dtb_ref,
                   q_ref, k_ref, v_ref, z_ref, gate_ref, xn_ref, tail_ref, *, tiles_per_seq):
    tm = x_ref.shape[0]
    width = HEADS * HEAD_DIM
    step = pl.program_id(0)

    @pl.when(step == 0)
    def _():
        tail_ref[...] = jnp.zeros(tail_ref.shape, F32)

    xn_ref[...] = (_rms_hat(x_ref[...]) * nw_ref[...]).astype(BF16)
    seq_start = (step % tiles_per_seq) == 0

    cb = GDN_COL_BLOCK
    blocks = 3 * width // cb

    def project(blk):
        return _dot(xn_ref[...], w_ref[:, blk * cb:(blk + 1) * cb])

    p = project(0)
    for blk in range(blocks):
        p_next = project(blk + 1) if blk + 1 < blocks else None
        cols = slice(blk * cb, (blk + 1) * cb)
        prev = jnp.where(seq_start, 0.0, tail_ref[:, cols])
        tail_ref[:, cols] = p[tm - SUBLANES:, :]
        xp = jnp.concatenate([prev, p], axis=0)
        cw = cw_ref[:, cols]
        c = cw[CONV_WIDTH - 1:CONV_WIDTH, :] * p
        for j in range(CONV_WIDTH - 1):
            c = c + cw[j:j + 1, :] * pltpu.roll(xp, CONV_WIDTH - 1 - j, 0)[SUBLANES:, :]
        kind, col = divmod(blk * cb, width)
        (q_ref, k_ref, v_ref)[kind][:, col:col + cb] = c
        p = p_next

    z_ref[...] = _dot(xn_ref[...], w_ref[:, 3 * width:4 * width])

    g_rows = _dot(xn_ref[...], w_ref[:, 4 * width:])
    eye_g = (lax.broadcasted_iota(jnp.int32, (2 * HEADS, 2 * HEADS), 0)
             == lax.broadcasted_iota(jnp.int32, (2 * HEADS, 2 * HEADS), 1)).astype(F32)
    gt = lax.dot_general(eye_g, g_rows, NT_DIMS, precision=lax.Precision.HIGHEST,
                         preferred_element_type=F32)
    beta = _sigmoid(gt)
    decay = -jnp.exp(alog_ref[...]) * _softplus(gt + dtb_ref[...])
    row = lax.broadcasted_iota(jnp.int32, gt.shape, 0)
    gate_ref[...] = jnp.where(row < HEADS, beta, decay)


def _gdn_in(x, norm_w, w_in, conv_w, alog16, dtb16, seq_len):
    t, d = x.shape
    tm = GDN_ROW_TILE
    width = HEADS * HEAD_DIM
    act = jax.ShapeDtypeStruct((t, width), F32)
    row_spec = pl.BlockSpec((tm, width), lambda i: (i, 0))
    return pl.pallas_call(
        functools.partial(_gdn_in_kernel, tiles_per_seq=seq_len // tm),
        grid=(t // tm,),
        in_specs=[
            pl.BlockSpec((tm, d), lambda i: (i, 0)),
            _const_spec((1, d)),
            _const_spec(w_in.shape),
            _const_spec(conv_w.shape),
            _const_spec(alog16.shape),
            _const_spec(dtb16.shape),
        ],
        out_specs=[row_spec, row_spec, row_spec, row_spec,
                   pl.BlockSpec((2 * HEADS, tm), lambda i: (0, i))],
        out_shape=[act, act, act, act, jax.ShapeDtypeStruct((2 * HEADS, t), F32)],
        scratch_shapes=[pltpu.VMEM((tm, d), BF16), pltpu.VMEM((SUBLANES, 3 * width), F32)],
        compiler_params=pltpu.CompilerParams(dimension_semantics=("arbitrary",),
                                             vmem_limit_bytes=VMEM_LIMIT),
        name="gdn_in",
    )(x, norm_w, w_in, conv_w, alog16, dtb16)


def _delta_kernel(q_ref, k_ref, v_ref, z_ref, beta_ref, g_ref, onw_ref, o_ref, gc_ref,
                  qw_a, b_a, op_a, cd_a, qw_b, b_b, op_b, cd_b, *, chunk, groups, heads):
    c_len = chunk
    bufs = ((qw_a, b_a, op_a, cd_a), (qw_b, b_b, op_b, cd_b))
    row = lax.broadcasted_iota(jnp.int32, (c_len, c_len), 0)
    col = lax.broadcasted_iota(jnp.int32, (c_len, c_len), 1)
    causal = row >= col
    strict = row > col
    eye = row == col
    upper = (row <= col).astype(F32)
    for hh in range(heads):
        gc_ref[hh] = jnp.dot(g_ref[hh], upper, precision=lax.Precision.HIGHEST,
                             preferred_element_type=F32)
    onw = onw_ref[...]
    n_steps = int(math.log2(c_len))

    def to_col(r):
        return jnp.sum(jnp.where(eye, r, 0.0), axis=1, keepdims=True)

    def chunk_rows(c):
        return slice(c * c_len, (c + 1) * c_len)

    def head_cols(hh):
        return slice(hh * HEAD_DIM, (hh + 1) * HEAD_DIM)

    def unit_rows(a, scale):
        return a * (lax.rsqrt(jnp.sum(a * a, axis=-1, keepdims=True) + EPS) * scale)

    def run(prep, adv, st):
        todo = list(range(adv[1])) if adv is not None else []

        def advance_one(st):
            if not todo:
                return st
            g = todo.pop(0)
            first, _, (qw_ref, b_ref, op_ref, cd_ref) = adv
            rows = chunk_rows(first + g)
            slots = [g * heads + hh for hh in range(heads)]
            res = [_dot(qw_ref[slots[hh]], st[hh].astype(BF16)) for hh in range(heads)]
            out = [res[hh][:c_len] + op_ref[slots[hh]] for hh in range(heads)]
            st = [cd_ref[slots[hh], 0:1, :] * st[hh] - res[hh][c_len:] + b_ref[slots[hh]]
                  for hh in range(heads)]
            for hh in range(heads):
                zc = z_ref[rows, head_cols(hh)]
                o_ref[rows, head_cols(hh)] = (_rms_hat(out[hh]) * onw * _silu(zc)).astype(o_ref.dtype)
            return st

        if prep is not None:
            first, size, (qw_ref, b_ref, op_ref, cd_ref) = prep
            chains = [(g, hh) for g in range(size) for hh in range(heads)]
            n = len(chains)
            cidx = [first + g for g, _ in chains]
            q = [unit_rows(_silu(q_ref[chunk_rows(cidx[i]), head_cols(chains[i][1])]),
                           HEAD_DIM ** -0.5) for i in range(n)]
            k = [unit_rows(_silu(k_ref[chunk_rows(cidx[i]), head_cols(chains[i][1])]), 1.0)
                 for i in range(n)]
            gc_r = [gc_ref[chains[i][1], pl.ds(cidx[i], 1), :] for i in range(n)]
            beta_c = [to_col(beta_ref[chains[i][1], pl.ds(cidx[i], 1), :]) for i in range(n)]
            gc_c = [to_col(r) for r in gc_r]
            gc_last = [r[:, c_len - 1:c_len] for r in gc_r]
            e_c = [jnp.exp(x) for x in gc_c]
            kb = [k[i] * beta_c[i] for i in range(n)]
            s = [lax.dot_general(jnp.concatenate([kb[i], q[i]], axis=0).astype(BF16),
                                 k[i].astype(BF16), NT_DIMS, preferred_element_type=F32)
                 for i in range(n)]
            st = advance_one(st)
            decay = [jnp.where(causal, jnp.exp(jnp.where(causal, gc_c[i] - gc_r[i], 0.0)), 0.0)
                     for i in range(n)]
            p = [jnp.where(strict, -(s[i][:c_len] * decay[i]), 0.0) for i in range(n)]
            intra = [s[i][c_len:] * decay[i] for i in range(n)]
            r = p
            p = [_dot(x.astype(BF16), x.astype(BF16)) for x in p]
            st = advance_one(st)
            for step in range(1, n_steps):
                last = step + 1 == n_steps
                p16 = [x.astype(BF16) for x in p]
                r16 = [x.astype(BF16) for x in r]
                t = [_dot(r16[i] if last else jnp.concatenate([p16[i], r16[i]], axis=0), p16[i])
                     for i in range(n)]
                r = [r[i] + p[i] + (t[i] if last else t[i][c_len:]) for i in range(n)]
                if not last:
                    p = [x[:c_len] for x in t]
                    st = advance_one(st)
            rhs = [jnp.concatenate([_silu(v_ref[chunk_rows(cidx[i]), head_cols(chains[i][1])]) * beta_c[i],
                                    kb[i] * e_c[i]], axis=1) for i in range(n)]
            y = [rhs[i] + _dot(r[i].astype(BF16), rhs[i].astype(BF16)) for i in range(n)]
            st = advance_one(st)
            y16 = [x.astype(BF16) for x in y]
            kd16 = [(k[i] * jnp.exp(gc_last[i] - gc_c[i])).astype(BF16) for i in range(n)]
            mb = [lax.dot_general(kd16[i], y16[i], TN_DIMS, preferred_element_type=F32)
                  for i in range(n)]
            iu = [_dot(intra[i].astype(BF16), y16[i]) for i in range(n)]
            st = advance_one(st)
            for i in range(n):
                b_ref[i] = mb[i][:, :HEAD_DIM]
                qw_ref[i, 0:c_len, :] = (q[i] * e_c[i] - iu[i][:, HEAD_DIM:]).astype(BF16)
                qw_ref[i, c_len:, :] = mb[i][:, HEAD_DIM:].astype(BF16)
                op_ref[i] = iu[i][:, :HEAD_DIM]
                cd_ref[i] = jnp.broadcast_to(jnp.exp(gc_last[i]), (SUBLANES, LANES))
        while todo:
            st = advance_one(st)
        return st

    firsts = [sum(groups[:j]) for j in range(len(groups))]
    plan = [(firsts[j], groups[j], bufs[j % 2]) for j in range(len(groups))]
    st = [jnp.zeros((HEAD_DIM, HEAD_DIM), F32) for _ in range(heads)]
    for j in range(len(plan) + 1):
        st = run(plan[j] if j < len(plan) else None, plan[j - 1] if j > 0 else None, st)


def _delta(q, k, v, z, gates, out_norm, batch, seq_len):
    t, width = q.shape
    c_len = DELTA_CHUNK
    nh = DELTA_HEADS
    n_chunks = seq_len // c_len
    assert sum(DELTA_GROUPS) == n_chunks and HEADS % nh == 0
    gates3 = gates.reshape(2 * HEADS, batch * n_chunks, c_len)
    seq_spec = pl.BlockSpec((seq_len, nh * HEAD_DIM), lambda b, h: (b, h))
    slots = max(DELTA_GROUPS) * nh
    buf_set = [pltpu.VMEM((slots, c_len + HEAD_DIM, HEAD_DIM), BF16),
               pltpu.VMEM((slots, HEAD_DIM, HEAD_DIM), F32),
               pltpu.VMEM((slots, c_len, HEAD_DIM), F32),
               pltpu.VMEM((slots, SUBLANES, LANES), F32)]
    return pl.pallas_call(
        functools.partial(_delta_kernel, chunk=c_len, groups=DELTA_GROUPS, heads=nh),
        grid=(batch, HEADS // nh),
        in_specs=[seq_spec, seq_spec, seq_spec, seq_spec,
                  pl.BlockSpec((nh, n_chunks, c_len), lambda b, h: (h, b, 0)),
                  pl.BlockSpec((nh, n_chunks, c_len), lambda b, h: (HEADS // nh + h, b, 0)),
                  pl.BlockSpec((1, HEAD_DIM), lambda b, h: (0, 0))],
        out_specs=seq_spec,
        out_shape=jax.ShapeDtypeStruct((t, width), BF16),
        scratch_shapes=[pltpu.VMEM((nh, n_chunks, c_len), F32)] + buf_set + buf_set,
        compiler_params=pltpu.CompilerParams(dimension_semantics=("parallel", "parallel"),
                                             vmem_limit_bytes=VMEM_LIMIT),
        name="delta",
    )(q, k, v, z, gates3, gates3, out_norm)


def _proj_mlp_kernel(o_ref, x_ref, wo_ref, nw_ref, w1_hbm, w2_hbm, out_ref, w1_ref, w2_ref, sem,
                     *, layer):
    chunks = D_FF // FF_CHUNK

    def copies(j):
        span = pl.ds(j * FF_CHUNK, FF_CHUNK)
        return (pltpu.make_async_copy(w1_hbm.at[layer, :, span], w1_ref.at[:, span], sem.at[0, j]),
                pltpu.make_async_copy(w2_hbm.at[layer, span, :], w2_ref.at[span, :], sem.at[1, j]))

    def body(fetch):
        if fetch:
            for c in copies(0):
                c.start()
        x1 = x_ref[...] + _dot(o_ref[...].astype(BF16), wo_ref[...])
        xn = (_rms_hat(x1) * nw_ref[...]).astype(BF16)
        acc = x1
        for j in range(chunks):
            if fetch:
                for c in copies(j):
                    c.wait()
                if j + 1 < chunks:
                    for c in copies(j + 1):
                        c.start()
            h = jnp.maximum(_dot(xn, w1_ref[:, j * FF_CHUNK:(j + 1) * FF_CHUNK]), 0.0)
            acc = acc + _dot((h * h).astype(BF16), w2_ref[j * FF_CHUNK:(j + 1) * FF_CHUNK, :])
        out_ref[...] = acc

    first = pl.program_id(0) == 0
    pl.when(first)(lambda: body(True))
    pl.when(jnp.logical_not(first))(lambda: body(False))


def _proj_mlp(o, x, w_out, norm_w, w1, w2, layer):
    t, d = x.shape
    tm = ROW_TILE
    return pl.pallas_call(
        functools.partial(_proj_mlp_kernel, layer=layer),
        grid=(t // tm,),
        in_specs=[pl.BlockSpec((tm, o.shape[1]), lambda i: (i, 0)),
                  pl.BlockSpec((tm, d), lambda i: (i, 0)),
                  _const_spec(w_out.shape), _const_spec((1, d)),
                  pl.BlockSpec(memory_space=pl.ANY), pl.BlockSpec(memory_space=pl.ANY)],
        out_specs=pl.BlockSpec((tm, d), lambda i: (i, 0)),
        out_shape=jax.ShapeDtypeStruct((t, d), F32),
        scratch_shapes=[pltpu.VMEM(w1.shape[1:], BF16), pltpu.VMEM(w2.shape[1:], BF16),
                        pltpu.SemaphoreType.DMA((2, D_FF // FF_CHUNK))],
        compiler_params=pltpu.CompilerParams(dimension_semantics=("arbitrary",),
                                             vmem_limit_bytes=VMEM_LIMIT),
        name="proj_mlp",
    )(o, x, w_out, norm_w, w1, w2)


def _rope_tab_kernel(pos_ref, freq_ref, cos_ref, s1_ref, s2_ref):
    rows = pos_ref.shape[0]
    ang = pos_ref[...].astype(F32) * freq_ref[...]
    cos = jnp.cos(ang)
    sin = jnp.sin(ang)
    lane = lax.broadcasted_iota(jnp.int32, ang.shape, 1)
    first_half = (lane % MAP_DIM) < ROPE_HALF
    group = lane // ROPE_HALF

    def spread(x, i):
        y = jnp.where(group == i, x, 0.0)
        y = y + pltpu.roll(y, ROPE_HALF, 1)
        return y + pltpu.roll(y, 2 * ROPE_HALF, 1)

    for i in range(LANES // ROPE_HALF):
        out = slice(i * rows, (i + 1) * rows)
        sin_i = spread(sin, i)
        cos_ref[out, :] = spread(cos, i)
        s1_ref[out, :] = jnp.where(first_half, -sin_i, 0.0)
        s2_ref[out, :] = jnp.where(first_half, 0.0, sin_i)


def _rope_tab(pos_packed, freq_row):
    per_row = LANES // ROPE_HALF
    rows = ROPE_TILE // per_row
    t = pos_packed.shape[0] * per_row
    tab = jax.ShapeDtypeStruct((t, LANES), F32)
    spec = pl.BlockSpec((ROPE_TILE, LANES), lambda i: (i, 0))
    return pl.pallas_call(
        _rope_tab_kernel,
        grid=(t // ROPE_TILE,),
        in_specs=[pl.BlockSpec((rows, LANES), lambda i: (i, 0)), _const_spec((1, LANES))],
        out_specs=[spec, spec, spec],
        out_shape=[tab, tab, tab],
        compiler_params=pltpu.CompilerParams(dimension_semantics=("parallel",)),
        name="rope_tab",
    )(pos_packed, freq_row)


def _attn_in_kernel(x_ref, kvn_ref, qnw_ref, wkv_ref, wq_ref, kg_ref, qg_ref,
                    cos_ref, s1_ref, s2_ref, k_ref, v_ref, q_ref, kvx_ref, qx_ref):
    width = HEADS * HEAD_DIM
    cb = 2 * MXU_DIM
    xhat = _rms_hat(x_ref[...])
    kvx_ref[...] = (xhat * kvn_ref[...]).astype(BF16)
    qx_ref[...] = (xhat * qnw_ref[...]).astype(BF16)
    cos = cos_ref[...]
    s1 = s1_ref[...]
    s2 = s2_ref[...]
    r = lax.broadcasted_iota(jnp.int32, (MXU_DIM, MXU_DIM), 0) // MAP_DIM
    c = lax.broadcasted_iota(jnp.int32, (MXU_DIM, MXU_DIM), 1) // MAP_DIM
    group_ones = (r == c).astype(BF16)

    def norm_rope(raw, gain, scale, o_ref, col):
        for s in range(raw.shape[1] // MXU_DIM):
            blk = raw[:, s * MXU_DIM:(s + 1) * MXU_DIM]
            ss = _dot((blk * blk).astype(BF16), group_ones)
            lo = col + s * MXU_DIM
            nb = blk * lax.rsqrt(ss * (1.0 / MAP_DIM) + EPS) * gain[:, lo:lo + MXU_DIM]
            for hh in range(MXU_DIM // LANES):
                xb = nb[:, hh * LANES:(hh + 1) * LANES]
                rot = xb * cos + pltpu.roll(xb, LANES - ROPE_HALF, 1) * s1 + pltpu.roll(xb, ROPE_HALF, 1) * s2
                o_ref[:, lo + hh * LANES:lo + (hh + 1) * LANES] = (rot * scale).astype(o_ref.dtype)

    plan = []
    for col in range(0, width, cb):
        plan += [("k", col), ("q", col)]
    plan += [("v", col) for col in range(0, width, cb)]

    def project(item):
        kind, col = item
        if kind == "q":
            return _dot(qx_ref[...], wq_ref[:, col:col + cb])
        base = 0 if kind == "k" else width
        return _dot(kvx_ref[...], wkv_ref[:, base + col:base + col + cb])

    raw = project(plan[0])
    for i, (kind, col) in enumerate(plan):
        raw_next = project(plan[i + 1]) if i + 1 < len(plan) else None
        if kind == "v":
            v_ref[:, col:col + cb] = raw.astype(v_ref.dtype)
        elif kind == "k":
            norm_rope(raw, kg_ref[...], 1.0, k_ref, col)
        else:
            norm_rope(raw, qg_ref[...], MAP_DIM ** -0.5 * LOG2E, q_ref, col)
        raw = raw_next


def _attn_in(x, kv_norm, q_norm_w, w_kv, w_q, k_gain, q_gain, cos, s1, s2):
    t, d = x.shape
    tm = ROW_TILE
    width = HEADS * HEAD_DIM
    act = jax.ShapeDtypeStruct((t, width), BF16)
    row_spec = pl.BlockSpec((tm, width), lambda i: (i, 0))
    tab_spec = pl.BlockSpec((tm, LANES), lambda i: (i, 0))
    return pl.pallas_call(
        _attn_in_kernel,
        grid=(t // tm,),
        in_specs=[pl.BlockSpec((tm, d), lambda i: (i, 0)),
                  _const_spec((1, d)), _const_spec((1, d)),
                  _const_spec(w_kv.shape), _const_spec(w_q.shape),
                  _const_spec((1, width)), _const_spec((1, width)),
                  tab_spec, tab_spec, tab_spec],
        out_specs=[row_spec, row_spec, row_spec],
        out_shape=[act, act, act],
        scratch_shapes=[pltpu.VMEM((tm, d), BF16), pltpu.VMEM((tm, d), BF16)],
        compiler_params=pltpu.CompilerParams(dimension_semantics=("parallel",),
                                             vmem_limit_bytes=VMEM_LIMIT),
        name="attn_in",
    )(x, kv_norm, q_norm_w, w_kv, w_q, k_gain, q_gain, cos, s1, s2)


def _diff_attn_kernel(q_ref, k_ref, v_ref, qg_ref, kg_ref, lam_ref, snw_ref, o_ref,
                      m_ref, l_ref, acc_ref, *, tq, lam_init):
    n_q = q_ref.shape[0] // tq
    piece = ATTN_DIAG_PIECE
    lane = lax.broadcasted_iota(jnp.int32, (tq, HEAD_DIM), 1)
    lp = lam_ref[...]
    lam = (jnp.exp(jnp.sum(lp[0:1] * lp[1:2], axis=-1, keepdims=True))
           - jnp.exp(jnp.sum(lp[2:3] * lp[3:4], axis=-1, keepdims=True)) + lam_init)
    snw = snw_ref[...]

    def q_maps(rows):
        q = q_ref[rows, :]
        zero = jnp.zeros_like(q)
        return (jnp.where(lane < MAP_DIM, q, zero), jnp.where(lane < MAP_DIM, zero, q))

    def scores(qm, rows):
        ks = k_ref[rows, :]
        return [lax.dot_general(x, ks, NT_DIMS, preferred_element_type=F32) for x in qm]

    def finish(rows, acc0, acc1, l0, l1):
        o = acc0 / l0 - lam * (acc1 / l1)
        o_ref[rows, :] = (_rms_hat(o) * snw * (1.0 - lam_init)).astype(o_ref.dtype)

    bound = SCORE_BOUND_COEF * jnp.max(jnp.abs(qg_ref[...])) * jnp.max(jnp.abs(kg_ref[...]))
    bounded = bound <= SCORE_BOUND_LIMIT

    @pl.when(bounded)
    def _():
        def lane_sums(p):
            out = p[:, 0:LANES]
            for b in range(1, p.shape[1] // LANES):
                out = out + p[:, b * LANES:(b + 1) * LANES]
            return out

        def plus(a, b):
            return b if a is None else a + b

        for qi in range(n_q):
            q0 = qi * tq
            qm = q_maps(slice(q0, q0 + tq))
            acc = [None, None]
            lsum = [None, None]
            s_next = scores(qm, slice(0, tq)) if qi > 0 else None
            for j in range(qi):
                s = s_next
                s_next = scores(qm, slice((j + 1) * tq, (j + 2) * tq)) if j + 1 < qi else None
                vs = v_ref[j * tq:(j + 1) * tq, :]
                for mi in range(2):
                    p = jnp.exp2(s[mi])
                    lsum[mi] = plus(lsum[mi], lane_sums(p))
                    acc[mi] = plus(acc[mi], _dot(p.astype(BF16), vs))
            for bi in range(tq // piece):
                rows = slice(bi * piece, (bi + 1) * piece)
                ncols = (bi + 1) * piece
                sd = scores([x[rows, :] for x in qm], slice(q0, q0 + ncols))
                r = lax.broadcasted_iota(jnp.int32, (piece, ncols), 0) + bi * piece
                c = lax.broadcasted_iota(jnp.int32, (piece, ncols), 1)
                vs = v_ref[q0:q0 + ncols, :]
                fin = []
                for mi in range(2):
                    p = jnp.where(r >= c, jnp.exp2(sd[mi]), 0.0)
                    below_l = None if lsum[mi] is None else lsum[mi][rows, :]
                    below_a = None if acc[mi] is None else acc[mi][rows, :]
                    fin.append((plus(below_a, _dot(p.astype(BF16), vs)),
                                jnp.sum(plus(below_l, lane_sums(p)), axis=-1, keepdims=True)))
                finish(slice(q0 + bi * piece, q0 + (bi + 1) * piece),
                       fin[0][0], fin[1][0], fin[0][1], fin[1][1])

    @pl.when(jnp.logical_not(bounded))
    def _():
        r = lax.broadcasted_iota(jnp.int32, (tq, tq), 0)
        c = lax.broadcasted_iota(jnp.int32, (tq, tq), 1)

        def q_tile(qi, carry):
            q_rows = pl.ds(pl.multiple_of(qi * tq, tq), tq)
            qm = q_maps(q_rows)
            m_ref[...] = jnp.full(m_ref.shape, NEG_INF, F32)
            l_ref[...] = jnp.zeros(l_ref.shape, F32)
            acc_ref[...] = jnp.zeros(acc_ref.shape, F32)

            def step(j, masked):
                kv_rows = pl.ds(pl.multiple_of(j * tq, tq), tq)
                s = scores(qm, kv_rows)
                vs = v_ref[kv_rows, :]
                for mi in range(2):
                    sm = jnp.where(r >= c, s[mi], NEG_INF) if masked else s[mi]
                    m_old = m_ref[mi]
                    m_new = jnp.maximum(m_old, jnp.max(sm, axis=-1, keepdims=True))
                    alpha = jnp.exp2(m_old - m_new)
                    p = jnp.exp2(sm - m_new[:, 0:1])
                    l_ref[mi] = alpha * l_ref[mi] + jnp.sum(p, axis=-1, keepdims=True)
                    acc_ref[mi] = alpha * acc_ref[mi] + _dot(p.astype(BF16), vs)
                    m_ref[mi] = m_new

            def full_body(j, carry):
                step(j, False)
                return carry

            lax.fori_loop(0, qi, full_body, 0)
            step(qi, True)
            finish(q_rows, acc_ref[0], acc_ref[1], l_ref[0], l_ref[1])
            return carry

        lax.fori_loop(0, n_q, q_tile, 0)


def _diff_attn(q, k, v, q_gain, k_gain, lam_params, sub_norm, batch, seq_len, lam_init):
    t, width = q.shape
    tq = ATTN_TILE
    seq_spec = pl.BlockSpec((seq_len, HEAD_DIM), lambda b, h: (b, h))
    return pl.pallas_call(
        functools.partial(_diff_attn_kernel, tq=tq, lam_init=lam_init),
        grid=(batch, HEADS),
        in_specs=[seq_spec, seq_spec, seq_spec,
                  _const_spec(q_gain.shape), _const_spec(k_gain.shape),
                  _const_spec(lam_params.shape), _const_spec((1, HEAD_DIM))],
        out_specs=seq_spec,
        out_shape=jax.ShapeDtypeStruct((t, width), BF16),
        scratch_shapes=[pltpu.VMEM((2, tq, LANES), F32), pltpu.VMEM((2, tq, LANES), F32),
                        pltpu.VMEM((2, tq, HEAD_DIM), F32)],
        compiler_params=pltpu.CompilerParams(dimension_semantics=("parallel", "parallel"),
                                             vmem_limit_bytes=VMEM_LIMIT),
        name="diff_attn",
    )(q, k, v, q_gain, k_gain, lam_params, sub_norm)


def kernel(x, positions, a_norm, a_w_in, a_conv_w, a_a_log, a_dt_bias, a_out_norm, a_w_out,
           kv_norm, w_kv, k_norm, b_norm, b_w_q, b_q_norm, b_lambda, b_sub_norm, b_w_out,
           mlp_norm, mlp_w1, mlp_w2):
    batch, seq_len, d = x.shape
    assert d == D_MODEL and a_norm.shape[0] == 1 and b_norm.shape[0] == 1
    assert seq_len % ROW_TILE == 0 and seq_len % ATTN_TILE == 0 and seq_len % DELTA_CHUNK == 0
    assert (batch * seq_len) % ROPE_TILE == 0
    t = batch * seq_len
    width = HEADS * HEAD_DIM
    xf = x.reshape(t, d)

    w_in = a_w_in[0]
    pad = jnp.zeros((HEADS, 1), F32)
    alog16 = jnp.concatenate([pad, a_a_log[0].reshape(HEADS, 1)], axis=0)
    dtb16 = jnp.concatenate([pad, a_dt_bias[0].reshape(HEADS, 1)], axis=0)
    q, k, v, z, gates = _gdn_in(xf, a_norm[0].reshape(1, d), w_in.astype(BF16), a_conv_w[0],
                                alog16, dtb16, seq_len)
    o = _delta(q, k, v, z, gates, a_out_norm[0].reshape(1, HEAD_DIM), batch, seq_len)
    w1_all = mlp_w1.astype(BF16)
    w2_all = mlp_w2.astype(BF16)
    xf = _proj_mlp(o, xf, a_w_out[0].astype(BF16), mlp_norm[0].reshape(1, d), w1_all, w2_all, 0)

    half = ROPE_HALF
    freqs = ROPE_THETA ** (-jnp.arange(half, dtype=F32) / half)
    freq_row = jnp.tile(freqs, LANES // half).reshape(1, LANES)
    per_row = LANES // half
    pos_packed = jnp.repeat(
        positions.reshape(t // ROPE_TILE, per_row, ROPE_TILE // per_row).transpose(0, 2, 1)
        .reshape(t // per_row, per_row), half, axis=1)
    cos, s1, s2 = _rope_tab(pos_packed, freq_row)
    k_gain = jnp.tile(k_norm, width // MAP_DIM).reshape(1, width)
    q_gain = jnp.tile(b_q_norm[0], width // MAP_DIM).reshape(1, width)
    kr, vv, qr = _attn_in(xf, kv_norm.reshape(1, d), b_norm[0].reshape(1, d),
                          w_kv.astype(BF16), b_w_q[0].astype(BF16), k_gain, q_gain, cos, s1, s2)
    lam_init = 0.8 - 0.6 * math.exp(-0.3 * 1)
    oa = _diff_attn(qr, kr, vv, b_q_norm[0].reshape(1, MAP_DIM), k_norm.reshape(1, MAP_DIM),
                    b_lambda[0], b_sub_norm[0].reshape(1, HEAD_DIM), batch, seq_len, lam_init)
    xf = _proj_mlp(oa, xf, b_w_out[0].astype(BF16), mlp_norm[1].reshape(1, d), w1_all, w2_all, 1)
    return xf.reshape(batch, seq_len, d)
```

```python
import functools
import math

import jax
import jax.numpy as jnp
from jax import lax
from jax.experimental import pallas as pl
from jax.experimental.pallas import tpu as pltpu

F32 = jnp.float32
BF16 = jnp.bfloat16

D_MODEL = 1024
HEADS = 8
HEAD_DIM = 128
MAP_DIM = 64
ROPE_HALF = MAP_DIM // 2
CONV_WIDTH = 4
D_FF = 4 * D_MODEL
ROPE_THETA = 10000.0
EPS = 1e-6
NEG_INF = -1e30
LOG2E = math.log2(math.e)
SCORE_BOUND_COEF = MAP_DIM * MAP_DIM ** -0.5 * LOG2E * 1.02
SCORE_BOUND_LIMIT = 100.0

LANES = 128
SUBLANES = 8
MXU_DIM = 256

ROW_TILE = 512
GDN_ROW_TILE = 512
GDN_COL_BLOCK = 2 * MXU_DIM
FF_CHUNK = 1024
DELTA_CHUNK = 64
DELTA_GROUPS = (4,) * 8
DELTA_HEADS = 4
ATTN_TILE = 512
ATTN_DIAG_PIECE = 256
VMEM_LIMIT = 56 * 1024 * 1024

NT_DIMS = (((1,), (1,)), ((), ()))
TN_DIMS = (((0,), (0,)), ((), ()))


def _rms_hat(x):
    return x * lax.rsqrt(jnp.mean(x * x, axis=-1, keepdims=True) + EPS)


def _sigmoid(x):
    return 1.0 / (1.0 + jnp.exp(-x))


def _silu(x):
    h = 0.5 * x
    return h * jnp.tanh(h) + h


def _softplus(x):
    return jnp.maximum(x, 0.0) + jnp.log(1.0 + jnp.exp(-jnp.abs(x)))


def _dot(a, b):
    return jnp.dot(a, b, preferred_element_type=F32)


def _const_spec(shape):
    zeros = (0,) * len(shape)
    return pl.BlockSpec(shape, lambda *_: zeros, pipeline_mode=pl.Buffered(1))


def _gdn_in_kernel(x_ref, nw_ref, w_ref, cw_ref, alog_ref, dtb_ref,
                   q_ref, k_ref, v_ref, z_ref, gate_ref, xn_ref, tail_ref, *, tiles_per_seq):
    tm = x_ref.shape[0]
    width = HEADS * HEAD_DIM
    step = pl.program_id(0)

    @pl.when(step == 0)
    def _():
        tail_ref[...] = jnp.zeros(tail_ref.shape, F32)

    xn_ref[...] = (_rms_hat(x_ref[...]) * nw_ref[...]).astype(BF16)
    seq_start = (step % tiles_per_seq) == 0

    cb = GDN_COL_BLOCK
    blocks = 3 * width // cb

    def project(blk):
        return _dot(xn_ref[...], w_ref[:, blk * cb:(blk + 1) * cb])

    p = project(0)
    for blk in range(blocks):
        p_next = project(blk + 1) if blk + 1 < blocks else None
        cols = slice(blk * cb, (blk + 1) * cb)
        prev = jnp.where(seq_start, 0.0, tail_ref[:, cols])
        tail_ref[:, cols] = p[tm - SUBLANES:, :]
        xp = jnp.concatenate([prev, p], axis=0)
        cw = cw_ref[:, cols]
        c = cw[CONV_WIDTH - 1:CONV_WIDTH, :] * p
        for j in range(CONV_WIDTH - 1):
            c = c + cw[j:j + 1, :] * pltpu.roll(xp, CONV_WIDTH - 1 - j, 0)[SUBLANES:, :]
        kind, col = divmod(blk * cb, width)
        (q_ref, k_ref, v_ref)[kind][:, col:col + cb] = c
        p = p_next

    z_ref[...] = _dot(xn_ref[...], w_ref[:, 3 * width:4 * width])

    g_rows = _dot(xn_ref[...], w_ref[:, 4 * width:])
    eye_g = (lax.broadcasted_iota(jnp.int32, (2 * HEADS, 2 * HEADS), 0)
             == lax.broadcasted_iota(jnp.int32, (2 * HEADS, 2 * HEADS), 1)).astype(F32)
    gt = lax.dot_general(eye_g, g_rows, NT_DIMS, precision=lax.Precision.HIGHEST,
                         preferred_element_type=F32)
    beta = _sigmoid(gt)
    decay = -jnp.exp(alog_ref[...]) * _softplus(gt + dtb_ref[...])
    row = lax.broadcasted_iota(jnp.int32, gt.shape, 0)
    gate_ref[...] = jnp.where(row < HEADS, beta, decay)


def _gdn_in(x, norm_w, w_in, conv_w, alog16, dtb16, seq_len):
    t, d = x.shape
    tm = GDN_ROW_TILE
    width = HEADS * HEAD_DIM
    act = jax.ShapeDtypeStruct((t, width), F32)
    row_spec = pl.BlockSpec((tm, width), lambda i: (i, 0))
    return pl.pallas_call(
        functools.partial(_gdn_in_kernel, tiles_per_seq=seq_len // tm),
        grid=(t // tm,),
        in_specs=[
            pl.BlockSpec((tm, d), lambda i: (i, 0)),
            _const_spec((1, d)),
            _const_spec(w_in.shape),
            _const_spec(conv_w.shape),
            _const_spec(alog16.shape),
            _const_spec(dtb16.shape),
        ],
        out_specs=[row_spec, row_spec, row_spec, row_spec,
                   pl.BlockSpec((2 * HEADS, tm), lambda i: (0, i))],
        out_shape=[act, act, act, act, jax.ShapeDtypeStruct((2 * HEADS, t), F32)],
        scratch_shapes=[pltpu.VMEM((tm, d), BF16), pltpu.VMEM((SUBLANES, 3 * width), F32)],
        compiler_params=pltpu.CompilerParams(dimension_semantics=("arbitrary",),
                                             vmem_limit_bytes=VMEM_LIMIT),
        name="gdn_in",
    )(x, norm_w, w_in, conv_w, alog16, dtb16)


def _delta_kernel(q_ref, k_ref, v_ref, z_ref, beta_ref, g_ref, onw_ref, o_ref, gc_ref,
                  qw_a, b_a, op_a, cd_a, qw_b, b_b, op_b, cd_b, *, chunk, groups, heads):
    c_len = chunk
    bufs = ((qw_a, b_a, op_a, cd_a), (qw_b, b_b, op_b, cd_b))
    row = lax.broadcasted_iota(jnp.int32, (c_len, c_len), 0)
    col = lax.broadcasted_iota(jnp.int32, (c_len, c_len), 1)
    causal = row >= col
    strict = row > col
    eye = row == col
    upper = (row <= col).astype(F32)
    for hh in range(heads):
        gc_ref[hh] = jnp.dot(g_ref[hh], upper, precision=lax.Precision.HIGHEST,
                             preferred_element_type=F32)
    onw = onw_ref[...]
    n_steps = int(math.log2(c_len))

    def to_col(r):
        return jnp.sum(jnp.where(eye, r, 0.0), axis=1, keepdims=True)

    def chunk_rows(c):
        return slice(c * c_len, (c + 1) * c_len)

    def head_cols(hh):
        return slice(hh * HEAD_DIM, (hh + 1) * HEAD_DIM)

    def unit_rows(a, scale):
        return a * (lax.rsqrt(jnp.sum(a * a, axis=-1, keepdims=True) + EPS) * scale)

    def run(prep, adv, st):
        todo = list(range(adv[1])) if adv is not None else []

        def advance_one(st):
            if not todo:
                return st
            g = todo.pop(0)
            first, _, (qw_ref, b_ref, op_ref, cd_ref) = adv
            rows = chunk_rows(first + g)
            slots = [g * heads + hh for hh in range(heads)]
            res = [_dot(qw_ref[slots[hh]], st[hh].astype(BF16)) for hh in range(heads)]
            out = [res[hh][:c_len] + op_ref[slots[hh]] for hh in range(heads)]
            st = [cd_ref[slots[hh], 0:1, :] * st[hh] - res[hh][c_len:] + b_ref[slots[hh]]
                  for hh in range(heads)]
            for hh in range(heads):
                zc = z_ref[rows, head_cols(hh)]
                o_ref[rows, head_cols(hh)] = (_rms_hat(out[hh]) * onw * _silu(zc)).astype(o_ref.dtype)
            return st

        if prep is not None:
            first, size, (qw_ref, b_ref, op_ref, cd_ref) = prep
            chains = [(g, hh) for g in range(size) for hh in range(heads)]
            n = len(chains)
            cidx = [first + g for g, _ in chains]
            q = [unit_rows(_silu(q_ref[chunk_rows(cidx[i]), head_cols(chains[i][1])]),
                           HEAD_DIM ** -0.5) for i in range(n)]
            k = [unit_rows(_silu(k_ref[chunk_rows(cidx[i]), head_cols(chains[i][1])]), 1.0)
                 for i in range(n)]
            gc_r = [gc_ref[chains[i][1], pl.ds(cidx[i], 1), :] for i in range(n)]
            beta_c = [to_col(beta_ref[chains[i][1], pl.ds(cidx[i], 1), :]) for i in range(n)]
            gc_c = [to_col(r) for r in gc_r]
            gc_last = [r[:, c_len - 1:c_len] for r in gc_r]
            e_c = [jnp.exp(x) for x in gc_c]
            kb = [k[i] * beta_c[i] for i in range(n)]
            s = [lax.dot_general(jnp.concatenate([kb[i], q[i]], axis=0).astype(BF16),
                                 k[i].astype(BF16), NT_DIMS, preferred_element_type=F32)
                 for i in range(n)]
            st = advance_one(st)
            decay = [jnp.where(causal, jnp.exp(jnp.where(causal, gc_c[i] - gc_r[i], 0.0)), 0.0)
                     for i in range(n)]
            p = [jnp.where(strict, -(s[i][:c_len] * decay[i]), 0.0) for i in range(n)]
            intra = [s[i][c_len:] * decay[i] for i in range(n)]
            r = p
            p = [_dot(x.astype(BF16), x.astype(BF16)) for x in p]
            st = advance_one(st)
            for step in range(1, n_steps):
                last = step + 1 == n_steps
                p16 = [x.astype(BF16) for x in p]
                r16 = [x.astype(BF16) for x in r]
                t = [_dot(r16[i] if last else jnp.concatenate([p16[i], r16[i]], axis=0), p16[i])
                     for i in range(n)]
                r = [r[i] + p[i] + (t[i] if last else t[i][c_len:]) for i in range(n)]
                if not last:
                    p = [x[:c_len] for x in t]
                    st = advance_one(st)
            rhs = [jnp.concatenate([_silu(v_ref[chunk_rows(cidx[i]), head_cols(chains[i][1])]) * beta_c[i],
                                    kb[i] * e_c[i]], axis=1) for i in range(n)]
            y = [rhs[i] + _dot(r[i].astype(BF16), rhs[i].astype(BF16)) for i in range(n)]
            st = advance_one(st)
            y16 = [x.astype(BF16) for x in y]
            kd16 = [(k[i] * jnp.exp(gc_last[i] - gc_c[i])).astype(BF16) for i in range(n)]
            mb = [lax.dot_general(kd16[i], y16[i], TN_DIMS, preferred_element_type=F32)
                  for i in range(n)]
            iu = [_dot(intra[i].astype(BF16), y16[i]) for i in range(n)]
            st = advance_one(st)
            for i in range(n):
                b_ref[i] = mb[i][:, :HEAD_DIM]
                qw_ref[i, 0:c_len, :] = (q[i] * e_c[i] - iu[i][:, HEAD_DIM:]).astype(BF16)
                qw_ref[i, c_len:, :] = mb[i][:, HEAD_DIM:].astype(BF16)
                op_ref[i] = iu[i][:, :HEAD_DIM]
                cd_ref[i] = jnp.broadcast_to(jnp.exp(gc_last[i]), (SUBLANES, LANES))
        while todo:
            st = advance_one(st)
        return st

    firsts = [sum(groups[:j]) for j in range(len(groups))]
    plan = [(firsts[j], groups[j], bufs[j % 2]) for j in range(len(groups))]
    st = [jnp.zeros((HEAD_DIM, HEAD_DIM), F32) for _ in range(heads)]
    for j in range(len(plan) + 1):
        st = run(plan[j] if j < len(plan) else None, plan[j - 1] if j > 0 else None, st)


def _delta(q, k, v, z, gates, out_norm, batch, seq_len):
    t, width = q.shape
    c_len = DELTA_CHUNK
    nh = DELTA_HEADS
    n_chunks = seq_len // c_len
    assert sum(DELTA_GROUPS) == n_chunks and HEADS % nh == 0
    gates3 = gates.reshape(2 * HEADS, batch * n_chunks, c_len)
    seq_spec = pl.BlockSpec((seq_len, nh * HEAD_DIM), lambda b, h: (b, h))
    slots = max(DELTA_GROUPS) * nh
    buf_set = [pltpu.VMEM((slots, c_len + HEAD_DIM, HEAD_DIM), BF16),
               pltpu.VMEM((slots, HEAD_DIM, HEAD_DIM), F32),
               pltpu.VMEM((slots, c_len, HEAD_DIM), F32),
               pltpu.VMEM((slots, SUBLANES, LANES), F32)]
    return pl.pallas_call(
        functools.partial(_delta_kernel, chunk=c_len, groups=DELTA_GROUPS, heads=nh),
        grid=(batch, HEADS // nh),
        in_specs=[seq_spec, seq_spec, seq_spec, seq_spec,
                  pl.BlockSpec((nh, n_chunks, c_len), lambda b, h: (h, b, 0)),
                  pl.BlockSpec((nh, n_chunks, c_len), lambda b, h: (HEADS // nh + h, b, 0)),
                  pl.BlockSpec((1, HEAD_DIM), lambda b, h: (0, 0))],
        out_specs=seq_spec,
        out_shape=jax.ShapeDtypeStruct((t, width), BF16),
        scratch_shapes=[pltpu.VMEM((nh, n_chunks, c_len), F32)] + buf_set + buf_set,
        compiler_params=pltpu.CompilerParams(dimension_semantics=("parallel", "parallel"),
                                             vmem_limit_bytes=VMEM_LIMIT),
        name="delta",
    )(q, k, v, z, gates3, gates3, out_norm)


def _proj_mlp_kernel(o_ref, x_ref, wo_ref, nw_ref, w1_hbm, w2_hbm, out_ref, w1_ref, w2_ref, sem,
                     *, layer):
    chunks = D_FF // FF_CHUNK

    def copies(j):
        span = pl.ds(j * FF_CHUNK, FF_CHUNK)
        return (pltpu.make_async_copy(w1_hbm.at[layer, :, span], w1_ref.at[:, span], sem.at[0, j]),
                pltpu.make_async_copy(w2_hbm.at[layer, span, :], w2_ref.at[span, :], sem.at[1, j]))

    def body(fetch):
        if fetch:
            for j in range(chunks):
                for c in copies(j):
                    c.start()
        x1 = x_ref[...] + _dot(o_ref[...].astype(BF16), wo_ref[...])
        xn = (_rms_hat(x1) * nw_ref[...]).astype(BF16)
        acc = x1
        for j in range(chunks):
            if fetch:
                for c in copies(j):
                    c.wait()
            h = jnp.maximum(_dot(xn, w1_ref[:, j * FF_CHUNK:(j + 1) * FF_CHUNK]), 0.0)
            acc = acc + _dot((h * h).astype(BF16), w2_ref[j * FF_CHUNK:(j + 1) * FF_CHUNK, :])
        out_ref[...] = acc

    first = pl.program_id(0) == 0
    pl.when(first)(lambda: body(True))
    pl.when(jnp.logical_not(first))(lambda: body(False))


def _proj_mlp(o, x, w_out, norm_w, w1, w2, layer):
    t, d = x.shape
    tm = ROW_TILE
    return pl.pallas_call(
        functools.partial(_proj_mlp_kernel, layer=layer),
        grid=(t // tm,),
        in_specs=[pl.BlockSpec((tm, o.shape[1]), lambda i: (i, 0)),
                  pl.BlockSpec((tm, d), lambda i: (i, 0)),
                  _const_spec(w_out.shape), _const_spec((1, d)),
                  pl.BlockSpec(memory_space=pl.ANY), pl.BlockSpec(memory_space=pl.ANY)],
        out_specs=pl.BlockSpec((tm, d), lambda i: (i, 0)),
        out_shape=jax.ShapeDtypeStruct((t, d), F32),
        scratch_shapes=[pltpu.VMEM(w1.shape[1:], BF16), pltpu.VMEM(w2.shape[1:], BF16),
                        pltpu.SemaphoreType.DMA((2, D_FF // FF_CHUNK))],
        compiler_params=pltpu.CompilerParams(dimension_semantics=("arbitrary",),
                                             vmem_limit_bytes=VMEM_LIMIT),
        name="proj_mlp",
    )(o, x, w_out, norm_w, w1, w2)


def _rope_tab_kernel(pos_ref, freq_ref, cos_ref, s1_ref, s2_ref):
    rows = pos_ref.shape[0]
    ang = pos_ref[...].astype(F32) * freq_ref[...]
    cos = jnp.cos(ang)
    sin = jnp.sin(ang)
    lane = lax.broadcasted_iota(jnp.int32, ang.shape, 1)
    first_half = (lane % MAP_DIM) < ROPE_HALF
    group = lane // ROPE_HALF

    def spread(x, i):
        y = jnp.where(group == i, x, 0.0)
        y = y + pltpu.roll(y, ROPE_HALF, 1)
        return y + pltpu.roll(y, 2 * ROPE_HALF, 1)

    for i in range(LANES // ROPE_HALF):
        out = slice(i * rows, (i + 1) * rows)
        sin_i = spread(sin, i)
        cos_ref[out, :] = spread(cos, i)
        s1_ref[out, :] = jnp.where(first_half, -sin_i, 0.0)
        s2_ref[out, :] = jnp.where(first_half, 0.0, sin_i)


def _attn_in_kernel(x_ref, kvn_ref, qnw_ref, wkv_ref, wq_ref, kg_ref, qg_ref,
                    cos_ref, s1_ref, s2_ref, k_ref, v_ref, q_ref, kvx_ref, qx_ref):
    width = HEADS * HEAD_DIM
    cb = 2 * MXU_DIM
    xhat = _rms_hat(x_ref[...])
    kvx_ref[...] = (xhat * kvn_ref[...]).astype(BF16)
    qx_ref[...] = (xhat * qnw_ref[...]).astype(BF16)
    cos = cos_ref[...]
    s1 = s1_ref[...]
    s2 = s2_ref[...]
    r = lax.broadcasted_iota(jnp.int32, (MXU_DIM, MXU_DIM), 0) // MAP_DIM
    c = lax.broadcasted_iota(jnp.int32, (MXU_DIM, MXU_DIM), 1) // MAP_DIM
    group_ones = (r == c).astype(BF16)

    def norm_rope(raw, gain, scale, o_ref, col):
        for s in range(raw.shape[1] // MXU_DIM):
            blk = raw[:, s * MXU_DIM:(s + 1) * MXU_DIM]
            ss = _dot((blk * blk).astype(BF16), group_ones)
            lo = col + s * MXU_DIM
            nb = blk * lax.rsqrt(ss * (1.0 / MAP_DIM) + EPS) * gain[:, lo:lo + MXU_DIM]
            for hh in range(MXU_DIM // LANES):
                xb = nb[:, hh * LANES:(hh + 1) * LANES]
                rot = xb * cos + pltpu.roll(xb, LANES - ROPE_HALF, 1) * s1 + pltpu.roll(xb, ROPE_HALF, 1) * s2
                o_ref[:, lo + hh * LANES:lo + (hh + 1) * LANES] = (rot * scale).astype(o_ref.dtype)

    plan = []
    for col in range(0, width, cb):
        plan += [("k", col), ("q", col)]
    plan += [("v", col) for col in range(0, width, cb)]

    def project(item):
        kind, col = item
        if kind == "q":
            return _dot(qx_ref[...], wq_ref[:, col:col + cb])
        base = 0 if kind == "k" else width
        return _dot(kvx_ref[...], wkv_ref[:, base + col:base + col + cb])

    raw = project(plan[0])
    for i, (kind, col) in enumerate(plan):
        raw_next = project(plan[i + 1]) if i + 1 < len(plan) else None
        if kind == "v":
            v_ref[:, col:col + cb] = raw.astype(v_ref.dtype)
        elif kind == "k":
            norm_rope(raw, kg_ref[...], 1.0, k_ref, col)
        else:
            norm_rope(raw, qg_ref[...], MAP_DIM ** -0.5 * LOG2E, q_ref, col)
        raw = raw_next


def _attn_in_rope_kernel(x_ref, kvn_ref, qnw_ref, wkv_ref, wq_ref, kg_ref, qg_ref, pos_ref, freq_ref,
                         k_ref, v_ref, q_ref, kvx_ref, qx_ref, cos_ref, s1_ref, s2_ref):
    _rope_tab_kernel(pos_ref, freq_ref, cos_ref, s1_ref, s2_ref)
    _attn_in_kernel(x_ref, kvn_ref, qnw_ref, wkv_ref, wq_ref, kg_ref, qg_ref,
                    cos_ref, s1_ref, s2_ref, k_ref, v_ref, q_ref, kvx_ref, qx_ref)


def _attn_in(x, kv_norm, q_norm_w, w_kv, w_q, k_gain, q_gain, pos_packed, freq_row):
    t, d = x.shape
    tm = ROW_TILE
    width = HEADS * HEAD_DIM
    act = jax.ShapeDtypeStruct((t, width), BF16)
    row_spec = pl.BlockSpec((tm, width), lambda i: (i, 0))
    tab = pltpu.VMEM((tm, LANES), F32)
    return pl.pallas_call(
        _attn_in_rope_kernel,
        grid=(t // tm,),
        in_specs=[pl.BlockSpec((tm, d), lambda i: (i, 0)),
                  _const_spec((1, d)), _const_spec((1, d)),
                  _const_spec(w_kv.shape), _const_spec(w_q.shape),
                  _const_spec((1, width)), _const_spec((1, width)),
                  pl.BlockSpec((tm * ROPE_HALF // LANES, LANES), lambda i: (i, 0)),
                  _const_spec((1, LANES))],
        out_specs=[row_spec, row_spec, row_spec],
        out_shape=[act, act, act],
        scratch_shapes=[pltpu.VMEM((tm, d), BF16), pltpu.VMEM((tm, d), BF16), tab, tab, tab],
        compiler_params=pltpu.CompilerParams(dimension_semantics=("parallel",),
                                             vmem_limit_bytes=VMEM_LIMIT),
        name="attn_in",
    )(x, kv_norm, q_norm_w, w_kv, w_q, k_gain, q_gain, pos_packed, freq_row)


def _diff_attn_kernel(q_ref, k_ref, v_ref, qg_ref, kg_ref, lam_ref, snw_ref, o_ref,
                      m_ref, l_ref, acc_ref, *, tq, lam_init):
    n_q = q_ref.shape[0] // tq
    piece = ATTN_DIAG_PIECE
    lane = lax.broadcasted_iota(jnp.int32, (tq, HEAD_DIM), 1)
    lp = lam_ref[...]
    lam = (jnp.exp(jnp.sum(lp[0:1] * lp[1:2], axis=-1, keepdims=True))
           - jnp.exp(jnp.sum(lp[2:3] * lp[3:4], axis=-1, keepdims=True)) + lam_init)
    snw = snw_ref[...]

    def q_maps(rows):
        q = q_ref[rows, :]
        zero = jnp.zeros_like(q)
        return (jnp.where(lane < MAP_DIM, q, zero), jnp.where(lane < MAP_DIM, zero, q))

    def scores(qm, rows):
        ks = k_ref[rows, :]
        return [lax.dot_general(x, ks, NT_DIMS, preferred_element_type=F32) for x in qm]

    def finish(rows, acc0, acc1, l0, l1):
        o = acc0 / l0 - lam * (acc1 / l1)
        o_ref[rows, :] = (_rms_hat(o) * snw * (1.0 - lam_init)).astype(o_ref.dtype)

    bound = SCORE_BOUND_COEF * jnp.max(jnp.abs(qg_ref[...])) * jnp.max(jnp.abs(kg_ref[...]))
    bounded = bound <= SCORE_BOUND_LIMIT

    @pl.when(bounded)
    def _():
        def lane_sums(p):
            out = p[:, 0:LANES]
            for b in range(1, p.shape[1] // LANES):
                out = out + p[:, b * LANES:(b + 1) * LANES]
            return out

        def plus(a, b):
            return b if a is None else a + b

        for qi in range(n_q):
            q0 = qi * tq
            qm = q_maps(slice(q0, q0 + tq))
            acc = [None, None]
            lsum = [None, None]
            s_next = scores(qm, slice(0, tq)) if qi > 0 else None
            for j in range(qi):
                s = s_next
                s_next = scores(qm, slice((j + 1) * tq, (j + 2) * tq)) if j + 1 < qi else None
                vs = v_ref[j * tq:(j + 1) * tq, :]
                for mi in range(2):
                    p = jnp.exp2(s[mi])
                    lsum[mi] = plus(lsum[mi], lane_sums(p))
                    acc[mi] = plus(acc[mi], _dot(p.astype(BF16), vs))
            for bi in range(tq // piece):
                rows = slice(bi * piece, (bi + 1) * piece)
                ncols = (bi + 1) * piece
                sd = scores([x[rows, :] for x in qm], slice(q0, q0 + ncols))
                r = lax.broadcasted_iota(jnp.int32, (piece, ncols), 0) + bi * piece
                c = lax.broadcasted_iota(jnp.int32, (piece, ncols), 1)
                vs = v_ref[q0:q0 + ncols, :]
                fin = []
                for mi in range(2):
                    p = jnp.where(r >= c, jnp.exp2(sd[mi]), 0.0)
                    below_l = None if lsum[mi] is None else lsum[mi][rows, :]
                    below_a = None if acc[mi] is None else acc[mi][rows, :]
                    fin.append((plus(below_a, _dot(p.astype(BF16), vs)),
                                jnp.sum(plus(below_l, lane_sums(p)), axis=-1, keepdims=True)))
                finish(slice(q0 + bi * piece, q0 + (bi + 1) * piece),
                       fin[0][0], fin[1][0], fin[0][1], fin[1][1])

    @pl.when(jnp.logical_not(bounded))
    def _():
        r = lax.broadcasted_iota(jnp.int32, (tq, tq), 0)
        c = lax.broadcasted_iota(jnp.int32, (tq, tq), 1)

        def q_tile(qi, carry):
            q_rows = pl.ds(pl.multiple_of(qi * tq, tq), tq)
            qm = q_maps(q_rows)
            m_ref[...] = jnp.full(m_ref.shape, NEG_INF, F32)
            l_ref[...] = jnp.zeros(l_ref.shape, F32)
            acc_ref[...] = jnp.zeros(acc_ref.shape, F32)

            def step(j, masked):
                kv_rows = pl.ds(pl.multiple_of(j * tq, tq), tq)
                s = scores(qm, kv_rows)
                vs = v_ref[kv_rows, :]
                for mi in range(2):
                    sm = jnp.where(r >= c, s[mi], NEG_INF) if masked else s[mi]
                    m_old = m_ref[mi]
                    m_new = jnp.maximum(m_old, jnp.max(sm, axis=-1, keepdims=True))
                    alpha = jnp.exp2(m_old - m_new)
                    p = jnp.exp2(sm - m_new[:, 0:1])
                    l_ref[mi] = alpha * l_ref[mi] + jnp.sum(p, axis=-1, keepdims=True)
                    acc_ref[mi] = alpha * acc_ref[mi] + _dot(p.astype(BF16), vs)
                    m_ref[mi] = m_new

            def full_body(j, carry):
                step(j, False)
                return carry

            lax.fori_loop(0, qi, full_body, 0)
            step(qi, True)
            finish(q_rows, acc_ref[0], acc_ref[1], l_ref[0], l_ref[1])
            return carry

        lax.fori_loop(0, n_q, q_tile, 0)


def _diff_attn(q, k, v, q_gain, k_gain, lam_params, sub_norm, batch, seq_len, lam_init):
    t, width = q.shape
    tq = ATTN_TILE
    seq_spec = pl.BlockSpec((seq_len, HEAD_DIM), lambda b, h: (b, h))
    return pl.pallas_call(
        functools.partial(_diff_attn_kernel, tq=tq, lam_init=lam_init),
        grid=(batch, HEADS),
        in_specs=[seq_spec, seq_spec, seq_spec,
                  _const_spec(q_gain.shape), _const_spec(k_gain.shape),
                  _const_spec(lam_params.shape), _const_spec((1, HEAD_DIM))],
        out_specs=seq_spec,
        out_shape=jax.ShapeDtypeStruct((t, width), BF16),
        scratch_shapes=[pltpu.VMEM((2, tq, LANES), F32), pltpu.VMEM((2, tq, LANES), F32),
                        pltpu.VMEM((2, tq, HEAD_DIM), F32)],
        compiler_params=pltpu.CompilerParams(dimension_semantics=("parallel", "parallel"),
                                             vmem_limit_bytes=VMEM_LIMIT),
        name="diff_attn",
    )(q, k, v, q_gain, k_gain, lam_params, sub_norm)


def kernel(x, positions, a_norm, a_w_in, a_conv_w, a_a_log, a_dt_bias, a_out_norm, a_w_out,
           kv_norm, w_kv, k_norm, b_norm, b_w_q, b_q_norm, b_lambda, b_sub_norm, b_w_out,
           mlp_norm, mlp_w1, mlp_w2):
    batch, seq_len, d = x.shape
    assert d == D_MODEL and a_norm.shape[0] == 1 and b_norm.shape[0] == 1
    assert seq_len % ROW_TILE == 0 and seq_len % ATTN_TILE == 0 and seq_len % DELTA_CHUNK == 0
    t = batch * seq_len
    width = HEADS * HEAD_DIM
    xf = x.reshape(t, d)

    w_in = a_w_in[0]
    pad = jnp.zeros((HEADS, 1), F32)
    alog16 = jnp.concatenate([pad, a_a_log[0].reshape(HEADS, 1)], axis=0)
    dtb16 = jnp.concatenate([pad, a_dt_bias[0].reshape(HEADS, 1)], axis=0)
    q, k, v, z, gates = _gdn_in(xf, a_norm[0].reshape(1, d), w_in.astype(BF16), a_conv_w[0],
                                alog16, dtb16, seq_len)
    o = _delta(q, k, v, z, gates, a_out_norm[0].reshape(1, HEAD_DIM), batch, seq_len)
    w1_all = mlp_w1.astype(BF16)
    w2_all = mlp_w2.astype(BF16)
    xf = _proj_mlp(o, xf, a_w_out[0].astype(BF16), mlp_norm[0].reshape(1, d), w1_all, w2_all, 0)

    half = ROPE_HALF
    freqs = ROPE_THETA ** (-jnp.arange(half, dtype=F32) / half)
    freq_row = jnp.tile(freqs, LANES // half).reshape(1, LANES)
    per_row = LANES // half
    pos_packed = jnp.repeat(
        positions.reshape(t // ROW_TILE, per_row, ROW_TILE // per_row).transpose(0, 2, 1)
        .reshape(t // per_row, per_row), half, axis=1)
    k_gain = jnp.tile(k_norm, width // MAP_DIM).reshape(1, width)
    q_gain = jnp.tile(b_q_norm[0], width // MAP_DIM).reshape(1, width)
    kr, vv, qr = _attn_in(xf, kv_norm.reshape(1, d), b_norm[0].reshape(1, d),
                          w_kv.astype(BF16), b_w_q[0].astype(BF16), k_gain, q_gain, pos_packed, freq_row)
    lam_init = 0.8 - 0.6 * math.exp(-0.3 * 1)
    oa = _diff_attn(qr, kr, vv, b_q_norm[0].reshape(1, MAP_DIM), k_norm.reshape(1, MAP_DIM),
                    b_lambda[0], b_sub_norm[0].reshape(1, HEAD_DIM), batch, seq_len, lam_init)
    xf = _proj_mlp(oa, xf, b_w_out[0].astype(BF16), mlp_norm[1].reshape(1, d), w1_all, w2_all, 1)
    return xf.reshape(batch, seq_len, d)
```
